```python
import jax
import jax.numpy as jnp
from jax import lax
import numpy as np

D_MODEL = 2048
BATCH = 2
SEQ = 4096
DEPTH = 4

N_MIXERS = 4
NORM_EPS = 1e-6
NEG_INF = -1e30

RW_HEAD_DIM = 64
RW_HEADS = D_MODEL // RW_HEAD_DIM
RW_DECAY_LORA = max(32, int(round(1.8 * D_MODEL ** 0.5 / 32)) * 32)
RW_AAA_LORA = max(32, int(round(1.8 * D_MODEL ** 0.5 / 32)) * 32)
RW_GATE_LORA = max(32, int(round(0.6 * D_MODEL ** 0.8 / 32)) * 32)
RW_GN_EPS = 64e-5
RW_N_MIX = 6

RET_HEADS = D_MODEL // 256
RET_DK = D_MODEL // RET_HEADS
RET_DV = 2 * RET_DK
RET_CHUNK = 128
RET_ROT_BASE = 10000.0

GLA_HEADS = 4
GLA_DK = D_MODEL // 2 // GLA_HEADS
GLA_DV = D_MODEL // GLA_HEADS
GLA_GATE_RANK = 16
GLA_GATE_NORM = 16.0
GLA_CHUNK = 64

NSA_HEADS = 16
NSA_KV_HEADS = 4
NSA_HD = D_MODEL // NSA_HEADS
NSA_CMP_BLK = 32
NSA_CMP_STRIDE = 16
NSA_SEL_BLK = 64
NSA_SEL_TOPK = 16
NSA_WINDOW = 512
NSA_Q_BLK = 64
NSA_IN = NSA_HEADS * NSA_HD + 6 * NSA_KV_HEADS * NSA_HD + 3 * NSA_HEADS

N_EXPERTS = 32
TOP_K = 4
D_EXPERT = 768
SWIGLU_ALPHA = 1.702
SWIGLU_LIMIT = 7.0
MOE_BLK = 128

kernel_name = "hybrid_rwkv7_retnet_gla_nsa_moe"


def _rms_norm(x, g, eps=NORM_EPS):
    xf = x.astype(jnp.float32)
    y = xf * lax.rsqrt(jnp.mean(xf * xf, axis=-1, keepdims=True) + eps)
    return (y * g.astype(jnp.float32)).astype(x.dtype)


def _group_norm(x, g, b, eps):
    xf = x.astype(jnp.float32)
    mu = jnp.mean(xf, axis=-1, keepdims=True)
    var = jnp.mean(jnp.square(xf - mu), axis=-1, keepdims=True)
    return (xf - mu) * lax.rsqrt(var + eps) * g.astype(jnp.float32) + b.astype(jnp.float32)


def _token_shift(x):
    return jnp.pad(x[:, :-1], ((0, 0), (1, 0), (0, 0)))


def _rotate(z, cos, sin):
    half = z.shape[-1] // 2
    z1, z2 = z[..., :half], z[..., half:]
    return jnp.concatenate([z1 * cos - z2 * sin, z1 * sin + z2 * cos], axis=-1)


def rwkv7_time_mix(h, mu, w_rkv, w0, w1, w2, a0, a1, a2, g1, g2, k_k, k_a, r_k, ln_g, ln_b, w_o):
    B, T, D = h.shape
    H, N = RW_HEADS, RW_HEAD_DIM
    f32 = jnp.float32
    xm = h[:, :, None, :] + (_token_shift(h) - h)[:, :, None, :] * mu
    rkv = jnp.einsum('btjd,jde->btje', xm[:, :, :3], w_rkv)
    r, k, v = rkv[:, :, 0], rkv[:, :, 1], rkv[:, :, 2]
    xw, xa, xg = xm[:, :, 3], xm[:, :, 4], xm[:, :, 5]
    w_log = -jax.nn.softplus(-(w0 + jnp.tanh(xw @ w1) @ w2).astype(f32)) - 0.5
    decay = jnp.exp(-jnp.exp(w_log))
    a = jax.nn.sigmoid((a0 + (xa @ a1) @ a2).astype(f32))
    g = jax.nn.sigmoid(xg @ g1) @ g2
    kk = (k * k_k).astype(f32).reshape(B, T, H, N)
    kk = kk / jnp.maximum(jnp.linalg.norm(kk, axis=-1, keepdims=True), 1e-12)
    k = k.astype(f32) * (1.0 + (a - 1.0) * k_a.astype(f32))

    def heads(z):
        return z.astype(f32).reshape(B, T, H, N)

    r_h, k_h, v_h, a_h, w_h = heads(r), heads(k), heads(v), heads(a), heads(decay)

    def step(S, inp):
        r_t, w_t, k_t, v_t, kk_t, b_t = inp
        sa = jnp.einsum('bhij,bhj->bhi', S, -kk_t)
        S = S * w_t[:, :, None, :] + sa[..., None] * b_t[:, :, None, :] + v_t[..., None] * k_t[:, :, None, :]
        return S, jnp.einsum('bhij,bhj->bhi', S, r_t)

    def seq(z):
        return jnp.moveaxis(z, 1, 0)

    S0 = jnp.zeros((B, H, N, N), f32)
    _, y = lax.scan(step, S0, (seq(r_h), seq(w_h), seq(k_h), seq(v_h), seq(kk), seq(kk * a_h)))
    y = jnp.moveaxis(y, 0, 1)
    y = _group_norm(y, ln_g.reshape(H, N), ln_b.reshape(H, N), RW_GN_EPS)
    bonus = jnp.sum(r_h * k_h * r_k.astype(f32), axis=-1, keepdims=True) * v_h
    out = ((y + bonus).reshape(B, T, D) * g.astype(f32)).astype(h.dtype)
    return out @ w_o


def retention_mix(h, w_in, gn_g, gn_b, w_o):
    B, T, D = h.shape
    H, dk, dv, C = RET_HEADS, RET_DK, RET_DV, RET_CHUNK
    nc = T // C
    f32 = jnp.float32
    q, k, v, g = jnp.split(h @ w_in, [H * dk, 2 * H * dk, 2 * H * dk + H * dv], axis=-1)
    q = q.astype(f32).reshape(B, T, H, dk)
    k = k.astype(f32).reshape(B, T, H, dk) * (dk ** -0.5)
    v = v.astype(f32).reshape(B, T, H, dv)
    theta = 1.0 / (RET_ROT_BASE ** jnp.linspace(0.0, 1.0, dk // 2, dtype=f32))
    ang = jnp.arange(T, dtype=f32)[:, None] * theta[None, :]
    cos, sin = jnp.cos(ang)[:, None, :], jnp.sin(ang)[:, None, :]
    q, k = _rotate(q, cos, sin), _rotate(k, cos, sin)
    log_gamma = jnp.log(1.0 - 2.0 ** (-5.0 - jnp.arange(H, dtype=f32)))

    def chunks(z):
        return z.reshape(B, nc, C, H, -1).transpose(0, 3, 1, 2, 4)

    qc, kc, vc = chunks(q), chunks(k), chunks(v)
    pos = jnp.arange(C, dtype=f32)
    rel = pos[:, None] - pos[None, :]
    dmask = jnp.where(rel >= 0, jnp.exp(jnp.maximum(rel, 0.0) * log_gamma[:, None, None]), 0.0)
    scores = jnp.einsum('bhncd,bhnsd->bhncs', qc, kc) * dmask[None, :, None]
    inner = jnp.einsum('bhncs,bhnse->bhnce', scores, vc)
    xi = jnp.exp((pos + 1.0)[None, :] * log_gamma[:, None])[..., None]
    zeta = jnp.exp((C - 1.0 - pos)[None, :] * log_gamma[:, None])[..., None]
    g_chunk = jnp.exp(C * log_gamma)[:, None, None]

    def step(R, inp):
        q_i, k_i, v_i = inp
        cross = jnp.einsum('bhcd,bhde->bhce', q_i * xi, R)
        R = g_chunk * R + jnp.einsum('bhcd,bhce->bhde', k_i * zeta, v_i)
        return R, cross

    R0 = jnp.zeros((B, H, dk, dv), f32)
    _, cross = lax.scan(step, R0, (jnp.moveaxis(qc, 2, 0), jnp.moveaxis(kc, 2, 0), jnp.moveaxis(vc, 2, 0)))
    o = inner + jnp.moveaxis(cross, 0, 2)
    o = o.transpose(0, 2, 3, 1, 4).reshape(B, T, H, dv)
    o = _group_norm(o, gn_g.reshape(H, dv), gn_b.reshape(H, dv), NORM_EPS)
    o = o.reshape(B, T, H * dv) * jax.nn.silu(g.astype(f32))
    return o.astype(h.dtype) @ w_o


def gla_mix(h, w_in, w_a1, w_a2, b_a, gn_g, w_o):
    B, T, D = h.shape
    H, dk, dv, C = GLA_HEADS, GLA_DK, GLA_DV, GLA_CHUNK
    nc = T // C
    f32 = jnp.float32
    q, k, v, g = jnp.split(h @ w_in, [H * dk, 2 * H * dk, 2 * H * dk + H * dv], axis=-1)
    log_a = jax.nn.log_sigmoid(((h @ w_a1) @ w_a2 + b_a).astype(f32)) / GLA_GATE_NORM

    def chunks(z):
        return z.astype(f32).reshape(B, nc, C, H, -1).transpose(0, 3, 1, 2, 4)

    qc = chunks(q) * (dk ** -0.5)
    kc, vc, la = chunks(k), chunks(v), chunks(log_a)
    b = jnp.cumsum(la, axis=3)
    b_last = b[:, :, :, -1:, :]
    q_in = qc * jnp.exp(b)
    k_in = kc * jnp.exp(-b)
    causal = jnp.tril(jnp.ones((C, C), dtype=bool))
    att = jnp.where(causal, jnp.einsum('bhncd,bhnsd->bhncs', q_in, k_in), 0.0)
    inner = jnp.einsum('bhncs,bhnse->bhnce', att, vc)
    k_st = kc * jnp.exp(b_last - b)
    d_last = jnp.exp(b[:, :, :, -1, :])

    def step(S, inp):
        q_i, k_i, v_i, d_i = inp
        cross = jnp.einsum('bhcd,bhde->bhce', q_i, S)
        S = d_i[..., None] * S + jnp.einsum('bhcd,bhce->bhde', k_i, v_i)
        return S, cross

    S0 = jnp.zeros((B, H, dk, dv), f32)
    xs = (jnp.moveaxis(q_in, 2, 0), jnp.moveaxis(k_st, 2, 0), jnp.moveaxis(vc, 2, 0), jnp.moveaxis(d_last, 2, 0))
    _, cross = lax.scan(step, S0, xs)
    o = inner + jnp.moveaxis(cross, 0, 2)
    o = o.transpose(0, 2, 3, 1, 4).reshape(B, T, H, dv)
    o = _rms_norm(o, gn_g).reshape(B, T, H * dv) * jax.nn.silu(g.astype(f32))
    return o.astype(h.dtype) @ w_o


def nsa_mix(h, w_in, q_g, k_g, cmp_pos, cmp_w1, cmp_w2, w_o):
    B, T, D = h.shape
    H, G, hd = NSA_HEADS, NSA_KV_HEADS, NSA_HD
    hpg = H // G
    L, S, Ls, W, Qb = NSA_CMP_BLK, NSA_CMP_STRIDE, NSA_SEL_BLK, NSA_WINDOW, NSA_Q_BLK
    f32 = jnp.float32
    kvw = G * hd
    cuts = [H * hd + i * kvw for i in range(7)]
    q, k_cmp, v_cmp, k_sel, v_sel, k_win, v_win, gate = jnp.split(h @ w_in, cuts, axis=-1)
    scale = hd ** -0.5
    q = _rms_norm(q.reshape(B, T, H, hd), q_g).astype(f32)
    q = q.reshape(B, T, G, hpg, hd).transpose(0, 2, 3, 1, 4) * scale

    def kvh(z):
        return z.reshape(B, T, G, hd)

    n_cmp = (T - L) // S + 1
    blk_idx = jnp.arange(n_cmp)[:, None] * S + jnp.arange(L)[None, :]

    def compress(z, pos, w1, w2):
        blocks = kvh(z)[:, blk_idx] + pos[:, None, :]
        flat = blocks.transpose(0, 1, 3, 2, 4).reshape(B, n_cmp, G, L * hd)
        return (jax.nn.silu(flat @ w1) @ w2).transpose(0, 2, 1, 3)

    kc = _rms_norm(compress(k_cmp, cmp_pos[0], cmp_w1[0], cmp_w2[0]), k_g[0]).astype(f32)
    vc = compress(v_cmp, cmp_pos[1], cmp_w1[1], cmp_w2[1]).astype(f32)
    t_pos = jnp.arange(T)
    cmp_start = jnp.arange(n_cmp) * S
    cmp_end = cmp_start + L - 1
    cmp_valid = cmp_end[None, :] <= t_pos[:, None]
    s_cmp = jnp.einsum('bghtd,bgnd->bghtn', q, kc)
    p_cmp = jax.nn.softmax(jnp.where(cmp_valid, s_cmp, NEG_INF), axis=-1) * cmp_valid
    o_cmp = jnp.einsum('bghtn,bgnd->bghtd', p_cmp, vc)

    n_slc = T // Ls
    slc_start = jnp.arange(n_slc) * Ls
    overlap = ((cmp_start[:, None] <= slc_start[None, :] + Ls - 1) & (cmp_end[:, None] >= slc_start[None, :])).astype(f32)
    imp = jnp.einsum('bghtn,nj->bgtj', p_cmp, overlap)
    cur = t_pos // Ls
    j = jnp.arange(n_slc)
    forced = (j[None, :] == 0) | (j[None, :] == cur[:, None]) | (j[None, :] == cur[:, None] - 1)
    future = j[None, :] > cur[:, None]
    imp = jnp.where(future, -jnp.inf, jnp.where(forced, jnp.inf, imp))
    n_top = min(NSA_SEL_TOPK, n_slc)
    _, sel_idx = lax.top_k(imp, n_top)

    def blocks_of(z):
        return z.reshape(B, n_slc, Ls, G, hd).transpose(0, 3, 1, 2, 4)

    k_sb = blocks_of(_rms_norm(kvh(k_sel), k_g[1]).astype(f32))
    v_sb = blocks_of(kvh(v_sel).astype(f32))

    def pad_front(z):
        return jnp.pad(z.transpose(0, 2, 1, 3), ((0, 0), (0, 0), (W, 0), (0, 0)))

    k_wp = pad_front(_rms_norm(kvh(k_win), k_g[2]).astype(f32))
    v_wp = pad_front(kvh(v_win).astype(f32))
    nqb = T // Qb
    q_blocks = jnp.moveaxis(q.reshape(B, G, hpg, nqb, Qb, hd), 3, 0)
    idx_blocks = jnp.moveaxis(sel_idx.reshape(B, G, nqb, Qb, n_top), 2, 0)
    b_ix = jnp.arange(B)[:, None, None, None]
    g_ix = jnp.arange(G)[None, :, None, None]
    offs = jnp.arange(Ls)
    w_offs = jnp.arange(W + Qb)

    def attend(args):
        qb, ib, bi = args
        t_q = bi * Qb + jnp.arange(Qb)
        kg = k_sb[b_ix, g_ix, ib].reshape(B, G, Qb, n_top * Ls, hd)
        vg = v_sb[b_ix, g_ix, ib].reshape(B, G, Qb, n_top * Ls, hd)
        kpos = (ib[..., None] * Ls + offs).reshape(B, G, Qb, n_top * Ls)
        m_sel = (kpos <= t_q[:, None])[:, :, None]
        s = jnp.einsum('bghqd,bgqkd->bghqk', qb, kg)
        o_sel = jnp.einsum('bghqk,bgqkd->bghqd', jax.nn.softmax(jnp.where(m_sel, s, NEG_INF), axis=-1), vg)
        start = bi * Qb
        kw = lax.dynamic_slice_in_dim(k_wp, start, W + Qb, axis=2)
        vw = lax.dynamic_slice_in_dim(v_wp, start, W + Qb, axis=2)
        wpos = start - W + w_offs
        m_win = (wpos[None, :] <= t_q[:, None]) & (wpos[None, :] > t_q[:, None] - W) & (wpos[None, :] >= 0)
        s = jnp.einsum('bghqd,bgkd->bghqk', qb, kw)
        o_win = jnp.einsum('bghqk,bgkd->bghqd', jax.nn.softmax(jnp.where(m_win, s, NEG_INF), axis=-1), vw)
        return o_sel, o_win

    o_sel, o_win = lax.map(attend, (q_blocks, idx_blocks, jnp.arange(nqb)))

    def unblock(z):
        return jnp.moveaxis(z, 0, 3).reshape(B, G, hpg, T, hd)

    gt = jax.nn.sigmoid(gate.astype(f32)).reshape(B, T, 3, G, hpg).transpose(2, 0, 3, 4, 1)[..., None]
    o = gt[0] * o_cmp + gt[1] * unblock(o_sel) + gt[2] * unblock(o_win)
    o = o.transpose(0, 3, 1, 2, 4).reshape(B, T, H * hd)
    return o.astype(h.dtype) @ w_o


def moe_ffn(h, w_router, b_router, w_gu, b_gu, w_down, b_down):
    B, T, D = h.shape
    N = B * T
    x = h.reshape(N, D)
    logits = (x @ w_router + b_router).astype(jnp.float32)
    top_v, top_e = lax.top_k(logits, TOP_K)
    top_w = jax.nn.softmax(top_v, axis=-1)
    e_flat = top_e.reshape(-1)
    w_flat = top_w.reshape(-1)
    t_flat = jnp.repeat(jnp.arange(N, dtype=jnp.int32), TOP_K)
    order = jnp.argsort(e_flat)
    e_s, t_s, w_s = e_flat[order], t_flat[order], w_flat[order]
    counts = jnp.zeros((N_EXPERTS,), jnp.int32).at[e_flat].add(1)
    padded = (counts + MOE_BLK - 1) // MOE_BLK * MOE_BLK
    g_start = jnp.cumsum(counts) - counts
    p_end = jnp.cumsum(padded)
    p_start = p_end - padded
    dest = p_start[e_s] + (jnp.arange(N * TOP_K, dtype=jnp.int32) - g_start[e_s])
    n_blocks = (N * TOP_K + MOE_BLK - 1) // MOE_BLK + N_EXPERTS
    cap = n_blocks * MOE_BLK
    buf_tok = jnp.full((cap,), N, jnp.int32).at[dest].set(t_s)
    buf_w = jnp.zeros((cap,), jnp.float32).at[dest].set(w_s)
    blk_e = jnp.minimum(jnp.searchsorted(p_end, jnp.arange(n_blocks, dtype=jnp.int32) * MOE_BLK, side='right'), N_EXPERTS - 1)
    x_pad = jnp.concatenate([x, jnp.zeros((1, D), x.dtype)], axis=0)

    def expert_block(args):
        tok, e = args
        xb = x_pad[tok]
        gu = xb @ w_gu[e] + b_gu[e]
        x_glu, x_lin = jnp.split(gu, 2, axis=-1)
        x_glu = jnp.minimum(x_glu, SWIGLU_LIMIT)
        x_lin = jnp.clip(x_lin, -SWIGLU_LIMIT, SWIGLU_LIMIT)
        act = x_glu * jax.nn.sigmoid(SWIGLU_ALPHA * x_glu) * (x_lin + 1.0)
        return act @ w_down[e] + b_down[e]

    out = lax.map(expert_block, (buf_tok.reshape(n_blocks, MOE_BLK), blk_e))
    out = out.reshape(cap, D) * buf_w[:, None]
    y = jnp.zeros((N + 1, D), out.dtype).at[buf_tok].add(out)[:N]
    return y.reshape(B, T, D).astype(h.dtype)


def setup_inputs(seed: int = 0) -> dict:
    key = jax.random.key(seed)
    ks = iter(jax.random.split(key, 64))
    f32 = jnp.float32
    D = D_MODEL

    def nrm(shape, std):
        return jax.random.normal(next(ks), shape, f32) * std

    def gain(shape):
        return 1.0 + nrm(shape, 0.02)

    nA = len(range(0, DEPTH, N_MIXERS))
    nB = len(range(1, DEPTH, N_MIXERS))
    nC = len(range(2, DEPTH, N_MIXERS))
    nD = len(range(3, DEPTH, N_MIXERS))
    gla_in = GLA_HEADS * (2 * GLA_DK + 2 * GLA_DV)
    ret_in = RET_HEADS * (2 * RET_DK + 2 * RET_DV)
    inp = {}
    inp['x'] = nrm((BATCH, SEQ, D), 1.0)
    inp['c'] = nrm((BATCH, D), 1.0)
    inp['ada_w'] = nrm((DEPTH, D, 6 * D), 0.5 * D ** -0.5)
    inp['ada_b'] = nrm((DEPTH, 6 * D), 0.02)
    inp['norm_g'] = gain((DEPTH, 2, D))
    inp['rw_mu'] = jax.random.uniform(next(ks), (nA, RW_N_MIX, D), f32)
    inp['rw_w_rkv'] = nrm((nA, 3, D, D), D ** -0.5)
    inp['rw_w0'] = jax.random.uniform(next(ks), (nA, D), f32, -6.5, -1.5)
    inp['rw_w1'] = nrm((nA, D, RW_DECAY_LORA), D ** -0.5)
    inp['rw_w2'] = nrm((nA, RW_DECAY_LORA, D), 0.1 * RW_DECAY_LORA ** -0.5)
    inp['rw_a0'] = nrm((nA, D), 0.1)
    inp['rw_a1'] = nrm((nA, D, RW_AAA_LORA), D ** -0.5)
    inp['rw_a2'] = nrm((nA, RW_AAA_LORA, D), 0.1 * RW_AAA_LORA ** -0.5)
    inp['rw_g1'] = nrm((nA, D, RW_GATE_LORA), D ** -0.5)
    inp['rw_g2'] = nrm((nA, RW_GATE_LORA, D), RW_GATE_LORA ** -0.5)
    inp['rw_k_k'] = 0.85 + nrm((nA, D), 0.02)
    inp['rw_k_a'] = gain((nA, D))
    inp['rw_r_k'] = nrm((nA, RW_HEADS, RW_HEAD_DIM), 0.1)
    inp['rw_ln_g'] = gain((nA, D))
    inp['rw_ln_b'] = nrm((nA, D), 0.02)
    inp['rw_w_o'] = nrm((nA, D, D), D ** -0.5)
    inp['ret_w_in'] = nrm((nB, D, ret_in), D ** -0.5)
    inp['ret_gn_g'] = gain((nB, RET_HEADS * RET_DV))
    inp['ret_gn_b'] = nrm((nB, RET_HEADS * RET_DV), 0.02)
    inp['ret_w_o'] = nrm((nB, RET_HEADS * RET_DV, D), (RET_HEADS * RET_DV) ** -0.5)
    inp['gla_w_in'] = nrm((nC, D, gla_in), D ** -0.5)
    inp['gla_w_a1'] = nrm((nC, D, GLA_GATE_RANK), D ** -0.5)
    inp['gla_w_a2'] = nrm((nC, GLA_GATE_RANK, GLA_HEADS * GLA_DK), GLA_GATE_RANK ** -0.5)
    inp['gla_b_a'] = nrm((nC, GLA_HEADS * GLA_DK), 0.1)
    inp['gla_gn_g'] = gain((nC, GLA_DV))
    inp['gla_w_o'] = nrm((nC, GLA_HEADS * GLA_DV, D), (GLA_HEADS * GLA_DV) ** -0.5)
    inp['nsa_w_in'] = nrm((nD, D, NSA_IN), D ** -0.5)
    inp['nsa_q_g'] = gain((nD, NSA_HD))
    inp['nsa_k_g'] = gain((nD, 3, NSA_HD))
    inp['nsa_cmp_pos'] = nrm((nD, 2, NSA_CMP_BLK, NSA_HD), 0.1)
    inp['nsa_cmp_w1'] = nrm((nD, 2, NSA_CMP_BLK * NSA_HD, NSA_HD), (NSA_CMP_BLK * NSA_HD) ** -0.5)
    inp['nsa_cmp_w2'] = nrm((nD, 2, NSA_HD, NSA_HD), NSA_HD ** -0.5)
    inp['nsa_w_o'] = nrm((nD, NSA_HEADS * NSA_HD, D), (NSA_HEADS * NSA_HD) ** -0.5)
    inp['moe_router_w'] = nrm((DEPTH, D, N_EXPERTS), D ** -0.5)
    inp['moe_router_b'] = nrm((DEPTH, N_EXPERTS), 0.01)
    inp['moe_w_gu'] = nrm((DEPTH, N_EXPERTS, D, 2 * D_EXPERT), D ** -0.5)
    inp['moe_b_gu'] = nrm((DEPTH, N_EXPERTS, 2 * D_EXPERT), 0.02)
    inp['moe_w_down'] = nrm((DEPTH, N_EXPERTS, D_EXPERT, D), D_EXPERT ** -0.5)
    inp['moe_b_down'] = nrm((DEPTH, N_EXPERTS, D), 0.02)
    return inp


def reference(x, c, ada_w, ada_b, norm_g,
              rw_mu, rw_w_rkv, rw_w0, rw_w1, rw_w2, rw_a0, rw_a1, rw_a2, rw_g1, rw_g2,
              rw_k_k, rw_k_a, rw_r_k, rw_ln_g, rw_ln_b, rw_w_o,
              ret_w_in, ret_gn_g, ret_gn_b, ret_w_o,
              gla_w_in, gla_w_a1, gla_w_a2, gla_b_a, gla_gn_g, gla_w_o,
              nsa_w_in, nsa_q_g, nsa_k_g, nsa_cmp_pos, nsa_cmp_w1, nsa_cmp_w2, nsa_w_o,
              moe_router_w, moe_router_b, moe_w_gu, moe_b_gu, moe_w_down, moe_b_down):
    c_act = jax.nn.silu(c)
    for i in range(DEPTH):
        mod = c_act @ ada_w[i] + ada_b[i]
        sh1, sc1, gt1, sh2, sc2, gt2 = jnp.split(mod, 6, axis=-1)
        h = _rms_norm(x, norm_g[i, 0]) * (1.0 + sc1[:, None]) + sh1[:, None]
        m, j = i % N_MIXERS, i // N_MIXERS
        if m == 0:
            y = rwkv7_time_mix(h, rw_mu[j], rw_w_rkv[j], rw_w0[j], rw_w1[j], rw_w2[j], rw_a0[j], rw_a1[j],
                               rw_a2[j], rw_g1[j], rw_g2[j], rw_k_k[j], rw_k_a[j], rw_r_k[j],
                               rw_ln_g[j], rw_ln_b[j], rw_w_o[j])
        elif m == 1:
            y = retention_mix(h, ret_w_in[j], ret_gn_g[j], ret_gn_b[j], ret_w_o[j])
        elif m == 2:
            y = gla_mix(h, gla_w_in[j], gla_w_a1[j], gla_w_a2[j], gla_b_a[j], gla_gn_g[j], gla_w_o[j])
        else:
            y = nsa_mix(h, nsa_w_in[j], nsa_q_g[j], nsa_k_g[j], nsa_cmp_pos[j], nsa_cmp_w1[j],
                        nsa_cmp_w2[j], nsa_w_o[j])
        x = x + gt1[:, None] * y
        h = _rms_norm(x, norm_g[i, 1]) * (1.0 + sc2[:, None]) + sh2[:, None]
        x = x + gt2[:, None] * moe_ffn(h, moe_router_w[i], moe_router_b[i], moe_w_gu[i], moe_b_gu[i],
                                       moe_w_down[i], moe_b_down[i])
    return x
```

```python
import functools
import math

import jax
import jax.numpy as jnp
from jax import lax
from jax.experimental import pallas as pl
from jax.experimental.pallas import tpu as pltpu

F32 = jnp.float32
BF16 = jnp.bfloat16
HIGHEST = lax.Precision.HIGHEST

NORM_EPS = 1e-6
NEG_INF = -1e30

RW_HEAD_DIM = 64
RW_GN_EPS = 64e-5
RW_CHUNK = 64
RW_HEADS_PER_STEP = 8

RET_HEADS = 8
RET_DK = 256
RET_DV = 512
RET_CHUNK = 128
RET_ROT_BASE = 10000.0

GLA_HEADS = 4
GLA_DK = 256
GLA_DV = 512
GLA_GATE_NORM = 16.0
GLA_CHUNK = 64

NSA_HEADS = 16
NSA_KV_HEADS = 4
NSA_HD = 128
NSA_CMP_BLK = 32
NSA_CMP_STRIDE = 16
NSA_SEL_BLK = 64
NSA_SEL_TOPK = 16
NSA_WINDOW = 512
NSA_CMP_TQ = 256
NSA_ATT_TQ = 128
NSA_ATT_TK = 256
IMP_BIG = 3e38

N_EXPERTS = 32
TOP_K = 4
D_EXPERT = 768
SWIGLU_ALPHA = 1.702
SWIGLU_LIMIT = 7.0
MOE_ROWS = 256

VMEM_LIMIT_BYTES = 52 * 1024 * 1024


def _params(*sem):
    return pltpu.CompilerParams(dimension_semantics=sem, vmem_limit_bytes=VMEM_LIMIT_BYTES)


def _bdot(a, b):
    return jnp.dot(a.astype(BF16), b.astype(BF16), preferred_element_type=F32)


def _bdot_nt(a, b):
    return lax.dot_general(a.astype(BF16), b.astype(BF16), (((1,), (1,)), ((), ())),
                           preferred_element_type=F32)


def _bdot_tn(a, b):
    return lax.dot_general(a.astype(BF16), b.astype(BF16), (((0,), (0,)), ((), ())),
                           preferred_element_type=F32)


def _fdot(a, b):
    return jnp.dot(a, b, precision=HIGHEST, preferred_element_type=F32)


def _fdot_nt(a, b):
    return lax.dot_general(a, b, (((1,), (1,)), ((), ())), precision=HIGHEST,
                           preferred_element_type=F32)


def _fdot_tn(a, b):
    return lax.dot_general(a, b, (((0,), (0,)), ((), ())), precision=HIGHEST,
                           preferred_element_type=F32)


def _mm_kernel(*refs, precise, has_bias, has_res):
    x_ref, w_ref = refs[0], refs[1]
    pos = 2
    if precise:
        acc = _fdot(x_ref[...].astype(F32), w_ref[...])
    else:
        acc = _bdot(x_ref[...], w_ref[...])
    if has_bias:
        acc = acc + refs[pos][...]
        pos += 1
    if has_res:
        acc = refs[pos][...] + refs[pos + 1][0] * acc
        pos += 2
    o_ref = refs[pos]
    o_ref[...] = acc.astype(o_ref.dtype)


def _matmul(x, w, lead=(), *, bias=None, res=None, gate=None, n_cols=None, tm=None, tn=None,
            precise=False, out_dtype=F32):
    M, K = x.shape
    N = n_cols if n_cols is not None else w.shape[-1]
    if tm is None:
        tm = min(M, 1024 if x.dtype == BF16 else 512)
    if tn is None:
        tn = min(N, 512) if N % 128 == 0 else N
    assert M % tm == 0
    nlead = len(lead)
    in_specs = [
        pl.BlockSpec((tm, K), lambda i, j: (i, 0)),
        pl.BlockSpec((None,) * nlead + (K, tn), lambda i, j: tuple(lead) + (0, j)),
    ]
    args = [x, w]
    if bias is not None:
        in_specs.append(pl.BlockSpec((1, tn), lambda i, j: (0, j)))
        args.append(bias)
    if res is not None:
        rows_per_gate = M // gate.shape[0]
        assert rows_per_gate % tm == 0
        in_specs.append(pl.BlockSpec((tm, tn), lambda i, j: (i, j)))
        in_specs.append(pl.BlockSpec((1, 1, tn), lambda i, j: ((i * tm) // rows_per_gate, 0, j)))
        args += [res, gate]
    return pl.pallas_call(
        functools.partial(_mm_kernel, precise=precise, has_bias=bias is not None,
                          has_res=res is not None),
        grid=(M // tm, pl.cdiv(N, tn)),
        in_specs=in_specs,
        out_specs=pl.BlockSpec((tm, tn), lambda i, j: (i, j)),
        out_shape=jax.ShapeDtypeStruct((M, N), out_dtype),
        compiler_params=_params("parallel", "parallel"),
    )(*args)


def _normmod_kernel(x_ref, g_ref, sc_ref, sh_ref, o_ref):
    x = x_ref[0]
    y = x * lax.rsqrt(jnp.mean(x * x, axis=-1, keepdims=True) + NORM_EPS) * g_ref[...]
    o_ref[0] = (y * (1.0 + sc_ref[0]) + sh_ref[0]).astype(o_ref.dtype)


def _norm_modulate(x, g, sc, sh, out_dtype):
    B, T, D = x.shape
    tr = min(T, 512)
    return pl.pallas_call(
        _normmod_kernel,
        grid=(B, T // tr),
        in_specs=[
            pl.BlockSpec((1, tr, D), lambda b, i: (b, i, 0)),
            pl.BlockSpec((1, D), lambda b, i: (0, 0)),
            pl.BlockSpec((1, 1, D), lambda b, i: (b, 0, 0)),
            pl.BlockSpec((1, 1, D), lambda b, i: (b, 0, 0)),
        ],
        out_specs=pl.BlockSpec((1, tr, D), lambda b, i: (b, i, 0)),
        out_shape=jax.ShapeDtypeStruct((B, T, D), out_dtype),
        compiler_params=_params("parallel", "parallel"),
    )(x, g.reshape(1, D), sc.reshape(B, 1, D), sh.reshape(B, 1, D))


def _rwkv_kernel(r_ref, k_ref, v_ref, lw_ref, a_ref, g_ref, kk_ref, ka_ref, rk_ref, lng_ref,
                 lnb_ref, o_ref, h_ref, *, C, N, HB):
    @pl.when(pl.program_id(2) == 0)
    def _():
        h_ref[...] = jnp.zeros_like(h_ref)

    row = lax.broadcasted_iota(jnp.int32, (C, C), 0)
    col = lax.broadcasted_iota(jnp.int32, (C, C), 1)
    incl = row >= col
    strict = row > col
    eye = (row == col).astype(F32)
    eye_n = (lax.broadcasted_iota(jnp.int32, (N, N), 0) == lax.broadcasted_iota(jnp.int32, (N, N), 1))
    lw_all = lw_ref[0]
    cum_all = _fdot(incl.astype(F32), lw_all)
    n_double = int(math.log2(C)) - 1

    outs = []
    for i in range(HB):
        sl = slice(i * N, (i + 1) * N)
        r, k, v, a = r_ref[0, :, sl], k_ref[0, :, sl], v_ref[0, :, sl], a_ref[0, :, sl]
        lw, cum = lw_all[:, sl], cum_all[:, sl]
        tot = cum[C - 1:C, :]
        kx = k * kk_ref[:, sl]
        kappa = kx / jnp.maximum(jnp.sqrt(jnp.sum(kx * kx, axis=-1, keepdims=True)), 1e-12)
        kp = k * (1.0 + (a - 1.0) * ka_ref[:, sl])
        b = kappa * a
        e_neg = jnp.exp(-cum)
        e_tail = jnp.exp(tot - cum)
        kap_t = kappa * jnp.exp(cum - lw)
        r_t = r * jnp.exp(cum)
        big = _bdot_nt(jnp.concatenate([kap_t, r_t], axis=0),
                       jnp.concatenate([b * e_neg, kp * e_neg], axis=0))
        t_b = jnp.where(strict, big[:C, :C], 0.0)
        t_k = jnp.where(strict, big[:C, C:], 0.0)
        m_b = jnp.where(incl, big[C:, :C], 0.0)
        m_k = jnp.where(incl, big[C:, C:], 0.0)
        p = -t_b
        inv = eye + p
        for _ in range(n_double):
            p = _bdot(p, p)
            inv = inv + _bdot(inv, p)
        h0 = h_ref[i]
        aw = _bdot(inv, jnp.concatenate([kap_t, _bdot(t_k, v)], axis=1))
        u = _bdot(aw[:, :N], h0) + aw[:, N:]
        y = _bdot(jnp.concatenate([r_t, m_k, -m_b], axis=1), jnp.concatenate([h0, v, u], axis=0))
        decay_tot = jnp.exp(jnp.sum(jnp.where(eye_n, tot, 0.0), axis=1, keepdims=True))
        h_ref[i] = decay_tot * h0 + _bdot_tn(
            jnp.concatenate([kp * e_tail, -(b * e_tail)], axis=0), jnp.concatenate([v, u], axis=0))
        mu = jnp.mean(y, axis=-1, keepdims=True)
        var = jnp.mean(jnp.square(y - mu), axis=-1, keepdims=True)
        yn = (y - mu) * lax.rsqrt(var + RW_GN_EPS) * lng_ref[:, sl] + lnb_ref[:, sl]
        bonus = jnp.sum(r * kp * rk_ref[:, sl], axis=-1, keepdims=True) * v
        outs.append(yn + bonus)
    o_ref[0] = (jnp.concatenate(outs, axis=-1) * g_ref[0]).astype(o_ref.dtype)


def _rwkv_core(r, k, v, lw, a, g, k_k, k_a, r_k, ln_g, ln_b):
    B, T, D = r.shape
    C, N, HB = RW_CHUNK, RW_HEAD_DIM, RW_HEADS_PER_STEP
    W = HB * N
    seq = pl.BlockSpec((1, C, W), lambda b, h, c: (b, c, h))
    par = pl.BlockSpec((1, W), lambda b, h, c: (0, h))
    return pl.pallas_call(
        functools.partial(_rwkv_kernel, C=C, N=N, HB=HB),
        grid=(B, D // W, T // C),
        in_specs=[seq] * 6 + [par] * 5,
        out_specs=seq,
        out_shape=jax.ShapeDtypeStruct((B, T, D), BF16),
        scratch_shapes=[pltpu.VMEM((HB, N, N), F32)],
        compiler_params=_params("parallel", "parallel", "arbitrary"),
    )(r, k, v, lw, a, g, k_k.reshape(1, D), k_a.reshape(1, D), r_k.reshape(1, D),
      ln_g.reshape(1, D), ln_b.reshape(1, D))


def _shift_mix_bf16(h, mu):
    hs = jnp.pad(h[:, :-1], ((0, 0), (1, 0), (0, 0)))
    return (h + (hs - h) * mu).astype(BF16)


def _rwkv7_mix(h, x_res, gate, mu, w_rkv, w0, w1, w2, a0, a1, a2, g1, g2, k_k, k_a, r_k, ln_g,
               ln_b, w_o):
    B, T, D = h.shape
    M = B * T
    xr, xk, xv, xw, xa, xg = [_shift_mix_bf16(h, mu[j]).reshape(M, D) for j in range(6)]
    r = _matmul(xr, w_rkv, (0, 0))
    k = _matmul(xk, w_rkv, (0, 1))
    v = _matmul(xv, w_rkv, (0, 2))
    w_pre = _matmul(jnp.tanh(_matmul(xw, w1, (0,))), w2, (0,), bias=w0.reshape(1, D))
    log_decay = -jnp.exp(-jax.nn.softplus(-w_pre) - 0.5)
    a = jax.nn.sigmoid(_matmul(_matmul(xa, a1, (0,)), a2, (0,), bias=a0.reshape(1, D)))
    g = _matmul(jax.nn.sigmoid(_matmul(xg, g1, (0,))), g2, (0,))
    sh = (B, T, D)
    o = _rwkv_core(r.reshape(sh), k.reshape(sh), v.reshape(sh), log_decay.reshape(sh),
                   a.reshape(sh), g.reshape(sh), k_k, k_a, r_k, ln_g, ln_b)
    return _matmul(o.reshape(M, D), w_o, (0,), res=x_res.reshape(M, D), gate=gate).reshape(sh)


def _ret_kernel(q_ref, k_ref, v_ref, g_ref, cos_ref, sin_ref, dm_ref, xz_ref, gng_ref, gnb_ref,
                o_ref, s_ref, *, dk):
    @pl.when(pl.program_id(2) == 0)
    def _():
        s_ref[...] = jnp.zeros_like(s_ref)

    cos, sin = cos_ref[...], sin_ref[...]
    half = dk // 2

    def rot(z):
        z1, z2 = z[:, :half], z[:, half:]
        return jnp.concatenate([z1 * cos - z2 * sin, z1 * sin + z2 * cos], axis=-1)

    q = rot(q_ref[0])
    k = rot(k_ref[0] * (dk ** -0.5))
    v = v_ref[0]
    xi, zeta, g_chunk = xz_ref[0, :, 0:1], xz_ref[0, :, 1:2], xz_ref[0, 0:1, 2:3]
    scores = _bdot_nt(q, k) * dm_ref[0]
    state = s_ref[...]
    o = _bdot(scores, v) + _bdot(q * xi, state)
    s_ref[...] = g_chunk * state + _bdot_tn(k * zeta, v)
    mu = jnp.mean(o, axis=-1, keepdims=True)
    var = jnp.mean(jnp.square(o - mu), axis=-1, keepdims=True)
    y = (o - mu) * lax.rsqrt(var + NORM_EPS) * gng_ref[0] + gnb_ref[0]
    gate = g_ref[0]
    o_ref[0] = (y * (gate * jax.nn.sigmoid(gate))).astype(o_ref.dtype)


def _retention_mix(h, x_res, gate, w_in, gn_g, gn_b, w_o):
    B, T, D = h.shape
    M = B * T
    H, dk, dv, C = RET_HEADS, RET_DK, RET_DV, RET_CHUNK
    proj = _matmul(h.reshape(M, D), w_in, (0,)).reshape(B, T, H * (2 * dk + 2 * dv))
    qb, kb, vb, gb = 0, (H * dk) // dk, (2 * H * dk) // dv, (2 * H * dk + H * dv) // dv
    theta = 1.0 / (RET_ROT_BASE ** jnp.linspace(0.0, 1.0, dk // 2, dtype=F32))
    ang = jnp.arange(T, dtype=F32)[:, None] * theta[None, :]
    log_gamma = jnp.log(1.0 - 2.0 ** (-5.0 - jnp.arange(H, dtype=F32)))
    pos = jnp.arange(C, dtype=F32)
    rel = pos[:, None] - pos[None, :]
    dmask = jnp.where(rel >= 0, jnp.exp(jnp.maximum(rel, 0.0) * log_gamma[:, None, None]), 0.0)
    xi = jnp.exp((pos + 1.0)[None, :] * log_gamma[:, None])
    zeta = jnp.exp((C - 1.0 - pos)[None, :] * log_gamma[:, None])
    g_chunk = jnp.broadcast_to(jnp.exp(C * log_gamma)[:, None], (H, C))
    xz = jnp.concatenate([jnp.stack([xi, zeta, g_chunk], axis=-1), jnp.zeros((H, C, 125), F32)], axis=-1)
    o = pl.pallas_call(
        functools.partial(_ret_kernel, dk=dk),
        grid=(B, H, T // C),
        in_specs=[
            pl.BlockSpec((1, C, dk), lambda b, h, c: (b, c, qb + h)),
            pl.BlockSpec((1, C, dk), lambda b, h, c: (b, c, kb + h)),
            pl.BlockSpec((1, C, dv), lambda b, h, c: (b, c, vb + h)),
            pl.BlockSpec((1, C, dv), lambda b, h, c: (b, c, gb + h)),
            pl.BlockSpec((C, dk // 2), lambda b, h, c: (c, 0)),
            pl.BlockSpec((C, dk // 2), lambda b, h, c: (c, 0)),
            pl.BlockSpec((1, C, C), lambda b, h, c: (h, 0, 0)),
            pl.BlockSpec((1, C, 128), lambda b, h, c: (h, 0, 0)),
            pl.BlockSpec((1, 1, dv), lambda b, h, c: (h, 0, 0)),
            pl.BlockSpec((1, 1, dv), lambda b, h, c: (h, 0, 0)),
        ],
        out_specs=pl.BlockSpec((1, C, dv), lambda b, h, c: (b, c, h)),
        out_shape=jax.ShapeDtypeStruct((B, T, H * dv), BF16),
        scratch_shapes=[pltpu.VMEM((dk, dv), F32)],
        compiler_params=_params("parallel", "parallel", "arbitrary"),
    )(proj, proj, proj, proj, jnp.cos(ang), jnp.sin(ang), dmask, xz,
      gn_g.reshape(H, 1, dv), gn_b.reshape(H, 1, dv))
    return _matmul(o.reshape(M, H * dv), w_o, (0,), res=x_res.reshape(M, D), gate=gate).reshape(B, T, D)


def _gla_kernel(q_ref, k_ref, v_ref, g_ref, la_ref, gn_ref, o_ref, s_ref, *, C, dk):
    @pl.when(pl.program_id(2) == 0)
    def _():
        s_ref[...] = jnp.zeros_like(s_ref)

    row = lax.broadcasted_iota(jnp.int32, (C, C), 0)
    col = lax.broadcasted_iota(jnp.int32, (C, C), 1)
    causal = row >= col
    la = la_ref[0]
    b = _fdot(causal.astype(F32), la)
    b_last = b[C - 1:C, :]
    eye_k = (lax.broadcasted_iota(jnp.int32, (dk, dk), 0) == lax.broadcasted_iota(jnp.int32, (dk, dk), 1))
    d_last_col = jnp.exp(jnp.sum(jnp.where(eye_k, b_last, 0.0), axis=1, keepdims=True))
    k, v = k_ref[0], v_ref[0]
    q_in = q_ref[0] * (dk ** -0.5) * jnp.exp(b)
    k_in = k * jnp.exp(-b)
    att = jnp.where(causal, _bdot_nt(q_in, k_in), 0.0)
    state = s_ref[...]
    o = _bdot(att, v) + _bdot(q_in, state)
    s_ref[...] = d_last_col * state + _bdot_tn(k * jnp.exp(b_last - b), v)
    y = o * lax.rsqrt(jnp.mean(o * o, axis=-1, keepdims=True) + NORM_EPS) * gn_ref[...]
    gate = g_ref[0]
    o_ref[0] = (y * (gate * jax.nn.sigmoid(gate))).astype(o_ref.dtype)


def _gla_mix(h, x_res, gate, w_in, w_a1, w_a2, b_a, gn_g, w_o):
    B, T, D = h.shape
    M = B * T
    H, dk, dv, C = GLA_HEADS, GLA_DK, GLA_DV, GLA_CHUNK
    h2 = h.reshape(M, D)
    proj = _matmul(h2, w_in, (0,)).reshape(B, T, H * (2 * dk + 2 * dv))
    z = _matmul(_matmul(h2, w_a1, (0,)), w_a2, (0,), bias=b_a.reshape(1, H * dk))
    log_a = (jax.nn.log_sigmoid(z) / GLA_GATE_NORM).reshape(B, T, H * dk)
    qb, kb, vb, gb = 0, H, (2 * H * dk) // dv, (2 * H * dk + H * dv) // dv
    o = pl.pallas_call(
        functools.partial(_gla_kernel, C=C, dk=dk),
        grid=(B, H, T // C),
        in_specs=[
            pl.BlockSpec((1, C, dk), lambda b, h, c: (b, c, qb + h)),
            pl.BlockSpec((1, C, dk), lambda b, h, c: (b, c, kb + h)),
            pl.BlockSpec((1, C, dv), lambda b, h, c: (b, c, vb + h)),
            pl.BlockSpec((1, C, dv), lambda b, h, c: (b, c, gb + h)),
            pl.BlockSpec((1, C, dk), lambda b, h, c: (b, c, h)),
            pl.BlockSpec((1, dv), lambda b, h, c: (0, 0)),
        ],
        out_specs=pl.BlockSpec((1, C, dv), lambda b, h, c: (b, c, h)),
        out_shape=jax.ShapeDtypeStruct((B, T, H * dv), BF16),
        scratch_shapes=[pltpu.VMEM((dk, dv), F32)],
        compiler_params=_params("parallel", "parallel", "arbitrary"),
    )(proj, proj, proj, proj, log_a, gn_g.reshape(1, dv))
    return _matmul(o.reshape(M, H * dv), w_o, (0,), res=x_res.reshape(M, D), gate=gate).reshape(B, T, D)


def _nsa_cmp_kernel(q_ref, kc_ref, vc_ref, o_ref, sel_ref, *, tq, n_cmp, n_pad, n_slc, hpg):
    L, S, Ls, hd = NSA_CMP_BLK, NSA_CMP_STRIDE, NSA_SEL_BLK, NSA_HD
    t_pos = pl.program_id(2) * tq + lax.broadcasted_iota(jnp.int32, (tq, 1), 0)
    n_ix = lax.broadcasted_iota(jnp.int32, (1, n_pad), 1)
    valid = (n_ix * S + (L - 1) <= t_pos) & (n_ix < n_cmp)
    validf = valid.astype(F32)
    c_start = lax.broadcasted_iota(jnp.int32, (n_pad, n_slc), 0) * S
    s_start = lax.broadcasted_iota(jnp.int32, (n_pad, n_slc), 1) * Ls
    overlap = ((c_start <= s_start + (Ls - 1)) & (c_start + (L - 1) >= s_start)
               & (c_start < n_cmp * S)).astype(F32)
    kc, vc = kc_ref[0, 0], vc_ref[0, 0]
    imp = jnp.zeros((tq, n_slc), F32)
    outs = []
    for hh in range(hpg):
        s = _fdot_nt(q_ref[0, :, hh * hd:(hh + 1) * hd], kc)
        s = jnp.where(valid, s, NEG_INF)
        e = jnp.exp(s - jnp.max(s, axis=-1, keepdims=True))
        p = e / jnp.sum(e, axis=-1, keepdims=True) * validf
        outs.append(_bdot(p, vc))
        imp = imp + _fdot(p, overlap)
    o_ref[0] = jnp.concatenate(outs, axis=-1)

    j = lax.broadcasted_iota(jnp.int32, (1, n_slc), 1)
    cur = t_pos // Ls
    forced = (j == 0) | (j == cur) | (j == cur - 1)
    imp = jnp.where(j > cur, -IMP_BIG, jnp.where(forced, IMP_BIG, imp))
    rank = jnp.zeros((tq, n_slc), jnp.int32)
    for jp in range(n_slc):
        c = imp[:, jp:jp + 1]
        rank = rank + ((c > imp) | ((c == imp) & (jp < j))).astype(jnp.int32)
    sel_ref[0, 0] = (rank < min(NSA_SEL_TOPK, n_slc)).astype(F32)


def _nsa_att_kernel(q_ref, ks_ref, vs_ref, kw_ref, vw_ref, sel_ref, oc_ref, gt_ref, o_ref,
                    m_ref, l_ref, acc_ref, *, tq, tk, T, n_slc, hpg):
    Ls, W, hd = NSA_SEL_BLK, NSA_WINDOW, NSA_HD
    qi = pl.program_id(2)
    t0 = qi * tq
    q = jnp.concatenate([q_ref[0, :, hh * hd:(hh + 1) * hd] for hh in range(hpg)], axis=0)
    t_q = t0 + lax.broadcasted_iota(jnp.int32, (tq, 1), 0)
    sel = sel_ref[0, 0]

    m_ref[...] = jnp.full_like(m_ref, NEG_INF)
    l_ref[...] = jnp.zeros_like(l_ref)
    acc_ref[...] = jnp.zeros_like(acc_ref)

    def body(kb, carry):
        k0 = pl.multiple_of(kb * tk, tk)
        kt = ks_ref[0, pl.ds(k0, tk), :]
        vt = vs_ref[0, pl.ds(k0, tk), :]
        blk_of_key = (k0 + lax.broadcasted_iota(jnp.int32, (n_slc, tk), 1)) // Ls
        expand = (blk_of_key == lax.broadcasted_iota(jnp.int32, (n_slc, tk), 0)).astype(F32)
        kpos = k0 + lax.broadcasted_iota(jnp.int32, (1, tk), 1)
        mask = (_bdot(sel, expand) > 0.5) & (kpos <= t_q)
        mask = jnp.concatenate([jnp.where(mask, 1.0, 0.0)] * hpg, axis=0) > 0.5
        s = jnp.where(mask, _bdot_nt(q, kt), NEG_INF)
        m_old = m_ref[...]
        m_new = jnp.maximum(m_old, jnp.max(s, axis=-1, keepdims=True))
        alpha = jnp.exp(m_old - m_new)
        p = jnp.where(mask, jnp.exp(s - m_new), 0.0)
        l_ref[...] = alpha * l_ref[...] + jnp.sum(p, axis=-1, keepdims=True)
        acc_ref[...] = alpha * acc_ref[...] + _bdot(p, vt)
        m_ref[...] = m_new
        return carry

    lax.fori_loop(0, (t0 + tq + tk - 1) // tk, body, 0)
    o_sel = acc_ref[...] / l_ref[...]

    span = W + tq
    w0 = pl.multiple_of(jnp.maximum(t0 - W, 0), tq)
    kw = kw_ref[0, pl.ds(w0, span), :]
    vw = vw_ref[0, pl.ds(w0, span), :]
    wpos = w0 + lax.broadcasted_iota(jnp.int32, (1, span), 1)
    m_win = (wpos <= t_q) & (wpos > t_q - W)
    m_win = jnp.concatenate([jnp.where(m_win, 1.0, 0.0)] * hpg, axis=0) > 0.5
    s = jnp.where(m_win, _bdot_nt(q, kw), NEG_INF)
    e = jnp.where(m_win, jnp.exp(s - jnp.max(s, axis=-1, keepdims=True)), 0.0)
    o_win = _bdot(e, vw) / jnp.sum(e, axis=-1, keepdims=True)

    gt = jax.nn.sigmoid(gt_ref[0, 0])
    outs = []
    for hh in range(hpg):
        rows = slice(hh * tq, (hh + 1) * tq)
        outs.append(gt[:, hh:hh + 1] * oc_ref[0, :, hh * hd:(hh + 1) * hd]
                    + gt[:, hpg + hh:hpg + hh + 1] * o_sel[rows]
                    + gt[:, 2 * hpg + hh:2 * hpg + hh + 1] * o_win[rows])
    o_ref[0] = jnp.concatenate(outs, axis=-1).astype(o_ref.dtype)


def _rms_norm(x, g):
    return x * lax.rsqrt(jnp.mean(x * x, axis=-1, keepdims=True) + NORM_EPS) * g


def _nsa_mix(h, x_res, gate, w_in, q_g, k_g, cmp_pos, cmp_w1, cmp_w2, w_o):
    B, T, D = h.shape
    M = B * T
    H, G, hd = NSA_HEADS, NSA_KV_HEADS, NSA_HD
    hpg = H // G
    L, S, Ls = NSA_CMP_BLK, NSA_CMP_STRIDE, NSA_SEL_BLK
    kvw = G * hd
    n_main = H * hd + 6 * kvw
    h2 = h.reshape(M, D)
    proj = _matmul(h2, w_in, (0,), n_cols=n_main)
    gate_w = jnp.pad(w_in[0, :, n_main:], ((0, 0), (0, 128 - 3 * H)))
    gates = _matmul(h2, gate_w)[:, :3 * H]
    q = _rms_norm(proj[:, :H * hd].reshape(B, T, H, hd), q_g) * (hd ** -0.5)
    q = q.reshape(B, T, H * hd)

    def kv(i):
        return proj[:, H * hd + i * kvw:H * hd + (i + 1) * kvw].reshape(B, T, G, hd)

    n_cmp = (T - L) // S + 1
    n_grp = T // S
    assert L == 2 * S and n_cmp == n_grp - 1

    def compress(z, pos, w1, w2):
        zg = z.reshape(B, n_grp, S, G, hd).transpose(0, 3, 1, 2, 4).reshape(B * G * n_grp, S * hd)
        w_halves = jnp.concatenate([w1[:S * hd], w1[S * hd:]], axis=1)
        part = _matmul(zg, w_halves, tm=min(512, B * G * n_grp)).reshape(B, G, n_grp, 2 * hd)
        pos_term = _matmul(jnp.broadcast_to(pos.reshape(1, L * hd), (8, L * hd)), w1, tm=8)[0]
        pre = part[:, :, :-1, :hd] + part[:, :, 1:, hd:] + pos_term
        pre = jnp.pad(pre, ((0, 0), (0, 0), (0, 1), (0, 0))).reshape(B * G * n_grp, hd)
        return _matmul(jax.nn.silu(pre), w2, tm=min(512, B * G * n_grp)).reshape(B, G, n_grp, hd)

    kc = _rms_norm(compress(kv(0), cmp_pos[0], cmp_w1[0], cmp_w2[0]), k_g[0])
    vc = compress(kv(1), cmp_pos[1], cmp_w1[1], cmp_w2[1])
    k_sel = _rms_norm(kv(2), k_g[1]).reshape(B, T, kvw)
    v_sel = kv(3).reshape(B, T, kvw)
    k_win = _rms_norm(kv(4), k_g[2]).reshape(B, T, kvw)
    v_win = kv(5).reshape(B, T, kvw)

    n_slc = T // Ls
    tq = min(NSA_CMP_TQ, T)
    o_cmp, sel = pl.pallas_call(
        functools.partial(_nsa_cmp_kernel, tq=tq, n_cmp=n_cmp, n_pad=n_grp, n_slc=n_slc, hpg=hpg),
        grid=(B, G, T // tq),
        in_specs=[
            pl.BlockSpec((1, tq, hpg * hd), lambda b, g, i: (b, i, g)),
            pl.BlockSpec((1, 1, n_grp, hd), lambda b, g, i: (b, g, 0, 0)),
            pl.BlockSpec((1, 1, n_grp, hd), lambda b, g, i: (b, g, 0, 0)),
        ],
        out_specs=[
            pl.BlockSpec((1, tq, hpg * hd), lambda b, g, i: (b, i, g)),
            pl.BlockSpec((1, 1, tq, n_slc), lambda b, g, i: (b, g, i, 0)),
        ],
        out_shape=[jax.ShapeDtypeStruct((B, T, H * hd), F32),
                   jax.ShapeDtypeStruct((B, G, T, n_slc), F32)],
        compiler_params=_params("parallel", "parallel", "parallel"),
    )(q, kc, vc)

    gt = gates.reshape(B, T, 3, G, hpg).transpose(0, 3, 1, 2, 4).reshape(B, G, T, 3 * hpg)
    tq = min(NSA_ATT_TQ, T)
    tk = min(NSA_ATT_TK, T)
    assert T >= NSA_WINDOW + tq
    kv_spec = pl.BlockSpec((1, T, hd), lambda b, g, i: (b, 0, g))
    o = pl.pallas_call(
        functools.partial(_nsa_att_kernel, tq=tq, tk=tk, T=T, n_slc=n_slc, hpg=hpg),
        grid=(B, G, T // tq),
        in_specs=[
            pl.BlockSpec((1, tq, hpg * hd), lambda b, g, i: (b, i, g)),
            kv_spec, kv_spec, kv_spec, kv_spec,
            pl.BlockSpec((1, 1, tq, n_slc), lambda b, g, i: (b, g, i, 0)),
            pl.BlockSpec((1, tq, hpg * hd), lambda b, g, i: (b, i, g)),
            pl.BlockSpec((1, 1, tq, 3 * hpg), lambda b, g, i: (b, g, i, 0)),
        ],
        out_specs=pl.BlockSpec((1, tq, hpg * hd), lambda b, g, i: (b, i, g)),
        out_shape=jax.ShapeDtypeStruct((B, T, H * hd), BF16),
        scratch_shapes=[pltpu.VMEM((hpg * tq, 1), F32), pltpu.VMEM((hpg * tq, 1), F32),
                        pltpu.VMEM((hpg * tq, hd), F32)],
        compiler_params=_params("parallel", "parallel", "arbitrary"),
    )(q, k_sel, v_sel, k_win, v_win, sel, o_cmp, gt)
    return _matmul(o.reshape(M, H * hd), w_o, (0,), res=x_res.reshape(M, D), gate=gate).reshape(B, T, D)


def _moe_up_kernel(blk_e_ref, x_ref, w_ref, b_ref, o_ref):
    gu = _bdot(x_ref[...], w_ref[...]) + b_ref[0]
    x_glu = jnp.minimum(gu[:, :D_EXPERT], SWIGLU_LIMIT)
    x_lin = jnp.clip(gu[:, D_EXPERT:], -SWIGLU_LIMIT, SWIGLU_LIMIT)
    o_ref[...] = (x_glu * jax.nn.sigmoid(SWIGLU_ALPHA * x_glu) * (x_lin + 1.0)).astype(o_ref.dtype)


def _moe_down_kernel(blk_e_ref, a_ref, w_ref, b_ref, rw_ref, o_ref):
    o_ref[...] = (_bdot(a_ref[...], w_ref[...]) + b_ref[0]) * rw_ref[...]


def _moe_ffn(h, layer, x_res, gate, w_router, b_router, w_gu, b_gu, w_down, b_down):
    B, T, D = h.shape
    N = B * T
    E, K, R = N_EXPERTS, TOP_K, MOE_ROWS
    x = h.reshape(N, D)
    w_r = jnp.pad(w_router[layer], ((0, 0), (0, 128 - E)))
    logits = _matmul(x, w_r, precise=True)[:, :E] + b_router[layer]
    top_v, top_e = lax.top_k(logits, K)
    top_w = jax.nn.softmax(top_v, axis=-1)
    e_flat = top_e.reshape(-1)
    order = jnp.argsort(e_flat)
    e_s = e_flat[order]
    t_s = (order // K).astype(jnp.int32)
    w_s = top_w.reshape(-1)[order]
    counts = jnp.zeros((E,), jnp.int32).at[e_flat].add(1)
    padded = (counts + R - 1) // R * R
    g_start = jnp.cumsum(counts) - counts
    p_end = jnp.cumsum(padded)
    p_start = p_end - padded
    dest = p_start[e_s] + (jnp.arange(N * K, dtype=jnp.int32) - g_start[e_s])
    n_blocks = (N * K + R - 1) // R + E
    cap = n_blocks * R
    buf_tok = jnp.full((cap,), N, jnp.int32).at[dest].set(t_s)
    buf_w = jnp.zeros((cap,), F32).at[dest].set(w_s)
    blk_e = jnp.minimum(jnp.searchsorted(p_end, jnp.arange(n_blocks, dtype=jnp.int32) * R, side='right'),
                        E - 1).astype(jnp.int32)
    x_pad = jnp.concatenate([x.astype(BF16), jnp.zeros((1, D), BF16)], axis=0)
    xb = x_pad[buf_tok]

    act = pl.pallas_call(
        _moe_up_kernel,
        grid_spec=pltpu.PrefetchScalarGridSpec(
            num_scalar_prefetch=1,
            grid=(n_blocks,),
            in_specs=[
                pl.BlockSpec((R, D), lambda i, be: (i, 0)),
                pl.BlockSpec((None, None, D, 2 * D_EXPERT), lambda i, be: (layer, be[i], 0, 0)),
                pl.BlockSpec((None, 1, 1, 2 * D_EXPERT), lambda i, be: (layer, be[i], 0, 0)),
            ],
            out_specs=pl.BlockSpec((R, D_EXPERT), lambda i, be: (i, 0)),
        ),
        out_shape=jax.ShapeDtypeStruct((cap, D_EXPERT), BF16),
        compiler_params=_params("arbitrary"),
    )(blk_e, xb, w_gu, b_gu.reshape(b_gu.shape[0], E, 1, 2 * D_EXPERT))

    out = pl.pallas_call(
        _moe_down_kernel,
        grid_spec=pltpu.PrefetchScalarGridSpec(
            num_scalar_prefetch=1,
            grid=(n_blocks,),
            in_specs=[
                pl.BlockSpec((R, D_EXPERT), lambda i, be: (i, 0)),
                pl.BlockSpec((None, None, D_EXPERT, D), lambda i, be: (layer, be[i], 0, 0)),
                pl.BlockSpec((None, 1, 1, D), lambda i, be: (layer, be[i], 0, 0)),
                pl.BlockSpec((R, 1), lambda i, be: (i, 0)),
            ],
            out_specs=pl.BlockSpec((R, D), lambda i, be: (i, 0)),
        ),
        out_shape=jax.ShapeDtypeStruct((cap, D), F32),
        compiler_params=_params("arbitrary"),
    )(blk_e, act, w_down, b_down.reshape(b_down.shape[0], E, 1, D), buf_w.reshape(cap, 1))

    slot = jnp.zeros((N * K,), jnp.int32).at[order].set(dest).reshape(N, K)
    y = jnp.sum(out[slot], axis=1)
    return x_res + gate * y.reshape(B, T, D)


def kernel(x, c, ada_w, ada_b, norm_g, rw_mu, rw_w_rkv, rw_w0, rw_w1, rw_w2, rw_a0, rw_a1, rw_a2, rw_g1, rw_g2, rw_k_k, rw_k_a, rw_r_k, rw_ln_g, rw_ln_b, rw_w_o, ret_w_in, ret_gn_g, ret_gn_b, ret_w_o, gla_w_in, gla_w_a1, gla_w_a2, gla_b_a, gla_gn_g, gla_w_o, nsa_w_in, nsa_q_g, nsa_k_g, nsa_cmp_pos, nsa_cmp_w1, nsa_cmp_w2, nsa_w_o, moe_router_w, moe_router_b, moe_w_gu, moe_b_gu, moe_w_down, moe_b_down):
    B, T, D = x.shape
    depth = ada_w.shape[0]
    c_act = jnp.pad(jax.nn.silu(c), ((0, 8 - B), (0, 0)))
    for i in range(depth):
        mod = _matmul(c_act, ada_w, (i,), tm=8, tn=1024)[:B] + ada_b[i]
        sh1, sc1, gt1, sh2, sc2, gt2 = jnp.split(mod, 6, axis=-1)
        gt1, gt2 = gt1.reshape(B, 1, D), gt2.reshape(B, 1, D)
        m, j = i % 4, i // 4
        if m == 0:
            h = _norm_modulate(x, norm_g[i, 0], sc1, sh1, F32)
            x = _rwkv7_mix(h, x, gt1, rw_mu[j], rw_w_rkv[j:j + 1], rw_w0[j], rw_w1[j:j + 1], rw_w2[j:j + 1],
                           rw_a0[j], rw_a1[j:j + 1], rw_a2[j:j + 1], rw_g1[j:j + 1], rw_g2[j:j + 1],
                           rw_k_k[j], rw_k_a[j], rw_r_k[j], rw_ln_g[j], rw_ln_b[j], rw_w_o[j:j + 1])
        elif m == 1:
            h = _norm_modulate(x, norm_g[i, 0], sc1, sh1, BF16)
            x = _retention_mix(h, x, gt1, ret_w_in[j:j + 1], ret_gn_g[j], ret_gn_b[j], ret_w_o[j:j + 1])
        elif m == 2:
            h = _norm_modulate(x, norm_g[i, 0], sc1, sh1, BF16)
            x = _gla_mix(h, x, gt1, gla_w_in[j:j + 1], gla_w_a1[j:j + 1], gla_w_a2[j:j + 1], gla_b_a[j],
                         gla_gn_g[j], gla_w_o[j:j + 1])
        else:
            h = _norm_modulate(x, norm_g[i, 0], sc1, sh1, BF16)
            x = _nsa_mix(h, x, gt1, nsa_w_in[j:j + 1], nsa_q_g[j], nsa_k_g[j], nsa_cmp_pos[j],
                         nsa_cmp_w1[j], nsa_cmp_w2[j], nsa_w_o[j:j + 1])
        h = _norm_modulate(x, norm_g[i, 1], sc2, sh2, F32)
        x = _moe_ffn(h, i, x, gt2, moe_router_w, moe_router_b, moe_w_gu, moe_b_gu, moe_w_down, moe_b_down)
    return x
```

```python
import functools
import math

import jax
import jax.numpy as jnp
from jax import lax
from jax.experimental import pallas as pl
from jax.experimental.pallas import tpu as pltpu

F32 = jnp.float32
BF16 = jnp.bfloat16
HIGHEST = lax.Precision.HIGHEST

NORM_EPS = 1e-6
NEG_INF = -1e30

RW_HEAD_DIM = 64
RW_GN_EPS = 64e-5
RW_CHUNK = 64
RW_HEADS_PER_STEP = 8

RET_HEADS = 8
RET_DK = 256
RET_DV = 512
RET_CHUNK = 128
RET_ROT_BASE = 10000.0

GLA_HEADS = 4
GLA_DK = 256
GLA_DV = 512
GLA_GATE_NORM = 16.0
GLA_CHUNK = 64

NSA_HEADS = 16
NSA_KV_HEADS = 4
NSA_HD = 128
NSA_CMP_BLK = 32
NSA_CMP_STRIDE = 16
NSA_SEL_BLK = 64
NSA_SEL_TOPK = 16
NSA_WINDOW = 512
NSA_CMP_TQ = 256
NSA_ATT_TQ = 128
NSA_ATT_TK = 512
IMP_BIG = 3e38

N_EXPERTS = 32
TOP_K = 4
D_EXPERT = 768
SWIGLU_ALPHA = 1.702
SWIGLU_LIMIT = 7.0
MOE_ROWS = 256

VMEM_LIMIT_BYTES = 52 * 1024 * 1024


def _params(*sem):
    return pltpu.CompilerParams(dimension_semantics=sem, vmem_limit_bytes=VMEM_LIMIT_BYTES)


def _bdot(a, b):
    return jnp.dot(a.astype(BF16), b.astype(BF16), preferred_element_type=F32)


def _bdot_nt(a, b):
    return lax.dot_general(a.astype(BF16), b.astype(BF16), (((1,), (1,)), ((), ())),
                           preferred_element_type=F32)


def _bdot_tn(a, b):
    return lax.dot_general(a.astype(BF16), b.astype(BF16), (((0,), (0,)), ((), ())),
                           preferred_element_type=F32)


def _fdot(a, b):
    return jnp.dot(a, b, precision=HIGHEST, preferred_element_type=F32)


def _fdot_nt(a, b):
    return lax.dot_general(a, b, (((1,), (1,)), ((), ())), precision=HIGHEST,
                           preferred_element_type=F32)


def _split3(x):
    hi = x.astype(BF16)
    rem = x - hi.astype(F32)
    mid = rem.astype(BF16)
    return hi, mid, (rem - mid.astype(F32)).astype(BF16)


def _exact_lhs_dot(m, x):
    mb = m.astype(BF16)
    return sum(jnp.dot(mb, part, preferred_element_type=F32) for part in _split3(x))


def _exact_rhs_dot(x, m):
    mb = m.astype(BF16)
    return sum(jnp.dot(part, mb, preferred_element_type=F32) for part in _split3(x))


def _mm_kernel(*refs, has_bias, has_res):
    x_ref, w_ref = refs[0], refs[1]
    pos = 2
    acc = _bdot(x_ref[...], w_ref[...])
    if has_bias:
        acc = acc + refs[pos][...]
        pos += 1
    if has_res:
        acc = refs[pos][...] + refs[pos + 1][0] * acc
        pos += 2
    o_ref = refs[pos]
    o_ref[...] = acc.astype(o_ref.dtype)


def _matmul(x, w, lead=(), *, bias=None, res=None, gate=None, n_cols=None, tm=None, tn=None,
            out_dtype=F32, name="matmul"):
    M, K = x.shape
    N = n_cols if n_cols is not None else w.shape[-1]
    if tm is None:
        tm = min(M, 1024 if x.dtype == BF16 else 512)
    if tn is None:
        tn = min(N, 512) if N % 128 == 0 else N
    assert M % tm == 0
    nlead = len(lead)
    in_specs = [
        pl.BlockSpec((tm, K), lambda i, j: (i, 0)),
        pl.BlockSpec((None,) * nlead + (K, tn), lambda i, j: tuple(lead) + (0, j)),
    ]
    args = [x, w]
    if bias is not None:
        in_specs.append(pl.BlockSpec((1, tn), lambda i, j: (0, j)))
        args.append(bias)
    if res is not None:
        rows_per_gate = M // gate.shape[0]
        assert rows_per_gate % tm == 0
        in_specs.append(pl.BlockSpec((tm, tn), lambda i, j: (i, j)))
        in_specs.append(pl.BlockSpec((1, 1, tn), lambda i, j: ((i * tm) // rows_per_gate, 0, j)))
        args += [res, gate]
    return pl.pallas_call(
        functools.partial(_mm_kernel, has_bias=bias is not None, has_res=res is not None),
        grid=(M // tm, pl.cdiv(N, tn)),
        in_specs=in_specs,
        out_specs=pl.BlockSpec((tm, tn), lambda i, j: (i, j)),
        out_shape=jax.ShapeDtypeStruct((M, N), out_dtype),
        compiler_params=_params("parallel", "parallel"),
        name=name,
    )(*args)


def _normmod_kernel(x_ref, g_ref, sc_ref, sh_ref, o_ref):
    x = x_ref[0]
    y = x * lax.rsqrt(jnp.mean(x * x, axis=-1, keepdims=True) + NORM_EPS) * g_ref[...]
    o_ref[0] = (y * (1.0 + sc_ref[0]) + sh_ref[0]).astype(o_ref.dtype)


def _norm_modulate(x, g, sc, sh, out_dtype):
    B, T, D = x.shape
    tr = min(T, 512)
    return pl.pallas_call(
        _normmod_kernel,
        grid=(B, T // tr),
        in_specs=[
            pl.BlockSpec((1, tr, D), lambda b, i: (b, i, 0)),
            pl.BlockSpec((1, D), lambda b, i: (0, 0)),
            pl.BlockSpec((1, 1, D), lambda b, i: (b, 0, 0)),
            pl.BlockSpec((1, 1, D), lambda b, i: (b, 0, 0)),
        ],
        out_specs=pl.BlockSpec((1, tr, D), lambda b, i: (b, i, 0)),
        out_shape=jax.ShapeDtypeStruct((B, T, D), out_dtype),
        compiler_params=_params("parallel", "parallel"),
        name="norm_modulate",
    )(x, g.reshape(1, D), sc.reshape(B, 1, D), sh.reshape(B, 1, D))


def _rwkv_kernel(r_ref, k_ref, v_ref, lw_ref, a_ref, g_ref, kk_ref, ka_ref, rk_ref, lng_ref,
                 lnb_ref, o_ref, h_ref, *, C, N, HB):
    @pl.when(pl.program_id(2) == 0)
    def _():
        h_ref[...] = jnp.zeros_like(h_ref)

    row = lax.broadcasted_iota(jnp.int32, (C, C), 0)
    col = lax.broadcasted_iota(jnp.int32, (C, C), 1)
    incl = row >= col
    strict = row > col
    eye = (row == col).astype(F32)
    eye_n = (lax.broadcasted_iota(jnp.int32, (N, N), 0) == lax.broadcasted_iota(jnp.int32, (N, N), 1))
    n_double = int(math.log2(C)) - 1
    heads = range(HB)

    def head(x, i):
        return x[:, i * N:(i + 1) * N]

    r_all, k_all, v_all, a_all, lw_all = r_ref[0], k_ref[0], v_ref[0], a_ref[0], lw_ref[0]
    cum_all = _exact_lhs_dot(incl.astype(F32), lw_all)
    tot_all = cum_all[C - 1:C, :]
    e_neg = jnp.exp(-cum_all)
    e_tail = jnp.exp(tot_all - cum_all)
    kx_all = k_all * kk_ref[...]
    kp_all = k_all * (1.0 + (a_all - 1.0) * ka_ref[...])
    rt_all = r_all * jnp.exp(cum_all)
    kn_all = kp_all * e_neg
    kt_all = kp_all * e_tail
    ep_all = jnp.exp(cum_all - lw_all)
    rkr_all = r_all * kp_all * rk_ref[...]

    kappa = [head(kx_all, i) / jnp.maximum(
        jnp.sqrt(jnp.sum(jnp.square(head(kx_all, i)), axis=-1, keepdims=True)), 1e-12) for i in heads]
    b = [kappa[i] * head(a_all, i) for i in heads]
    kap_t = [kappa[i] * head(ep_all, i) for i in heads]
    r_t = [head(rt_all, i) for i in heads]
    v = [head(v_all, i) for i in heads]
    big = [_bdot_nt(jnp.concatenate([kap_t[i], r_t[i]], axis=0),
                    jnp.concatenate([b[i] * head(e_neg, i), head(kn_all, i)], axis=0))
           for i in heads]
    t_k = [jnp.where(strict, big[i][:C, C:], 0.0) for i in heads]
    m_b = [jnp.where(incl, big[i][C:, :C], 0.0) for i in heads]
    m_k = [jnp.where(incl, big[i][C:, C:], 0.0) for i in heads]
    p = [jnp.where(strict, -big[i][:C, :C], 0.0) for i in heads]
    inv = [eye + p[i] for i in heads]
    tkv = [_bdot(t_k[i], v[i]) for i in heads]
    for _ in range(n_double):
        p = [_bdot(p[i], p[i]) for i in heads]
        inv = [inv[i] + _bdot(inv[i], p[i]) for i in heads]
    h0 = [h_ref[i] for i in heads]
    aw = [_bdot(inv[i], jnp.concatenate([kap_t[i], tkv[i]], axis=1)) for i in heads]
    u = [_bdot(aw[i][:, :N], h0[i]) + aw[i][:, N:] for i in heads]
    y = [_bdot(jnp.concatenate([r_t[i], m_k[i], -m_b[i]], axis=1),
               jnp.concatenate([h0[i], v[i], u[i]], axis=0)) for i in heads]
    for i in heads:
        decay_tot = jnp.exp(jnp.sum(jnp.where(eye_n, head(tot_all, i), 0.0), axis=1, keepdims=True))
        h_ref[i] = decay_tot * h0[i] + _bdot_tn(
            jnp.concatenate([head(kt_all, i), -(b[i] * head(e_tail, i))], axis=0),
            jnp.concatenate([v[i], u[i]], axis=0))
    outs = []
    for i in heads:
        mu = jnp.mean(y[i], axis=-1, keepdims=True)
        var = jnp.mean(jnp.square(y[i] - mu), axis=-1, keepdims=True)
        yn = (y[i] - mu) * lax.rsqrt(var + RW_GN_EPS)
        outs.append(yn * head(lng_ref[...], i) + head(lnb_ref[...], i)
                    + jnp.sum(head(rkr_all, i), axis=-1, keepdims=True) * v[i])
    o_ref[0] = (jnp.concatenate(outs, axis=-1) * g_ref[0]).astype(o_ref.dtype)


def _rwkv_core(r, k, v, lw, a, g, k_k, k_a, r_k, ln_g, ln_b):
    B, T, D = r.shape
    C, N, HB = RW_CHUNK, RW_HEAD_DIM, RW_HEADS_PER_STEP
    W = HB * N
    seq = pl.BlockSpec((1, C, W), lambda b, h, c: (b, c, h))
    par = pl.BlockSpec((1, W), lambda b, h, c: (0, h))
    return pl.pallas_call(
        functools.partial(_rwkv_kernel, C=C, N=N, HB=HB),
        grid=(B, D // W, T // C),
        in_specs=[seq] * 6 + [par] * 5,
        out_specs=seq,
        out_shape=jax.ShapeDtypeStruct((B, T, D), BF16),
        scratch_shapes=[pltpu.VMEM((HB, N, N), F32)],
        compiler_params=_params("parallel", "parallel", "arbitrary"),
        name="rwkv_core",
    )(r, k, v, lw, a, g, k_k.reshape(1, D), k_a.reshape(1, D), r_k.reshape(1, D),
      ln_g.reshape(1, D), ln_b.reshape(1, D))


def _shift_mix_bf16(h, mu):
    hs = jnp.pad(h[:, :-1], ((0, 0), (1, 0), (0, 0)))
    return (h + (hs - h) * mu).astype(BF16)


def _rwkv7_mix(h, x_res, gate, mu, w_rkv, w0, w1, w2, a0, a1, a2, g1, g2, k_k, k_a, r_k, ln_g,
               ln_b, w_o):
    B, T, D = h.shape
    M = B * T
    xr, xk, xv, xw, xa, xg = [_shift_mix_bf16(h, mu[j]).reshape(M, D) for j in range(6)]
    r = _matmul(xr, w_rkv, (0, 0))
    k = _matmul(xk, w_rkv, (0, 1))
    v = _matmul(xv, w_rkv, (0, 2))
    w_pre = _matmul(jnp.tanh(_matmul(xw, w1, (0,))), w2, (0,), bias=w0.reshape(1, D))
    log_decay = -jnp.exp(-jax.nn.softplus(-w_pre) - 0.5)
    a = jax.nn.sigmoid(_matmul(_matmul(xa, a1, (0,)), a2, (0,), bias=a0.reshape(1, D)))
    g = _matmul(jax.nn.sigmoid(_matmul(xg, g1, (0,))), g2, (0,))
    sh = (B, T, D)
    o = _rwkv_core(r.reshape(sh), k.reshape(sh), v.reshape(sh), log_decay.reshape(sh),
                   a.reshape(sh), g.reshape(sh), k_k, k_a, r_k, ln_g, ln_b)
    return _matmul(o.reshape(M, D), w_o, (0,), res=x_res.reshape(M, D), gate=gate).reshape(sh)


def _ret_kernel(q_ref, k_ref, v_ref, g_ref, cos_ref, sin_ref, dm_ref, xz_ref, gng_ref, gnb_ref,
                o_ref, s_ref, *, dk):
    @pl.when(pl.program_id(2) == 0)
    def _():
        s_ref[...] = jnp.zeros_like(s_ref)

    cos, sin = cos_ref[...], sin_ref[...]
    half = dk // 2

    def rot(z):
        z1, z2 = z[:, :half], z[:, half:]
        return jnp.concatenate([z1 * cos - z2 * sin, z1 * sin + z2 * cos], axis=-1)

    q = rot(q_ref[0])
    k = rot(k_ref[0] * (dk ** -0.5))
    v = v_ref[0]
    xi, zeta, g_chunk = xz_ref[0, :, 0:1], xz_ref[0, :, 1:2], xz_ref[0, 0:1, 2:3]
    scores = _bdot_nt(q, k) * dm_ref[0]
    state = s_ref[...]
    o = _bdot(scores, v) + _bdot(q * xi, state)
    s_ref[...] = g_chunk * state + _bdot_tn(k * zeta, v)
    mu = jnp.mean(o, axis=-1, keepdims=True)
    var = jnp.mean(jnp.square(o - mu), axis=-1, keepdims=True)
    y = (o - mu) * lax.rsqrt(var + NORM_EPS) * gng_ref[0] + gnb_ref[0]
    gate = g_ref[0]
    o_ref[0] = (y * (gate * jax.nn.sigmoid(gate))).astype(o_ref.dtype)


def _retention_mix(h, x_res, gate, w_in, gn_g, gn_b, w_o):
    B, T, D = h.shape
    M = B * T
    H, dk, dv, C = RET_HEADS, RET_DK, RET_DV, RET_CHUNK
    proj = _matmul(h.reshape(M, D), w_in, (0,)).reshape(B, T, H * (2 * dk + 2 * dv))
    qb, kb, vb, gb = 0, (H * dk) // dk, (2 * H * dk) // dv, (2 * H * dk + H * dv) // dv
    theta = 1.0 / (RET_ROT_BASE ** jnp.linspace(0.0, 1.0, dk // 2, dtype=F32))
    ang = jnp.arange(T, dtype=F32)[:, None] * theta[None, :]
    log_gamma = jnp.log(1.0 - 2.0 ** (-5.0 - jnp.arange(H, dtype=F32)))
    pos = jnp.arange(C, dtype=F32)
    rel = pos[:, None] - pos[None, :]
    dmask = jnp.where(rel >= 0, jnp.exp(jnp.maximum(rel, 0.0) * log_gamma[:, None, None]), 0.0)
    xi = jnp.exp((pos + 1.0)[None, :] * log_gamma[:, None])
    zeta = jnp.exp((C - 1.0 - pos)[None, :] * log_gamma[:, None])
    g_chunk = jnp.broadcast_to(jnp.exp(C * log_gamma)[:, None], (H, C))
    xz = jnp.concatenate([jnp.stack([xi, zeta, g_chunk], axis=-1), jnp.zeros((H, C, 125), F32)], axis=-1)
    o = pl.pallas_call(
        functools.partial(_ret_kernel, dk=dk),
        grid=(B, H, T // C),
        in_specs=[
            pl.BlockSpec((1, C, dk), lambda b, h, c: (b, c, qb + h)),
            pl.BlockSpec((1, C, dk), lambda b, h, c: (b, c, kb + h)),
            pl.BlockSpec((1, C, dv), lambda b, h, c: (b, c, vb + h)),
            pl.BlockSpec((1, C, dv), lambda b, h, c: (b, c, gb + h)),
            pl.BlockSpec((C, dk // 2), lambda b, h, c: (c, 0)),
            pl.BlockSpec((C, dk // 2), lambda b, h, c: (c, 0)),
            pl.BlockSpec((1, C, C), lambda b, h, c: (h, 0, 0)),
            pl.BlockSpec((1, C, 128), lambda b, h, c: (h, 0, 0)),
            pl.BlockSpec((1, 1, dv), lambda b, h, c: (h, 0, 0)),
            pl.BlockSpec((1, 1, dv), lambda b, h, c: (h, 0, 0)),
        ],
        out_specs=pl.BlockSpec((1, C, dv), lambda b, h, c: (b, c, h)),
        out_shape=jax.ShapeDtypeStruct((B, T, H * dv), BF16),
        scratch_shapes=[pltpu.VMEM((dk, dv), F32)],
        compiler_params=_params("parallel", "parallel", "arbitrary"),
        name="retention_core",
    )(proj, proj, proj, proj, jnp.cos(ang), jnp.sin(ang), dmask, xz,
      gn_g.reshape(H, 1, dv), gn_b.reshape(H, 1, dv))
    return _matmul(o.reshape(M, H * dv), w_o, (0,), res=x_res.reshape(M, D), gate=gate).reshape(B, T, D)


def _gla_kernel(q_ref, k_ref, v_ref, g_ref, la_ref, gn_ref, o_ref, s_ref, *, C, dk):
    @pl.when(pl.program_id(2) == 0)
    def _():
        s_ref[...] = jnp.zeros_like(s_ref)

    row = lax.broadcasted_iota(jnp.int32, (C, C), 0)
    col = lax.broadcasted_iota(jnp.int32, (C, C), 1)
    causal = row >= col
    la = la_ref[0]
    b = _exact_lhs_dot(causal.astype(F32), la)
    b_last = b[C - 1:C, :]
    eye_k = (lax.broadcasted_iota(jnp.int32, (dk, dk), 0) == lax.broadcasted_iota(jnp.int32, (dk, dk), 1))
    d_last_col = jnp.exp(jnp.sum(jnp.where(eye_k, b_last, 0.0), axis=1, keepdims=True))
    k, v = k_ref[0], v_ref[0]
    q_in = q_ref[0] * (dk ** -0.5) * jnp.exp(b)
    k_in = k * jnp.exp(-b)
    att = jnp.where(causal, _bdot_nt(q_in, k_in), 0.0)
    state = s_ref[...]
    o = _bdot(att, v) + _bdot(q_in, state)
    s_ref[...] = d_last_col * state + _bdot_tn(k * jnp.exp(b_last - b), v)
    y = o * lax.rsqrt(jnp.mean(o * o, axis=-1, keepdims=True) + NORM_EPS) * gn_ref[...]
    gate = g_ref[0]
    o_ref[0] = (y * (gate * jax.nn.sigmoid(gate))).astype(o_ref.dtype)


def _gla_mix(h, x_res, gate, w_in, w_a1, w_a2, b_a, gn_g, w_o):
    B, T, D = h.shape
    M = B * T
    H, dk, dv, C = GLA_HEADS, GLA_DK, GLA_DV, GLA_CHUNK
    h2 = h.reshape(M, D)
    proj = _matmul(h2, w_in, (0,)).reshape(B, T, H * (2 * dk + 2 * dv))
    z = _matmul(_matmul(h2, w_a1, (0,)), w_a2, (0,), bias=b_a.reshape(1, H * dk))
    log_a = (jax.nn.log_sigmoid(z) / GLA_GATE_NORM).reshape(B, T, H * dk)
    qb, kb, vb, gb = 0, H, (2 * H * dk) // dv, (2 * H * dk + H * dv) // dv
    o = pl.pallas_call(
        functools.partial(_gla_kernel, C=C, dk=dk),
        grid=(B, H, T // C),
        in_specs=[
            pl.BlockSpec((1, C, dk), lambda b, h, c: (b, c, qb + h)),
            pl.BlockSpec((1, C, dk), lambda b, h, c: (b, c, kb + h)),
            pl.BlockSpec((1, C, dv), lambda b, h, c: (b, c, vb + h)),
            pl.BlockSpec((1, C, dv), lambda b, h, c: (b, c, gb + h)),
            pl.BlockSpec((1, C, dk), lambda b, h, c: (b, c, h)),
            pl.BlockSpec((1, dv), lambda b, h, c: (0, 0)),
        ],
        out_specs=pl.BlockSpec((1, C, dv), lambda b, h, c: (b, c, h)),
        out_shape=jax.ShapeDtypeStruct((B, T, H * dv), BF16),
        scratch_shapes=[pltpu.VMEM((dk, dv), F32)],
        compiler_params=_params("parallel", "parallel", "arbitrary"),
        name="gla_core",
    )(proj, proj, proj, proj, log_a, gn_g.reshape(1, dv))
    return _matmul(o.reshape(M, H * dv), w_o, (0,), res=x_res.reshape(M, D), gate=gate).reshape(B, T, D)


def _nsa_cmp_kernel(q_ref, kc_ref, vc_ref, o_ref, sel_ref, *, tq, n_cmp, n_pad, n_slc, hpg):
    L, S, Ls, hd = NSA_CMP_BLK, NSA_CMP_STRIDE, NSA_SEL_BLK, NSA_HD
    t_pos = pl.program_id(2) * tq + lax.broadcasted_iota(jnp.int32, (tq, 1), 0)
    n_ix = lax.broadcasted_iota(jnp.int32, (1, n_pad), 1)
    valid = (n_ix * S + (L - 1) <= t_pos) & (n_ix < n_cmp)
    validf = valid.astype(F32)
    c_start = lax.broadcasted_iota(jnp.int32, (n_pad, n_slc), 0) * S
    s_start = lax.broadcasted_iota(jnp.int32, (n_pad, n_slc), 1) * Ls
    overlap = ((c_start <= s_start + (Ls - 1)) & (c_start + (L - 1) >= s_start)
               & (c_start < n_cmp * S)).astype(F32)
    kc, vc = kc_ref[0, 0], vc_ref[0, 0]
    imp = jnp.zeros((tq, n_slc), F32)
    outs = []
    for hh in range(hpg):
        s = _fdot_nt(q_ref[0, :, hh * hd:(hh + 1) * hd], kc)
        s = jnp.where(valid, s, NEG_INF)
        e = jnp.exp(s - jnp.max(s, axis=-1, keepdims=True))
        p = e / jnp.sum(e, axis=-1, keepdims=True) * validf
        outs.append(_bdot(p, vc))
        imp = imp + _exact_rhs_dot(p, overlap)
    o_ref[0] = jnp.concatenate(outs, axis=-1)

    j = lax.broadcasted_iota(jnp.int32, (1, n_slc), 1)
    cur = t_pos // Ls
    forced = (j == 0) | (j == cur) | (j == cur - 1)
    imp = jnp.where(j > cur, -IMP_BIG, jnp.where(forced, IMP_BIG, imp))
    rank = jnp.zeros((tq, n_slc), jnp.int32)
    for jp in range(n_slc):
        c = imp[:, jp:jp + 1]
        rank = rank + ((c > imp) | ((c == imp) & (jp < j))).astype(jnp.int32)
    sel_ref[0, 0] = (rank < min(NSA_SEL_TOPK, n_slc)).astype(F32)


def _nsa_att_kernel(q_ref, ks_ref, vs_ref, kw_ref, vw_ref, sel_ref, oc_ref, gt_ref, o_ref,
                    m_ref, l_ref, acc_ref, *, tq, tk, T, n_slc, hpg):
    Ls, W, hd = NSA_SEL_BLK, NSA_WINDOW, NSA_HD
    qi = pl.program_id(2)
    t0 = qi * tq
    q = jnp.concatenate([q_ref[0, :, hh * hd:(hh + 1) * hd] for hh in range(hpg)], axis=0)
    t_q = t0 + lax.broadcasted_iota(jnp.int32, (tq, 1), 0)
    sel = sel_ref[0, 0]

    m_ref[...] = jnp.full_like(m_ref, NEG_INF)
    l_ref[...] = jnp.zeros_like(l_ref)
    acc_ref[...] = jnp.zeros_like(acc_ref)

    def body(kb, carry):
        k0 = pl.multiple_of(kb * tk, tk)
        kt = ks_ref[0, pl.ds(k0, tk), :]
        vt = vs_ref[0, pl.ds(k0, tk), :]
        blk_of_key = (k0 + lax.broadcasted_iota(jnp.int32, (n_slc, tk), 1)) // Ls
        expand = (blk_of_key == lax.broadcasted_iota(jnp.int32, (n_slc, tk), 0)).astype(F32)
        kpos = k0 + lax.broadcasted_iota(jnp.int32, (1, tk), 1)
        bias = jnp.where((_bdot(sel, expand) > 0.5) & (kpos <= t_q), 0.0, NEG_INF)
        s_all = _bdot_nt(q, kt)
        ps = []
        for hh in range(hpg):
            rows = slice(hh * tq, (hh + 1) * tq)
            s = s_all[rows] + bias
            m_old = m_ref[rows]
            m_new = jnp.maximum(m_old, jnp.max(s, axis=-1, keepdims=True))
            alpha = jnp.exp(m_old - m_new)
            p = jnp.exp(s - m_new)
            l_ref[rows] = alpha * l_ref[rows] + jnp.sum(p, axis=-1, keepdims=True)
            acc_ref[rows] = alpha * acc_ref[rows]
            m_ref[rows] = m_new
            ps.append(p.astype(BF16))
        acc_ref[...] += jnp.dot(jnp.concatenate(ps, axis=0), vt, preferred_element_type=F32)
        return carry

    lax.fori_loop(0, (t0 + tq + tk - 1) // tk, body, 0)
    o_sel = acc_ref[...] / l_ref[...]

    span = W + tq
    w0 = pl.multiple_of(jnp.maximum(t0 - W, 0), tq)
    kw = kw_ref[0, pl.ds(w0, span), :]
    vw = vw_ref[0, pl.ds(w0, span), :]
    wpos = w0 + lax.broadcasted_iota(jnp.int32, (1, span), 1)
    bias_w = jnp.where((wpos <= t_q) & (wpos > t_q - W), 0.0, NEG_INF)
    s_all = _bdot_nt(q, kw)
    es, ls = [], []
    for hh in range(hpg):
        s = s_all[hh * tq:(hh + 1) * tq] + bias_w
        e = jnp.exp(s - jnp.max(s, axis=-1, keepdims=True))
        ls.append(jnp.sum(e, axis=-1, keepdims=True))
        es.append(e.astype(BF16))
    o_win = (jnp.dot(jnp.concatenate(es, axis=0), vw, preferred_element_type=F32)
             / jnp.concatenate(ls, axis=0))

    gt = jax.nn.sigmoid(gt_ref[0, 0])
    outs = []
    for hh in range(hpg):
        rows = slice(hh * tq, (hh + 1) * tq)
        outs.append(gt[:, hh:hh + 1] * oc_ref[0, :, hh * hd:(hh + 1) * hd]
                    + gt[:, hpg + hh:hpg + hh + 1] * o_sel[rows]
                    + gt[:, 2 * hpg + hh:2 * hpg + hh + 1] * o_win[rows])
    o_ref[0] = jnp.concatenate(outs, axis=-1).astype(o_ref.dtype)


def _rms_norm(x, g):
    return x * lax.rsqrt(jnp.mean(x * x, axis=-1, keepdims=True) + NORM_EPS) * g


def _nsa_mix(h, x_res, gate, w_in, q_g, k_g, cmp_pos, cmp_w1, cmp_w2, w_o):
    B, T, D = h.shape
    M = B * T
    H, G, hd = NSA_HEADS, NSA_KV_HEADS, NSA_HD
    hpg = H // G
    L, S, Ls = NSA_CMP_BLK, NSA_CMP_STRIDE, NSA_SEL_BLK
    kvw = G * hd
    n_main = H * hd + 6 * kvw
    h2 = h.reshape(M, D)
    proj = _matmul(h2, w_in, (0,), n_cols=n_main)
    gate_w = jnp.pad(w_in[0, :, n_main:], ((0, 0), (0, 128 - 3 * H)))
    gates = _matmul(h2, gate_w)[:, :3 * H]
    q = _rms_norm(proj[:, :H * hd].reshape(B, T, H, hd), q_g) * (hd ** -0.5)
    q = q.reshape(B, T, H * hd)

    def kv(i):
        return proj[:, H * hd + i * kvw:H * hd + (i + 1) * kvw].reshape(B, T, G, hd)

    n_cmp = (T - L) // S + 1
    n_grp = T // S
    assert L == 2 * S and n_cmp == n_grp - 1

    def compress(z, pos, w1, w2):
        zg = z.reshape(B, n_grp, S, G, hd).transpose(0, 3, 1, 2, 4).reshape(B * G * n_grp, S * hd)
        w_halves = jnp.concatenate([w1[:S * hd], w1[S * hd:]], axis=1)
        part = _matmul(zg, w_halves, tm=min(512, B * G * n_grp)).reshape(B, G, n_grp, 2 * hd)
        pos_term = _matmul(jnp.broadcast_to(pos.reshape(1, L * hd), (8, L * hd)), w1, tm=8)[0]
        pre = part[:, :, :-1, :hd] + part[:, :, 1:, hd:] + pos_term
        pre = jnp.pad(pre, ((0, 0), (0, 0), (0, 1), (0, 0))).reshape(B * G * n_grp, hd)
        return _matmul(jax.nn.silu(pre), w2, tm=min(512, B * G * n_grp)).reshape(B, G, n_grp, hd)

    kc = _rms_norm(compress(kv(0), cmp_pos[0], cmp_w1[0], cmp_w2[0]), k_g[0])
    vc = compress(kv(1), cmp_pos[1], cmp_w1[1], cmp_w2[1])
    k_sel = _rms_norm(kv(2), k_g[1]).reshape(B, T, kvw).astype(BF16)
    v_sel = kv(3).reshape(B, T, kvw).astype(BF16)
    k_win = _rms_norm(kv(4), k_g[2]).reshape(B, T, kvw).astype(BF16)
    v_win = kv(5).reshape(B, T, kvw).astype(BF16)

    n_slc = T // Ls
    tq = min(NSA_CMP_TQ, T)
    o_cmp, sel = pl.pallas_call(
        functools.partial(_nsa_cmp_kernel, tq=tq, n_cmp=n_cmp, n_pad=n_grp, n_slc=n_slc, hpg=hpg),
        grid=(B, G, T // tq),
        in_specs=[
            pl.BlockSpec((1, tq, hpg * hd), lambda b, g, i: (b, i, g)),
            pl.BlockSpec((1, 1, n_grp, hd), lambda b, g, i: (b, g, 0, 0)),
            pl.BlockSpec((1, 1, n_grp, hd), lambda b, g, i: (b, g, 0, 0)),
        ],
        out_specs=[
            pl.BlockSpec((1, tq, hpg * hd), lambda b, g, i: (b, i, g)),
            pl.BlockSpec((1, 1, tq, n_slc), lambda b, g, i: (b, g, i, 0)),
        ],
        out_shape=[jax.ShapeDtypeStruct((B, T, H * hd), F32),
                   jax.ShapeDtypeStruct((B, G, T, n_slc), F32)],
        compiler_params=_params("parallel", "parallel", "parallel"),
        name="nsa_compressed",
    )(q, kc, vc)

    gt = gates.reshape(B, T, 3, G, hpg).transpose(0, 3, 1, 2, 4).reshape(B, G, T, 3 * hpg)
    tq = min(NSA_ATT_TQ, T)
    tk = min(NSA_ATT_TK, T)
    assert T >= NSA_WINDOW + tq
    kv_spec = pl.BlockSpec((1, T, hd), lambda b, g, i: (b, 0, g))
    o = pl.pallas_call(
        functools.partial(_nsa_att_kernel, tq=tq, tk=tk, T=T, n_slc=n_slc, hpg=hpg),
        grid=(B, G, T // tq),
        in_specs=[
            pl.BlockSpec((1, tq, hpg * hd), lambda b, g, i: (b, i, g)),
            kv_spec, kv_spec, kv_spec, kv_spec,
            pl.BlockSpec((1, 1, tq, n_slc), lambda b, g, i: (b, g, i, 0)),
            pl.BlockSpec((1, tq, hpg * hd), lambda b, g, i: (b, i, g)),
            pl.BlockSpec((1, 1, tq, 3 * hpg), lambda b, g, i: (b, g, i, 0)),
        ],
        out_specs=pl.BlockSpec((1, tq, hpg * hd), lambda b, g, i: (b, i, g)),
        out_shape=jax.ShapeDtypeStruct((B, T, H * hd), BF16),
        scratch_shapes=[pltpu.VMEM((hpg * tq, 1), F32), pltpu.VMEM((hpg * tq, 1), F32),
                        pltpu.VMEM((hpg * tq, hd), F32)],
        compiler_params=_params("parallel", "parallel", "arbitrary"),
        name="nsa_selected_window",
    )(q.astype(BF16), k_sel, v_sel, k_win, v_win, sel, o_cmp, gt)
    return _matmul(o.reshape(M, H * hd), w_o, (0,), res=x_res.reshape(M, D), gate=gate).reshape(B, T, D)


def _moe_up_kernel(blk_e_ref, x_ref, w_ref, b_ref, o_ref):
    gu = _bdot(x_ref[...], w_ref[...]) + b_ref[0]
    x_glu = jnp.minimum(gu[:, :D_EXPERT], SWIGLU_LIMIT)
    x_lin = jnp.clip(gu[:, D_EXPERT:], -SWIGLU_LIMIT, SWIGLU_LIMIT)
    o_ref[...] = (x_glu * jax.nn.sigmoid(SWIGLU_ALPHA * x_glu) * (x_lin + 1.0)).astype(o_ref.dtype)


def _moe_down_kernel(blk_e_ref, a_ref, w_ref, b_ref, o_ref):
    o_ref[...] = _bdot(a_ref[...], w_ref[...]) + b_ref[0]


def _router_kernel(x_ref, w_ref, b_ref, e_ref, p_ref):
    logits = _fdot(x_ref[...], w_ref[...]) + b_ref[...]
    lane = lax.broadcasted_iota(jnp.int32, logits.shape, 1)
    vals, idxs = [], []
    for _ in range(TOP_K):
        m = jnp.max(logits, axis=-1, keepdims=True)
        idx = jnp.min(jnp.where(logits == m, lane, logits.shape[1]), axis=-1, keepdims=True)
        vals.append(m)
        idxs.append(idx)
        logits = jnp.where(lane == idx, -IMP_BIG, logits)
    es = [jnp.exp(v - vals[0]) for v in vals]
    total = sum(es)
    e_out = jnp.zeros(logits.shape, jnp.int32)
    p_out = jnp.zeros(logits.shape, F32)
    for k in range(TOP_K):
        e_out = jnp.where(lane == k, idxs[k], e_out)
        p_out = jnp.where(lane == k, es[k] / total, p_out)
    e_ref[...] = e_out
    p_ref[...] = p_out


def _route(x, w_router, b_router, layer):
    N, D = x.shape
    E, K = N_EXPERTS, TOP_K
    lanes = 128
    w_r = jnp.pad(w_router[layer], ((0, 0), (0, lanes - E)))
    b_r = jnp.concatenate([b_router[layer], jnp.full((lanes - E,), NEG_INF, F32)]).reshape(1, lanes)
    tm = min(N, 512)
    top_e, top_w = pl.pallas_call(
        _router_kernel,
        grid=(N // tm,),
        in_specs=[pl.BlockSpec((tm, D), lambda i: (i, 0)),
                  pl.BlockSpec((D, lanes), lambda i: (0, 0)),
                  pl.BlockSpec((1, lanes), lambda i: (0, 0))],
        out_specs=[pl.BlockSpec((tm, lanes), lambda i: (i, 0)),
                   pl.BlockSpec((tm, lanes), lambda i: (i, 0))],
        out_shape=[jax.ShapeDtypeStruct((N, lanes), jnp.int32),
                   jax.ShapeDtypeStruct((N, lanes), F32)],
        compiler_params=_params("parallel"),
        name="moe_router",
    )(x, w_r, b_r)
    return top_e[:, :K], top_w[:, :K]


def _moe_ffn(h, layer, x_res, gate, w_router, b_router, w_gu, b_gu, w_down, b_down):
    B, T, D = h.shape
    N = B * T
    E, K, R = N_EXPERTS, TOP_K, MOE_ROWS
    NK = N * K
    x = h.reshape(N, D)
    top_e, top_w = _route(x, w_router, b_router, layer)
    pair = jnp.arange(NK, dtype=jnp.int32)
    e_s, order = lax.sort((top_e.reshape(-1), pair), num_keys=1, is_stable=True)
    one_hot = e_s[:, None] == jnp.arange(E, dtype=jnp.int32)[None, :]
    counts = jnp.sum(one_hot, axis=0, dtype=jnp.int32)
    padded = (counts + R - 1) // R * R
    g_start = jnp.cumsum(counts) - counts
    p_end = jnp.cumsum(padded)
    shift = (p_end - padded) - g_start
    dest = pair + jnp.sum(jnp.where(one_hot, shift[None, :], 0), axis=1)
    n_blocks = (NK + R - 1) // R + E
    cap = n_blocks * R
    blk_start = jnp.arange(n_blocks, dtype=jnp.int32) * R
    blk_e = jnp.minimum(jnp.sum(p_end[None, :] <= blk_start[:, None], axis=1), E - 1).astype(jnp.int32)
    blk_shift = jnp.sum(jnp.where(blk_e[:, None] == jnp.arange(E)[None, :], shift[None, :], 0), axis=1)
    src = jnp.clip(jnp.arange(cap, dtype=jnp.int32) - jnp.repeat(blk_shift, R), 0, NK - 1)
    tok = jnp.take(order, src, axis=0) // K
    xb = jnp.take(x.astype(BF16), tok, axis=0)

    act = pl.pallas_call(
        _moe_up_kernel,
        grid_spec=pltpu.PrefetchScalarGridSpec(
            num_scalar_prefetch=1,
            grid=(n_blocks,),
            in_specs=[
                pl.BlockSpec((R, D), lambda i, be: (i, 0)),
                pl.BlockSpec((None, None, D, 2 * D_EXPERT), lambda i, be: (layer, be[i], 0, 0)),
                pl.BlockSpec((None, 1, 1, 2 * D_EXPERT), lambda i, be: (layer, be[i], 0, 0)),
            ],
            out_specs=pl.BlockSpec((R, D_EXPERT), lambda i, be: (i, 0)),
        ),
        out_shape=jax.ShapeDtypeStruct((cap, D_EXPERT), BF16),
        compiler_params=_params("arbitrary"),
        name="moe_up",
    )(blk_e, xb, w_gu, b_gu.reshape(b_gu.shape[0], E, 1, 2 * D_EXPERT))

    out = pl.pallas_call(
        _moe_down_kernel,
        grid_spec=pltpu.PrefetchScalarGridSpec(
            num_scalar_prefetch=1,
            grid=(n_blocks,),
            in_specs=[
                pl.BlockSpec((R, D_EXPERT), lambda i, be: (i, 0)),
                pl.BlockSpec((None, None, D_EXPERT, D), lambda i, be: (layer, be[i], 0, 0)),
                pl.BlockSpec((None, 1, 1, D), lambda i, be: (layer, be[i], 0, 0)),
            ],
            out_specs=pl.BlockSpec((R, D), lambda i, be: (i, 0)),
        ),
        out_shape=jax.ShapeDtypeStruct((cap, D), F32),
        compiler_params=_params("arbitrary"),
        name="moe_down",
    )(blk_e, act, w_down, b_down.reshape(b_down.shape[0], E, 1, D))

    _, slot = lax.sort((order, dest), num_keys=1)
    rows = jnp.take(out, slot, axis=0).reshape(N, K, D)
    y = jnp.sum(rows * top_w[:, :, None], axis=1)
    return x_res + gate * y.reshape(B, T, D)


def kernel(x, c, ada_w, ada_b, norm_g, rw_mu, rw_w_rkv, rw_w0, rw_w1, rw_w2, rw_a0, rw_a1, rw_a2, rw_g1, rw_g2, rw_k_k, rw_k_a, rw_r_k, rw_ln_g, rw_ln_b, rw_w_o, ret_w_in, ret_gn_g, ret_gn_b, ret_w_o, gla_w_in, gla_w_a1, gla_w_a2, gla_b_a, gla_gn_g, gla_w_o, nsa_w_in, nsa_q_g, nsa_k_g, nsa_cmp_pos, nsa_cmp_w1, nsa_cmp_w2, nsa_w_o, moe_router_w, moe_router_b, moe_w_gu, moe_b_gu, moe_w_down, moe_b_down):
    B, T, D = x.shape
    depth = ada_w.shape[0]
    c_act = jnp.pad(jax.nn.silu(c), ((0, 8 - B), (0, 0)))
    for i in range(depth):
        mod = _matmul(c_act, ada_w, (i,), tm=8, tn=1024)[:B] + ada_b[i]
        sh1, sc1, gt1, sh2, sc2, gt2 = jnp.split(mod, 6, axis=-1)
        gt1, gt2 = gt1.reshape(B, 1, D), gt2.reshape(B, 1, D)
        m, j = i % 4, i // 4
        if m == 0:
            h = _norm_modulate(x, norm_g[i, 0], sc1, sh1, F32)
            x = _rwkv7_mix(h, x, gt1, rw_mu[j], rw_w_rkv[j:j + 1], rw_w0[j], rw_w1[j:j + 1], rw_w2[j:j + 1],
                           rw_a0[j], rw_a1[j:j + 1], rw_a2[j:j + 1], rw_g1[j:j + 1], rw_g2[j:j + 1],
                           rw_k_k[j], rw_k_a[j], rw_r_k[j], rw_ln_g[j], rw_ln_b[j], rw_w_o[j:j + 1])
        elif m == 1:
            h = _norm_modulate(x, norm_g[i, 0], sc1, sh1, BF16)
            x = _retention_mix(h, x, gt1, ret_w_in[j:j + 1], ret_gn_g[j], ret_gn_b[j], ret_w_o[j:j + 1])
        elif m == 2:
            h = _norm_modulate(x, norm_g[i, 0], sc1, sh1, BF16)
            x = _gla_mix(h, x, gt1, gla_w_in[j:j + 1], gla_w_a1[j:j + 1], gla_w_a2[j:j + 1], gla_b_a[j],
                         gla_gn_g[j], gla_w_o[j:j + 1])
        else:
            h = _norm_modulate(x, norm_g[i, 0], sc1, sh1, BF16)
            x = _nsa_mix(h, x, gt1, nsa_w_in[j:j + 1], nsa_q_g[j], nsa_k_g[j], nsa_cmp_pos[j],
                         nsa_cmp_w1[j], nsa_cmp_w2[j], nsa_w_o[j:j + 1])
        h = _norm_modulate(x, norm_g[i, 1], sc2, sh2, F32)
        x = _moe_ffn(h, i, x, gt2, moe_router_w, moe_router_b, moe_w_gu, moe_b_gu, moe_w_down, moe_b_down)
    return x
```

```python
import functools
import math

import jax
import jax.numpy as jnp
from jax import lax
from jax.experimental import pallas as pl
from jax.experimental.pallas import tpu as pltpu

F32 = jnp.float32
BF16 = jnp.bfloat16
HIGHEST = lax.Precision.HIGHEST

NORM_EPS = 1e-6
NEG_INF = -1e30

RW_HEAD_DIM = 64
RW_GN_EPS = 64e-5
RW_CHUNK = 64
RW_HEADS_PER_STEP = 8

RET_HEADS = 8
RET_DK = 256
RET_DV = 512
RET_CHUNK = 128
RET_ROT_BASE = 10000.0

GLA_HEADS = 4
GLA_DK = 256
GLA_DV = 512
GLA_GATE_NORM = 16.0
GLA_CHUNK = 64

NSA_HEADS = 16
NSA_KV_HEADS = 4
NSA_HD = 128
NSA_CMP_BLK = 32
NSA_CMP_STRIDE = 16
NSA_SEL_BLK = 64
NSA_SEL_TOPK = 16
NSA_WINDOW = 512
NSA_CMP_TQ = 256
NSA_ATT_TQ = 128
NSA_ATT_TK = 512
IMP_BIG = 3e38

N_EXPERTS = 32
TOP_K = 4
D_EXPERT = 768
SWIGLU_ALPHA = 1.702
SWIGLU_LIMIT = 7.0
MOE_ROWS = 256
MOE_COMBINE_TOKENS = 64
ROUTER_ROWS = 8

VMEM_LIMIT_BYTES = 52 * 1024 * 1024


def _params(*sem):
    return pltpu.CompilerParams(dimension_semantics=sem, vmem_limit_bytes=VMEM_LIMIT_BYTES)


def _bdot(a, b):
    return jnp.dot(a.astype(BF16), b.astype(BF16), preferred_element_type=F32)


def _bdot_nt(a, b):
    return lax.dot_general(a.astype(BF16), b.astype(BF16), (((1,), (1,)), ((), ())),
                           preferred_element_type=F32)


def _bdot_tn(a, b):
    return lax.dot_general(a.astype(BF16), b.astype(BF16), (((0,), (0,)), ((), ())),
                           preferred_element_type=F32)


def _fdot(a, b):
    return jnp.dot(a, b, precision=HIGHEST, preferred_element_type=F32)


def _fdot_nt(a, b):
    return lax.dot_general(a, b, (((1,), (1,)), ((), ())), precision=HIGHEST,
                           preferred_element_type=F32)


def _split3(x):
    hi = x.astype(BF16)
    rem = x - hi.astype(F32)
    mid = rem.astype(BF16)
    return hi, mid, (rem - mid.astype(F32)).astype(BF16)


def _exact_lhs_dot(m, x):
    mb = m.astype(BF16)
    return sum(jnp.dot(mb, part, preferred_element_type=F32) for part in _split3(x))


def _exact_rhs_dot(x, m):
    mb = m.astype(BF16)
    return sum(jnp.dot(part, mb, preferred_element_type=F32) for part in _split3(x))


def _softplus(z):
    return jnp.maximum(z, 0.0) + jnp.log(1.0 + jnp.exp(-jnp.abs(z)))


_POST = {
    None: lambda z: z,
    "tanh": jnp.tanh,
    "sigmoid": jax.nn.sigmoid,
    "silu": jax.nn.silu,
    "rwkv_log_decay": lambda z: -jnp.exp(-_softplus(-z) - 0.5),
    "gla_log_gate": lambda z: -_softplus(-z) / GLA_GATE_NORM,
}


def _mm_kernel(*refs, has_bias, has_res, post):
    x_ref, w_ref = refs[0], refs[1]
    pos = 2
    acc = _bdot(x_ref[...], w_ref[...])
    if has_bias:
        acc = acc + refs[pos][...]
        pos += 1
    acc = _POST[post](acc)
    if has_res:
        acc = refs[pos][...] + refs[pos + 1][0] * acc
        pos += 2
    o_ref = refs[pos]
    o_ref[...] = acc.astype(o_ref.dtype)


def _matmul(x, w, lead=(), *, bias=None, res=None, gate=None, n_cols=None, tm=None, tn=None,
            post=None, out_dtype=F32, name="matmul"):
    M, K = x.shape
    N = n_cols if n_cols is not None else w.shape[-1]
    if tm is None:
        tm = min(M, 1024 if x.dtype == BF16 else 512)
    if tn is None:
        tn = min(N, 512) if N % 128 == 0 else N
    assert M % tm == 0
    nlead = len(lead)
    in_specs = [
        pl.BlockSpec((tm, K), lambda i, j: (i, 0)),
        pl.BlockSpec((None,) * nlead + (K, tn), lambda i, j: tuple(lead) + (0, j)),
    ]
    args = [x, w]
    if bias is not None:
        in_specs.append(pl.BlockSpec((1, tn), lambda i, j: (0, j)))
        args.append(bias)
    if res is not None:
        rows_per_gate = M // gate.shape[0]
        assert rows_per_gate % tm == 0
        in_specs.append(pl.BlockSpec((tm, tn), lambda i, j: (i, j)))
        in_specs.append(pl.BlockSpec((1, 1, tn), lambda i, j: ((i * tm) // rows_per_gate, 0, j)))
        args += [res, gate]
    return pl.pallas_call(
        functools.partial(_mm_kernel, has_bias=bias is not None, has_res=res is not None, post=post),
        grid=(M // tm, pl.cdiv(N, tn)),
        in_specs=in_specs,
        out_specs=pl.BlockSpec((tm, tn), lambda i, j: (i, j)),
        out_shape=jax.ShapeDtypeStruct((M, N), out_dtype),
        compiler_params=_params("parallel", "parallel"),
        name=name,
    )(*args)


def _modulated_norm(x, g, sc, sh):
    y = x * lax.rsqrt(jnp.mean(x * x, axis=-1, keepdims=True) + NORM_EPS) * g
    return y * (1.0 + sc) + sh


def _normmod_kernel(x_ref, g_ref, sc_ref, sh_ref, *o_refs):
    h = _modulated_norm(x_ref[0], g_ref[...], sc_ref[0], sh_ref[0])
    for o_ref in o_refs:
        o_ref[0] = h.astype(o_ref.dtype)


def _norm_modulate(x, g, sc, sh, out_dtypes):
    B, T, D = x.shape
    tr = min(T, 512)
    row = pl.BlockSpec((1, tr, D), lambda b, i: (b, i, 0))
    per_batch = pl.BlockSpec((1, 1, D), lambda b, i: (b, 0, 0))
    return pl.pallas_call(
        _normmod_kernel,
        grid=(B, T // tr),
        in_specs=[row, pl.BlockSpec((1, D), lambda b, i: (0, 0)), per_batch, per_batch],
        out_specs=[row] * len(out_dtypes),
        out_shape=[jax.ShapeDtypeStruct((B, T, D), dt) for dt in out_dtypes],
        compiler_params=_params("parallel", "parallel"),
        name="norm_modulate",
    )(x, g.reshape(1, D), sc.reshape(B, 1, D), sh.reshape(B, 1, D))


def _rwkv_prep_kernel(x_ref, halo_ref, g_ref, sc_ref, sh_ref, mu_ref, *o_refs, halo_rows):
    h = _modulated_norm(x_ref[0], g_ref[...], sc_ref[0], sh_ref[0])
    h_halo = _modulated_norm(halo_ref[0], g_ref[...], sc_ref[0], sh_ref[0])
    first = jnp.where(pl.program_id(1) > 0, h_halo[halo_rows - 1:halo_rows, :], 0.0)
    row = lax.broadcasted_iota(jnp.int32, h.shape, 0)
    d = jnp.where(row == 0, first, pltpu.roll(h, 1, axis=0)) - h
    for j, o_ref in enumerate(o_refs):
        o_ref[0] = (h + d * mu_ref[j:j + 1, :]).astype(o_ref.dtype)


def _rwkv_prep(x, g, sc, sh, mu):
    B, T, D = x.shape
    tr = min(T, 512)
    hr = 8
    n_mix = mu.shape[0]
    row = pl.BlockSpec((1, tr, D), lambda b, i: (b, i, 0))
    per_batch = pl.BlockSpec((1, 1, D), lambda b, i: (b, 0, 0))
    return pl.pallas_call(
        functools.partial(_rwkv_prep_kernel, halo_rows=hr),
        grid=(B, T // tr),
        in_specs=[row,
                  pl.BlockSpec((1, hr, D), lambda b, i: (b, jnp.maximum(i * (tr // hr) - 1, 0), 0)),
                  pl.BlockSpec((1, D), lambda b, i: (0, 0)), per_batch, per_batch,
                  pl.BlockSpec((n_mix, D), lambda b, i: (0, 0))],
        out_specs=[row] * n_mix,
        out_shape=[jax.ShapeDtypeStruct((B, T, D), BF16)] * n_mix,
        compiler_params=_params("parallel", "parallel"),
        name="rwkv_shift_mix",
    )(x, x, g.reshape(1, D), sc.reshape(B, 1, D), sh.reshape(B, 1, D), mu)


def _rwkv_kernel(r_ref, k_ref, v_ref, lw_ref, a_ref, g_ref, kk_ref, ka_ref, rk_ref, lng_ref,
                 lnb_ref, o_ref, h_ref, *, C, N, HB):
    @pl.when(pl.program_id(2) == 0)
    def _():
        h_ref[...] = jnp.zeros_like(h_ref)

    row = lax.broadcasted_iota(jnp.int32, (C, C), 0)
    col = lax.broadcasted_iota(jnp.int32, (C, C), 1)
    incl = row >= col
    strict = row > col
    eye = (row == col).astype(F32)
    eye_n = (lax.broadcasted_iota(jnp.int32, (N, N), 0) == lax.broadcasted_iota(jnp.int32, (N, N), 1))
    n_double = int(math.log2(C)) - 1
    heads = range(HB)

    def head(x, i):
        return x[:, i * N:(i + 1) * N]

    r_all, k_all, v_all, a_all, lw_all = r_ref[0], k_ref[0], v_ref[0], a_ref[0], lw_ref[0]
    cum_all = _exact_lhs_dot(incl.astype(F32), lw_all)
    tot_all = cum_all[C - 1:C, :]
    e_neg = jnp.exp(-cum_all)
    e_tail = jnp.exp(tot_all - cum_all)
    kx_all = k_all * kk_ref[...]
    kp_all = k_all * (1.0 + (a_all - 1.0) * ka_ref[...])
    rt_all = r_all * jnp.exp(cum_all)
    kn_all = kp_all * e_neg
    kt_all = kp_all * e_tail
    ep_all = jnp.exp(cum_all - lw_all)
    rkr_all = r_all * kp_all * rk_ref[...]

    kappa = [head(kx_all, i) / jnp.maximum(
        jnp.sqrt(jnp.sum(jnp.square(head(kx_all, i)), axis=-1, keepdims=True)), 1e-12) for i in heads]
    b = [kappa[i] * head(a_all, i) for i in heads]
    kap_t = [kappa[i] * head(ep_all, i) for i in heads]
    r_t = [head(rt_all, i) for i in heads]
    v = [head(v_all, i) for i in heads]
    big = [_bdot_nt(jnp.concatenate([kap_t[i], r_t[i]], axis=0),
                    jnp.concatenate([b[i] * head(e_neg, i), head(kn_all, i)], axis=0))
           for i in heads]
    t_k = [jnp.where(strict, big[i][:C, C:], 0.0) for i in heads]
    m_b = [jnp.where(incl, big[i][C:, :C], 0.0) for i in heads]
    m_k = [jnp.where(incl, big[i][C:, C:], 0.0) for i in heads]
    p = [jnp.where(strict, -big[i][:C, :C], 0.0) for i in heads]
    inv = [eye + p[i] for i in heads]
    tkv = [_bdot(t_k[i], v[i]) for i in heads]
    for _ in range(n_double):
        p = [_bdot(p[i], p[i]) for i in heads]
        inv = [inv[i] + _bdot(inv[i], p[i]) for i in heads]
    h0 = [h_ref[i] for i in heads]
    aw = [_bdot(inv[i], jnp.concatenate([kap_t[i], tkv[i]], axis=1)) for i in heads]
    u = [_bdot(aw[i][:, :N], h0[i]) + aw[i][:, N:] for i in heads]
    y = [_bdot(jnp.concatenate([r_t[i], m_k[i], -m_b[i]], axis=1),
               jnp.concatenate([h0[i], v[i], u[i]], axis=0)) for i in heads]
    for i in heads:
        decay_tot = jnp.exp(jnp.sum(jnp.where(eye_n, head(tot_all, i), 0.0), axis=1, keepdims=True))
        h_ref[i] = decay_tot * h0[i] + _bdot_tn(
            jnp.concatenate([head(kt_all, i), -(b[i] * head(e_tail, i))], axis=0),
            jnp.concatenate([v[i], u[i]], axis=0))
    outs = []
    for i in heads:
        mu = jnp.mean(y[i], axis=-1, keepdims=True)
        var = jnp.mean(jnp.square(y[i] - mu), axis=-1, keepdims=True)
        yn = (y[i] - mu) * lax.rsqrt(var + RW_GN_EPS)
        outs.append(yn * head(lng_ref[...], i) + head(lnb_ref[...], i)
                    + jnp.sum(head(rkr_all, i), axis=-1, keepdims=True) * v[i])
    o_ref[0] = (jnp.concatenate(outs, axis=-1) * g_ref[0]).astype(o_ref.dtype)


def _rwkv_core(r, k, v, lw, a, g, k_k, k_a, r_k, ln_g, ln_b):
    B, T, D = r.shape
    C, N, HB = RW_CHUNK, RW_HEAD_DIM, RW_HEADS_PER_STEP
    W = HB * N
    seq = pl.BlockSpec((1, C, W), lambda b, h, c: (b, c, h))
    par = pl.BlockSpec((1, W), lambda b, h, c: (0, h))
    return pl.pallas_call(
        functools.partial(_rwkv_kernel, C=C, N=N, HB=HB),
        grid=(B, D // W, T // C),
        in_specs=[seq] * 6 + [par] * 5,
        out_specs=seq,
        out_shape=jax.ShapeDtypeStruct((B, T, D), BF16),
        scratch_shapes=[pltpu.VMEM((HB, N, N), F32)],
        compiler_params=_params("parallel", "parallel", "arbitrary"),
        name="rwkv_core",
    )(r, k, v, lw, a, g, k_k.reshape(1, D), k_a.reshape(1, D), r_k.reshape(1, D),
      ln_g.reshape(1, D), ln_b.reshape(1, D))


def _rwkv7_mix(x_res, norm_g, sc, shift, gate, mu, w_rkv, w0, w1, w2, a0, a1, a2, g1, g2, k_k, k_a,
               r_k, ln_g, ln_b, w_o):
    B, T, D = x_res.shape
    M = B * T
    xr, xk, xv, xw, xa, xg = [z.reshape(M, D) for z in _rwkv_prep(x_res, norm_g, sc, shift, mu)]
    r = _matmul(xr, w_rkv, (0, 0), name="rwkv_r")
    k = _matmul(xk, w_rkv, (0, 1), name="rwkv_k")
    v = _matmul(xv, w_rkv, (0, 2), name="rwkv_v")
    log_decay = _matmul(_matmul(xw, w1, (0,), post="tanh", out_dtype=BF16), w2, (0,),
                        bias=w0.reshape(1, D), post="rwkv_log_decay", name="rwkv_decay")
    a = _matmul(_matmul(xa, a1, (0,), out_dtype=BF16), a2, (0,), bias=a0.reshape(1, D),
                post="sigmoid", name="rwkv_a")
    g = _matmul(_matmul(xg, g1, (0,), post="sigmoid", out_dtype=BF16), g2, (0,), name="rwkv_g")
    sh = (B, T, D)
    o = _rwkv_core(r.reshape(sh), k.reshape(sh), v.reshape(sh), log_decay.reshape(sh),
                   a.reshape(sh), g.reshape(sh), k_k, k_a, r_k, ln_g, ln_b)
    return _matmul(o.reshape(M, D), w_o, (0,), res=x_res.reshape(M, D), gate=gate).reshape(sh)


def _ret_kernel(q_ref, k_ref, v_ref, g_ref, cos_ref, sin_ref, dm_ref, xz_ref, gng_ref, gnb_ref,
                o_ref, s_ref, *, dk):
    @pl.when(pl.program_id(2) == 0)
    def _():
        s_ref[...] = jnp.zeros_like(s_ref)

    cos, sin = cos_ref[...], sin_ref[...]
    half = dk // 2

    def rot(z):
        z1, z2 = z[:, :half], z[:, half:]
        return jnp.concatenate([z1 * cos - z2 * sin, z1 * sin + z2 * cos], axis=-1)

    q = rot(q_ref[0])
    k = rot(k_ref[0] * (dk ** -0.5))
    v = v_ref[0]
    xi, zeta, g_chunk = xz_ref[0, :, 0:1], xz_ref[0, :, 1:2], xz_ref[0, 0:1, 2:3]
    scores = _bdot_nt(q, k) * dm_ref[0]
    state = s_ref[...]
    o = _bdot(scores, v) + _bdot(q * xi, state)
    s_ref[...] = g_chunk * state + _bdot_tn(k * zeta, v)
    mu = jnp.mean(o, axis=-1, keepdims=True)
    var = jnp.mean(jnp.square(o - mu), axis=-1, keepdims=True)
    y = (o - mu) * lax.rsqrt(var + NORM_EPS) * gng_ref[0] + gnb_ref[0]
    gate = g_ref[0]
    o_ref[0] = (y * (gate * jax.nn.sigmoid(gate))).astype(o_ref.dtype)


def _retention_mix(h, x_res, gate, w_in, gn_g, gn_b, w_o):
    B, T, D = h.shape
    M = B * T
    H, dk, dv, C = RET_HEADS, RET_DK, RET_DV, RET_CHUNK
    proj = _matmul(h.reshape(M, D), w_in, (0,)).reshape(B, T, H * (2 * dk + 2 * dv))
    qb, kb, vb, gb = 0, (H * dk) // dk, (2 * H * dk) // dv, (2 * H * dk + H * dv) // dv
    theta = 1.0 / (RET_ROT_BASE ** jnp.linspace(0.0, 1.0, dk // 2, dtype=F32))
    ang = jnp.arange(T, dtype=F32)[:, None] * theta[None, :]
    log_gamma = jnp.log(1.0 - 2.0 ** (-5.0 - jnp.arange(H, dtype=F32)))
    pos = jnp.arange(C, dtype=F32)
    rel = pos[:, None] - pos[None, :]
    dmask = jnp.where(rel >= 0, jnp.exp(jnp.maximum(rel, 0.0) * log_gamma[:, None, None]), 0.0)
    xi = jnp.exp((pos + 1.0)[None, :] * log_gamma[:, None])
    zeta = jnp.exp((C - 1.0 - pos)[None, :] * log_gamma[:, None])
    g_chunk = jnp.broadcast_to(jnp.exp(C * log_gamma)[:, None], (H, C))
    xz = jnp.concatenate([jnp.stack([xi, zeta, g_chunk], axis=-1), jnp.zeros((H, C, 125), F32)], axis=-1)
    o = pl.pallas_call(
        functools.partial(_ret_kernel, dk=dk),
        grid=(B, H, T // C),
        in_specs=[
            pl.BlockSpec((1, C, dk), lambda b, h, c: (b, c, qb + h)),
            pl.BlockSpec((1, C, dk), lambda b, h, c: (b, c, kb + h)),
            pl.BlockSpec((1, C, dv), lambda b, h, c: (b, c, vb + h)),
            pl.BlockSpec((1, C, dv), lambda b, h, c: (b, c, gb + h)),
            pl.BlockSpec((C, dk // 2), lambda b, h, c: (c, 0)),
            pl.BlockSpec((C, dk // 2), lambda b, h, c: (c, 0)),
            pl.BlockSpec((1, C, C), lambda b, h, c: (h, 0, 0)),
            pl.BlockSpec((1, C, 128), lambda b, h, c: (h, 0, 0)),
            pl.BlockSpec((1, 1, dv), lambda b, h, c: (h, 0, 0)),
            pl.BlockSpec((1, 1, dv), lambda b, h, c: (h, 0, 0)),
        ],
        out_specs=pl.BlockSpec((1, C, dv), lambda b, h, c: (b, c, h)),
        out_shape=jax.ShapeDtypeStruct((B, T, H * dv), BF16),
        scratch_shapes=[pltpu.VMEM((dk, dv), F32)],
        compiler_params=_params("parallel", "parallel", "arbitrary"),
        name="retention_core",
    )(proj, proj, proj, proj, jnp.cos(ang), jnp.sin(ang), dmask, xz,
      gn_g.reshape(H, 1, dv), gn_b.reshape(H, 1, dv))
    return _matmul(o.reshape(M, H * dv), w_o, (0,), res=x_res.reshape(M, D), gate=gate).reshape(B, T, D)


def _gla_kernel(q_ref, k_ref, v_ref, g_ref, la_ref, gn_ref, o_ref, s_ref, *, C, dk):
    @pl.when(pl.program_id(2) == 0)
    def _():
        s_ref[...] = jnp.zeros_like(s_ref)

    row = lax.broadcasted_iota(jnp.int32, (C, C), 0)
    col = lax.broadcasted_iota(jnp.int32, (C, C), 1)
    causal = row >= col
    la = la_ref[0]
    b = _exact_lhs_dot(causal.astype(F32), la)
    b_last = b[C - 1:C, :]
    eye_k = (lax.broadcasted_iota(jnp.int32, (dk, dk), 0) == lax.broadcasted_iota(jnp.int32, (dk, dk), 1))
    d_last_col = jnp.exp(jnp.sum(jnp.where(eye_k, b_last, 0.0), axis=1, keepdims=True))
    k, v = k_ref[0], v_ref[0]
    q_in = q_ref[0] * (dk ** -0.5) * jnp.exp(b)
    k_in = k * jnp.exp(-b)
    att = jnp.where(causal, _bdot_nt(q_in, k_in), 0.0)
    state = s_ref[...]
    o = _bdot(att, v) + _bdot(q_in, state)
    s_ref[...] = d_last_col * state + _bdot_tn(k * jnp.exp(b_last - b), v)
    y = o * lax.rsqrt(jnp.mean(o * o, axis=-1, keepdims=True) + NORM_EPS) * gn_ref[...]
    gate = g_ref[0]
    o_ref[0] = (y * (gate * jax.nn.sigmoid(gate))).astype(o_ref.dtype)


def _gla_mix(h, x_res, gate, w_in, w_a1, w_a2, b_a, gn_g, w_o):
    B, T, D = h.shape
    M = B * T
    H, dk, dv, C = GLA_HEADS, GLA_DK, GLA_DV, GLA_CHUNK
    h2 = h.reshape(M, D)
    proj = _matmul(h2, w_in, (0,)).reshape(B, T, H * (2 * dk + 2 * dv))
    log_a = _matmul(_matmul(h2, w_a1, (0,), out_dtype=BF16), w_a2, (0,), bias=b_a.reshape(1, H * dk),
                    post="gla_log_gate", name="gla_log_gate").reshape(B, T, H * dk)
    qb, kb, vb, gb = 0, H, (2 * H * dk) // dv, (2 * H * dk + H * dv) // dv
    o = pl.pallas_call(
        functools.partial(_gla_kernel, C=C, dk=dk),
        grid=(B, H, T // C),
        in_specs=[
            pl.BlockSpec((1, C, dk), lambda b, h, c: (b, c, qb + h)),
            pl.BlockSpec((1, C, dk), lambda b, h, c: (b, c, kb + h)),
            pl.BlockSpec((1, C, dv), lambda b, h, c: (b, c, vb + h)),
            pl.BlockSpec((1, C, dv), lambda b, h, c: (b, c, gb + h)),
            pl.BlockSpec((1, C, dk), lambda b, h, c: (b, c, h)),
            pl.BlockSpec((1, dv), lambda b, h, c: (0, 0)),
        ],
        out_specs=pl.BlockSpec((1, C, dv), lambda b, h, c: (b, c, h)),
        out_shape=jax.ShapeDtypeStruct((B, T, H * dv), BF16),
        scratch_shapes=[pltpu.VMEM((dk, dv), F32)],
        compiler_params=_params("parallel", "parallel", "arbitrary"),
        name="gla_core",
    )(proj, proj, proj, proj, log_a, gn_g.reshape(1, dv))
    return _matmul(o.reshape(M, H * dv), w_o, (0,), res=x_res.reshape(M, D), gate=gate).reshape(B, T, D)


def _nsa_cmp_kernel(q_ref, kc_ref, vc_ref, o_ref, sel_ref, *, tq, n_cmp, n_pad, n_slc, hpg):
    L, S, Ls, hd = NSA_CMP_BLK, NSA_CMP_STRIDE, NSA_SEL_BLK, NSA_HD
    t_pos = pl.program_id(2) * tq + lax.broadcasted_iota(jnp.int32, (tq, 1), 0)
    n_ix = lax.broadcasted_iota(jnp.int32, (1, n_pad), 1)
    valid = (n_ix * S + (L - 1) <= t_pos) & (n_ix < n_cmp)
    validf = valid.astype(F32)
    c_start = lax.broadcasted_iota(jnp.int32, (n_pad, n_slc), 0) * S
    s_start = lax.broadcasted_iota(jnp.int32, (n_pad, n_slc), 1) * Ls
    overlap = ((c_start <= s_start + (Ls - 1)) & (c_start + (L - 1) >= s_start)
               & (c_start < n_cmp * S)).astype(F32)
    kc, vc = kc_ref[0, 0], vc_ref[0, 0]
    imp = jnp.zeros((tq, n_slc), F32)
    outs = []
    for hh in range(hpg):
        s = _fdot_nt(q_ref[0, :, hh * hd:(hh + 1) * hd], kc)
        s = jnp.where(valid, s, NEG_INF)
        e = jnp.exp(s - jnp.max(s, axis=-1, keepdims=True))
        p = e / jnp.sum(e, axis=-1, keepdims=True) * validf
        outs.append(_bdot(p, vc))
        imp = imp + _exact_rhs_dot(p, overlap)
    o_ref[0] = jnp.concatenate(outs, axis=-1)

    j = lax.broadcasted_iota(jnp.int32, (1, n_slc), 1)
    cur = t_pos // Ls
    forced = (j == 0) | (j == cur) | (j == cur - 1)
    imp = jnp.where(j > cur, -IMP_BIG, jnp.where(forced, IMP_BIG, imp))
    rank = jnp.zeros((tq, n_slc), jnp.int32)
    for jp in range(n_slc):
        c = imp[:, jp:jp + 1]
        rank = rank + ((c > imp) | ((c == imp) & (jp < j))).astype(jnp.int32)
    sel_ref[0, 0] = (rank < min(NSA_SEL_TOPK, n_slc)).astype(F32)


def _nsa_att_kernel(q_ref, ks_ref, vs_ref, kw_ref, vw_ref, sel_ref, oc_ref, gt_ref, o_ref,
                    m_ref, l_ref, acc_ref, *, tq, tk, T, n_slc, hpg):
    Ls, W, hd = NSA_SEL_BLK, NSA_WINDOW, NSA_HD
    qi = pl.program_id(2)
    t0 = qi * tq
    q = jnp.concatenate([q_ref[0, :, hh * hd:(hh + 1) * hd] for hh in range(hpg)], axis=0)
    t_q = t0 + lax.broadcasted_iota(jnp.int32, (tq, 1), 0)
    sel = sel_ref[0, 0]

    m_ref[...] = jnp.full_like(m_ref, NEG_INF)
    l_ref[...] = jnp.zeros_like(l_ref)
    acc_ref[...] = jnp.zeros_like(acc_ref)

    def body(kb, carry):
        k0 = pl.multiple_of(kb * tk, tk)
        kt = ks_ref[0, pl.ds(k0, tk), :]
        vt = vs_ref[0, pl.ds(k0, tk), :]
        blk_of_key = (k0 + lax.broadcasted_iota(jnp.int32, (n_slc, tk), 1)) // Ls
        expand = (blk_of_key == lax.broadcasted_iota(jnp.int32, (n_slc, tk), 0)).astype(F32)
        kpos = k0 + lax.broadcasted_iota(jnp.int32, (1, tk), 1)
        bias = jnp.where((_bdot(sel, expand) > 0.5) & (kpos <= t_q), 0.0, NEG_INF)
        s_all = _bdot_nt(q, kt)
        ps = []
        for hh in range(hpg):
            rows = slice(hh * tq, (hh + 1) * tq)
            s = s_all[rows] + bias
            m_old = m_ref[rows]
            m_new = jnp.maximum(m_old, jnp.max(s, axis=-1, keepdims=True))
            alpha = jnp.exp(m_old - m_new)
            p = jnp.exp(s - m_new)
            l_ref[rows] = alpha * l_ref[rows] + jnp.sum(p, axis=-1, keepdims=True)
            acc_ref[rows] = alpha * acc_ref[rows]
            m_ref[rows] = m_new
            ps.append(p.astype(BF16))
        acc_ref[...] += jnp.dot(jnp.concatenate(ps, axis=0), vt, preferred_element_type=F32)
        return carry

    lax.fori_loop(0, (t0 + tq + tk - 1) // tk, body, 0)
    o_sel = acc_ref[...] / l_ref[...]

    span = W + tq
    w0 = pl.multiple_of(jnp.maximum(t0 - W, 0), tq)
    kw = kw_ref[0, pl.ds(w0, span), :]
    vw = vw_ref[0, pl.ds(w0, span), :]
    wpos = w0 + lax.broadcasted_iota(jnp.int32, (1, span), 1)
    bias_w = jnp.where((wpos <= t_q) & (wpos > t_q - W), 0.0, NEG_INF)
    s_all = _bdot_nt(q, kw)
    es, ls = [], []
    for hh in range(hpg):
        s = s_all[hh * tq:(hh + 1) * tq] + bias_w
        e = jnp.exp(s - jnp.max(s, axis=-1, keepdims=True))
        ls.append(jnp.sum(e, axis=-1, keepdims=True))
        es.append(e.astype(BF16))
    o_win = (jnp.dot(jnp.concatenate(es, axis=0), vw, preferred_element_type=F32)
             / jnp.concatenate(ls, axis=0))

    gt = jax.nn.sigmoid(gt_ref[0, 0])
    outs = []
    for hh in range(hpg):
        rows = slice(hh * tq, (hh + 1) * tq)
        outs.append(gt[:, hh:hh + 1] * oc_ref[0, :, hh * hd:(hh + 1) * hd]
                    + gt[:, hpg + hh:hpg + hh + 1] * o_sel[rows]
                    + gt[:, 2 * hpg + hh:2 * hpg + hh + 1] * o_win[rows])
    o_ref[0] = jnp.concatenate(outs, axis=-1).astype(o_ref.dtype)


def _rms_norm(x, g):
    return x * lax.rsqrt(jnp.mean(x * x, axis=-1, keepdims=True) + NORM_EPS) * g


def _nsa_mix(h, x_res, gate, w_in, q_g, k_g, cmp_pos, cmp_w1, cmp_w2, w_o):
    B, T, D = h.shape
    M = B * T
    H, G, hd = NSA_HEADS, NSA_KV_HEADS, NSA_HD
    hpg = H // G
    L, S, Ls = NSA_CMP_BLK, NSA_CMP_STRIDE, NSA_SEL_BLK
    kvw = G * hd
    n_main = H * hd + 6 * kvw
    h2 = h.reshape(M, D)
    proj = _matmul(h2, w_in, (0,), n_cols=n_main)
    gate_w = jnp.pad(w_in[0, :, n_main:], ((0, 0), (0, 128 - 3 * H)))
    gates = _matmul(h2, gate_w)[:, :3 * H]
    q = _rms_norm(proj[:, :H * hd].reshape(B, T, H, hd), q_g) * (hd ** -0.5)
    q = q.reshape(B, T, H * hd)

    def kv(i):
        return proj[:, H * hd + i * kvw:H * hd + (i + 1) * kvw].reshape(B, T, G, hd)

    n_cmp = (T - L) // S + 1
    n_grp = T // S
    assert L == 2 * S and n_cmp == n_grp - 1

    def compress(z, pos, w1, w2):
        zg = z.reshape(B, n_grp, S, G, hd).transpose(0, 3, 1, 2, 4).reshape(B * G * n_grp, S * hd)
        w_halves = jnp.concatenate([w1[:S * hd], w1[S * hd:]], axis=1)
        part = _matmul(zg, w_halves, tm=min(512, B * G * n_grp)).reshape(B, G, n_grp, 2 * hd)
        pos_term = _matmul(jnp.broadcast_to(pos.reshape(1, L * hd), (8, L * hd)), w1, tm=8)[0]
        pre = part[:, :, :-1, :hd] + part[:, :, 1:, hd:] + pos_term
        pre = jnp.pad(pre, ((0, 0), (0, 0), (0, 1), (0, 0))).reshape(B * G * n_grp, hd)
        return _matmul(jax.nn.silu(pre), w2, tm=min(512, B * G * n_grp)).reshape(B, G, n_grp, hd)

    kc = _rms_norm(compress(kv(0), cmp_pos[0], cmp_w1[0], cmp_w2[0]), k_g[0])
    vc = compress(kv(1), cmp_pos[1], cmp_w1[1], cmp_w2[1])
    k_sel = _rms_norm(kv(2), k_g[1]).reshape(B, T, kvw).astype(BF16)
    v_sel = kv(3).reshape(B, T, kvw).astype(BF16)
    k_win = _rms_norm(kv(4), k_g[2]).reshape(B, T, kvw).astype(BF16)
    v_win = kv(5).reshape(B, T, kvw).astype(BF16)

    n_slc = T // Ls
    tq = min(NSA_CMP_TQ, T)
    o_cmp, sel = pl.pallas_call(
        functools.partial(_nsa_cmp_kernel, tq=tq, n_cmp=n_cmp, n_pad=n_grp, n_slc=n_slc, hpg=hpg),
        grid=(B, G, T // tq),
        in_specs=[
            pl.BlockSpec((1, tq, hpg * hd), lambda b, g, i: (b, i, g)),
            pl.BlockSpec((1, 1, n_grp, hd), lambda b, g, i: (b, g, 0, 0)),
            pl.BlockSpec((1, 1, n_grp, hd), lambda b, g, i: (b, g, 0, 0)),
        ],
        out_specs=[
            pl.BlockSpec((1, tq, hpg * hd), lambda b, g, i: (b, i, g)),
            pl.BlockSpec((1, 1, tq, n_slc), lambda b, g, i: (b, g, i, 0)),
        ],
        out_shape=[jax.ShapeDtypeStruct((B, T, H * hd), F32),
                   jax.ShapeDtypeStruct((B, G, T, n_slc), F32)],
        compiler_params=_params("parallel", "parallel", "parallel"),
        name="nsa_compressed",
    )(q, kc, vc)

    gt = gates.reshape(B, T, 3, G, hpg).transpose(0, 3, 1, 2, 4).reshape(B, G, T, 3 * hpg)
    tq = min(NSA_ATT_TQ, T)
    tk = min(NSA_ATT_TK, T)
    assert T >= NSA_WINDOW + tq
    kv_spec = pl.BlockSpec((1, T, hd), lambda b, g, i: (b, 0, g))
    o = pl.pallas_call(
        functools.partial(_nsa_att_kernel, tq=tq, tk=tk, T=T, n_slc=n_slc, hpg=hpg),
        grid=(B, G, T // tq),
        in_specs=[
            pl.BlockSpec((1, tq, hpg * hd), lambda b, g, i: (b, i, g)),
            kv_spec, kv_spec, kv_spec, kv_spec,
            pl.BlockSpec((1, 1, tq, n_slc), lambda b, g, i: (b, g, i, 0)),
            pl.BlockSpec((1, tq, hpg * hd), lambda b, g, i: (b, i, g)),
            pl.BlockSpec((1, 1, tq, 3 * hpg), lambda b, g, i: (b, g, i, 0)),
        ],
        out_specs=pl.BlockSpec((1, tq, hpg * hd), lambda b, g, i: (b, i, g)),
        out_shape=jax.ShapeDtypeStruct((B, T, H * hd), BF16),
        scratch_shapes=[pltpu.VMEM((hpg * tq, 1), F32), pltpu.VMEM((hpg * tq, 1), F32),
                        pltpu.VMEM((hpg * tq, hd), F32)],
        compiler_params=_params("parallel", "parallel", "arbitrary"),
        name="nsa_selected_window",
    )(q.astype(BF16), k_sel, v_sel, k_win, v_win, sel, o_cmp, gt)
    return _matmul(o.reshape(M, H * hd), w_o, (0,), res=x_res.reshape(M, D), gate=gate).reshape(B, T, D)


def _moe_up_kernel(blk_e_ref, x_ref, w_ref, b_ref, o_ref):
    gu = _bdot(x_ref[...], w_ref[...]) + b_ref[0]
    x_glu = jnp.minimum(gu[:, :D_EXPERT], SWIGLU_LIMIT)
    x_lin = jnp.clip(gu[:, D_EXPERT:], -SWIGLU_LIMIT, SWIGLU_LIMIT)
    o_ref[...] = (x_glu * jax.nn.sigmoid(SWIGLU_ALPHA * x_glu) * (x_lin + 1.0)).astype(o_ref.dtype)


def _moe_down_kernel(blk_e_ref, a_ref, w_ref, b_ref, o_ref):
    o_ref[...] = _bdot(a_ref[...], w_ref[...]) + b_ref[0]


def _router_kernel(x_ref, w_ref, b_ref, et_ref, p_ref):
    logits = _fdot(x_ref[...], w_ref[...]) + b_ref[...]
    lane = lax.broadcasted_iota(jnp.int32, logits.shape, 1)
    vals, idxs = [], []
    for _ in range(TOP_K):
        m = jnp.max(logits, axis=-1, keepdims=True)
        idx = jnp.min(jnp.where(logits == m, lane, logits.shape[1]), axis=-1, keepdims=True)
        vals.append(m)
        idxs.append(idx)
        logits = jnp.where(lane == idx, -IMP_BIG, logits)
    es = [jnp.exp(v - vals[0]) for v in vals]
    total = sum(es)
    e_out = jnp.zeros(logits.shape, jnp.int32)
    p_out = jnp.zeros(logits.shape, F32)
    for k in range(TOP_K):
        e_out = jnp.where(lane == k, idxs[k], e_out)
        p_out = jnp.where(lane == k, es[k] / total, p_out)
    et_ref[...] = e_out.T[:ROUTER_ROWS]
    p_ref[...] = p_out


def _route(x, w_router, b_router, layer):
    N, D = x.shape
    E, K = N_EXPERTS, TOP_K
    lanes = 128
    w_r = jnp.pad(w_router[layer], ((0, 0), (0, lanes - E)))
    b_r = jnp.concatenate([b_router[layer], jnp.full((lanes - E,), NEG_INF, F32)]).reshape(1, lanes)
    tm = min(N, 512)
    return pl.pallas_call(
        _router_kernel,
        grid=(N // tm,),
        in_specs=[pl.BlockSpec((tm, D), lambda i: (i, 0)),
                  pl.BlockSpec((D, lanes), lambda i: (0, 0)),
                  pl.BlockSpec((1, lanes), lambda i: (0, 0))],
        out_specs=[pl.BlockSpec((ROUTER_ROWS, tm), lambda i: (0, i)),
                   pl.BlockSpec((tm, lanes), lambda i: (i, 0))],
        out_shape=[jax.ShapeDtypeStruct((ROUTER_ROWS, N), jnp.int32),
                   jax.ShapeDtypeStruct((N, lanes), F32)],
        compiler_params=_params("parallel"),
        name="moe_router",
    )(x, w_r, b_r)


def _combine_kernel(slot_ref, out_hbm, p_ref, res_ref, gate_ref, o_ref, buf, sem, *, TT, K, N):
    i = pl.program_id(0)
    n_steps = pl.num_programs(0)

    def fetch(step, par):
        for k in range(K):
            def one_row(r, carry):
                row = slot_ref[k * N + step * TT + r]
                pltpu.make_async_copy(out_hbm.at[pl.ds(row, 1)], buf.at[par, pl.ds(k * TT + r, 1)],
                                      sem.at[par]).start()
                return carry
            lax.fori_loop(0, TT, one_row, 0, unroll=8)

    @pl.when(i == 0)
    def _():
        fetch(0, 0)

    @pl.when(i + 1 < n_steps)
    def _():
        fetch(i + 1, (i + 1) % 2)

    par = i % 2
    pltpu.make_async_copy(out_hbm.at[pl.ds(0, K * TT)], buf.at[par], sem.at[par]).wait()
    p = p_ref[...]
    y = p[:, 0:1] * buf[par, 0:TT, :]
    for k in range(1, K):
        y = y + p[:, k:k + 1] * buf[par, k * TT:(k + 1) * TT, :]
    o_ref[...] = res_ref[...] + gate_ref[0] * y


def _moe_combine(out, slot, top_w, x_res, gate, T):
    N, D = x_res.shape
    K, TT = TOP_K, MOE_COMBINE_TOKENS
    return pl.pallas_call(
        functools.partial(_combine_kernel, TT=TT, K=K, N=N),
        grid_spec=pltpu.PrefetchScalarGridSpec(
            num_scalar_prefetch=1,
            grid=(N // TT,),
            in_specs=[
                pl.BlockSpec(memory_space=pl.ANY),
                pl.BlockSpec((TT, top_w.shape[1]), lambda i, s: (i, 0)),
                pl.BlockSpec((TT, D), lambda i, s: (i, 0)),
                pl.BlockSpec((1, 1, D), lambda i, s: ((i * TT) // T, 0, 0)),
            ],
            out_specs=pl.BlockSpec((TT, D), lambda i, s: (i, 0)),
            scratch_shapes=[pltpu.VMEM((2, K * TT, D), F32), pltpu.SemaphoreType.DMA((2,))],
        ),
        out_shape=jax.ShapeDtypeStruct((N, D), F32),
        compiler_params=_params("arbitrary"),
        name="moe_combine",
    )(slot, out, top_w, x_res, gate)


def _moe_ffn(h, h_bf16, layer, x_res, gate, w_router, b_router, w_gu, b_gu, w_down, b_down):
    B, T, D = h.shape
    N = B * T
    E, K, R = N_EXPERTS, TOP_K, MOE_ROWS
    NK = N * K
    x = h.reshape(N, D)
    top_e_t, top_w = _route(x, w_router, b_router, layer)
    pair = jnp.arange(NK, dtype=jnp.int32)
    e_s, order = lax.sort((top_e_t[:K].reshape(-1), pair), num_keys=1, is_stable=True)
    one_hot = jnp.arange(E, dtype=jnp.int32)[:, None] == e_s[None, :]
    counts = jnp.sum(one_hot, axis=1, dtype=jnp.int32)
    padded = (counts + R - 1) // R * R
    g_start = jnp.cumsum(counts) - counts
    p_end = jnp.cumsum(padded)
    shift = (p_end - padded) - g_start
    dest = pair + jnp.sum(jnp.where(one_hot, shift[:, None], 0), axis=0)
    n_blocks = (NK + R - 1) // R + E
    cap = n_blocks * R
    blk_start = jnp.arange(n_blocks, dtype=jnp.int32) * R
    blk_e = jnp.minimum(jnp.sum(p_end[None, :] <= blk_start[:, None], axis=1), E - 1).astype(jnp.int32)
    blk_shift = jnp.sum(jnp.where(blk_e[:, None] == jnp.arange(E)[None, :], shift[None, :], 0), axis=1)
    src = jnp.clip(jnp.arange(cap, dtype=jnp.int32) - jnp.repeat(blk_shift, R), 0, NK - 1)
    tok = jnp.take(order, src, axis=0) % N
    xb = jnp.take(h_bf16.reshape(N, D), tok, axis=0)

    act = pl.pallas_call(
        _moe_up_kernel,
        grid_spec=pltpu.PrefetchScalarGridSpec(
            num_scalar_prefetch=1,
            grid=(n_blocks,),
            in_specs=[
                pl.BlockSpec((R, D), lambda i, be: (i, 0)),
                pl.BlockSpec((None, None, D, 2 * D_EXPERT), lambda i, be: (layer, be[i], 0, 0)),
                pl.BlockSpec((None, 1, 1, 2 * D_EXPERT), lambda i, be: (layer, be[i], 0, 0)),
            ],
            out_specs=pl.BlockSpec((R, D_EXPERT), lambda i, be: (i, 0)),
        ),
        out_shape=jax.ShapeDtypeStruct((cap, D_EXPERT), BF16),
        compiler_params=_params("arbitrary"),
        name="moe_up",
    )(blk_e, xb, w_gu, b_gu.reshape(b_gu.shape[0], E, 1, 2 * D_EXPERT))

    out = pl.pallas_call(
        _moe_down_kernel,
        grid_spec=pltpu.PrefetchScalarGridSpec(
            num_scalar_prefetch=1,
            grid=(n_blocks,),
            in_specs=[
                pl.BlockSpec((R, D_EXPERT), lambda i, be: (i, 0)),
                pl.BlockSpec((None, None, D_EXPERT, D), lambda i, be: (layer, be[i], 0, 0)),
                pl.BlockSpec((None, 1, 1, D), lambda i, be: (layer, be[i], 0, 0)),
            ],
            out_specs=pl.BlockSpec((R, D), lambda i, be: (i, 0)),
        ),
        out_shape=jax.ShapeDtypeStruct((cap, D), F32),
        compiler_params=_params("arbitrary"),
        name="moe_down",
    )(blk_e, act, w_down, b_down.reshape(b_down.shape[0], E, 1, D))

    _, slot = lax.sort((order, dest), num_keys=1)
    return _moe_combine(out, slot, top_w, x_res.reshape(N, D), gate, T).reshape(B, T, D)


def kernel(x, c, ada_w, ada_b, norm_g, rw_mu, rw_w_rkv, rw_w0, rw_w1, rw_w2, rw_a0, rw_a1, rw_a2, rw_g1, rw_g2, rw_k_k, rw_k_a, rw_r_k, rw_ln_g, rw_ln_b, rw_w_o, ret_w_in, ret_gn_g, ret_gn_b, ret_w_o, gla_w_in, gla_w_a1, gla_w_a2, gla_b_a, gla_gn_g, gla_w_o, nsa_w_in, nsa_q_g, nsa_k_g, nsa_cmp_pos, nsa_cmp_w1, nsa_cmp_w2, nsa_w_o, moe_router_w, moe_router_b, moe_w_gu, moe_b_gu, moe_w_down, moe_b_down):
    B, T, D = x.shape
    depth = ada_w.shape[0]
    c_act = jnp.pad(jax.nn.silu(c), ((0, 8 - B), (0, 0)))
    for i in range(depth):
        mod = _matmul(c_act, ada_w, (i,), tm=8, tn=1024)[:B] + ada_b[i]
        sh1, sc1, gt1, sh2, sc2, gt2 = jnp.split(mod, 6, axis=-1)
        gt1, gt2 = gt1.reshape(B, 1, D), gt2.reshape(B, 1, D)
        m, j = i % 4, i // 4
        if m == 0:
            x = _rwkv7_mix(x, norm_g[i, 0], sc1, sh1, gt1, rw_mu[j], rw_w_rkv[j:j + 1], rw_w0[j],
                           rw_w1[j:j + 1], rw_w2[j:j + 1], rw_a0[j], rw_a1[j:j + 1], rw_a2[j:j + 1],
                           rw_g1[j:j + 1], rw_g2[j:j + 1], rw_k_k[j], rw_k_a[j], rw_r_k[j], rw_ln_g[j],
                           rw_ln_b[j], rw_w_o[j:j + 1])
        elif m == 1:
            h, = _norm_modulate(x, norm_g[i, 0], sc1, sh1, (BF16,))
            x = _retention_mix(h, x, gt1, ret_w_in[j:j + 1], ret_gn_g[j], ret_gn_b[j], ret_w_o[j:j + 1])
        elif m == 2:
            h, = _norm_modulate(x, norm_g[i, 0], sc1, sh1, (BF16,))
            x = _gla_mix(h, x, gt1, gla_w_in[j:j + 1], gla_w_a1[j:j + 1], gla_w_a2[j:j + 1], gla_b_a[j],
                         gla_gn_g[j], gla_w_o[j:j + 1])
        else:
            h, = _norm_modulate(x, norm_g[i, 0], sc1, sh1, (BF16,))
            x = _nsa_mix(h, x, gt1, nsa_w_in[j:j + 1], nsa_q_g[j], nsa_k_g[j], nsa_cmp_pos[j],
                         nsa_cmp_w1[j], nsa_cmp_w2[j], nsa_w_o[j:j + 1])
        h, h_bf16 = _norm_modulate(x, norm_g[i, 1], sc2, sh2, (F32, BF16))
        x = _moe_ffn(h, h_bf16, i, x, gt2, moe_router_w, moe_router_b, moe_w_gu, moe_b_gu, moe_w_down,
                     moe_b_down)
    return x
```

```python
import functools
import math

import jax
import jax.numpy as jnp
from jax import lax
from jax.experimental import pallas as pl
from jax.experimental.pallas import tpu as pltpu

F32 = jnp.float32
BF16 = jnp.bfloat16
HIGHEST = lax.Precision.HIGHEST

NORM_EPS = 1e-6
NEG_INF = -1e30

RW_HEAD_DIM = 64
RW_GN_EPS = 64e-5
RW_CHUNK = 64
RW_HEADS_PER_STEP = 8
RW_CHUNKS_PER_STEP = 1

RET_HEADS = 8
RET_DK = 256
RET_DV = 512
RET_CHUNK = 128
RET_ROT_BASE = 10000.0

GLA_HEADS = 4
GLA_DK = 256
GLA_DV = 512
GLA_GATE_NORM = 16.0
GLA_CHUNK = 64

NSA_HEADS = 16
NSA_KV_HEADS = 4
NSA_HD = 128
NSA_CMP_BLK = 32
NSA_CMP_STRIDE = 16
NSA_SEL_BLK = 64
NSA_SEL_TOPK = 16
NSA_WINDOW = 512
NSA_CMP_TQ = 256
NSA_ATT_TQ = 128
NSA_ATT_TK = 512
IMP_BIG = 3e38

N_EXPERTS = 32
TOP_K = 4
D_EXPERT = 768
SWIGLU_ALPHA = 1.702
SWIGLU_LIMIT = 7.0
MOE_ROWS = 256
MOE_COMBINE_TOKENS = 64
ROUTER_ROWS = 8

VMEM_LIMIT_BYTES = 52 * 1024 * 1024


def _params(*sem):
    return pltpu.CompilerParams(dimension_semantics=sem, vmem_limit_bytes=VMEM_LIMIT_BYTES)


def _bdot(a, b):
    return jnp.dot(a.astype(BF16), b.astype(BF16), preferred_element_type=F32)


def _bdot_nt(a, b):
    return lax.dot_general(a.astype(BF16), b.astype(BF16), (((1,), (1,)), ((), ())),
                           preferred_element_type=F32)


def _bdot_tn(a, b):
    return lax.dot_general(a.astype(BF16), b.astype(BF16), (((0,), (0,)), ((), ())),
                           preferred_element_type=F32)


def _fdot(a, b):
    return jnp.dot(a, b, precision=HIGHEST, preferred_element_type=F32)


def _fdot_nt(a, b):
    return lax.dot_general(a, b, (((1,), (1,)), ((), ())), precision=HIGHEST,
                           preferred_element_type=F32)


def _split3(x):
    hi = x.astype(BF16)
    rem = x - hi.astype(F32)
    mid = rem.astype(BF16)
    return hi, mid, (rem - mid.astype(F32)).astype(BF16)


def _exact_lhs_dot(m, x):
    mb = m.astype(BF16)
    return sum(jnp.dot(mb, part, preferred_element_type=F32) for part in _split3(x))


def _exact_rhs_dot(x, m):
    mb = m.astype(BF16)
    return sum(jnp.dot(part, mb, preferred_element_type=F32) for part in _split3(x))


def _softplus(z):
    return jnp.maximum(z, 0.0) + jnp.log(1.0 + jnp.exp(-jnp.abs(z)))


_POST = {
    None: lambda z: z,
    "tanh": jnp.tanh,
    "sigmoid": jax.nn.sigmoid,
    "silu": jax.nn.silu,
    "rwkv_log_decay": lambda z: -jnp.exp(-_softplus(-z) - 0.5),
    "gla_log_gate": lambda z: -_softplus(-z) / GLA_GATE_NORM,
}


def _mm_kernel(*refs, has_bias, has_res, post):
    x_ref, w_ref = refs[0], refs[1]
    pos = 2
    acc = _bdot(x_ref[...], w_ref[...])
    if has_bias:
        acc = acc + refs[pos][...]
        pos += 1
    acc = _POST[post](acc)
    if has_res:
        acc = refs[pos][...] + refs[pos + 1][0] * acc
        pos += 2
    o_ref = refs[pos]
    o_ref[...] = acc.astype(o_ref.dtype)


def _matmul(x, w, lead=(), *, bias=None, res=None, gate=None, n_cols=None, tm=None, tn=None,
            post=None, out_dtype=F32, name="matmul"):
    M, K = x.shape
    N = n_cols if n_cols is not None else w.shape[-1]
    if tm is None:
        tm = min(M, 1024 if x.dtype == BF16 else 512)
    if tn is None:
        tn = min(N, 512) if N % 128 == 0 else N
    assert M % tm == 0
    nlead = len(lead)
    in_specs = [
        pl.BlockSpec((tm, K), lambda i, j: (i, 0)),
        pl.BlockSpec((None,) * nlead + (K, tn), lambda i, j: tuple(lead) + (0, j)),
    ]
    args = [x, w]
    if bias is not None:
        in_specs.append(pl.BlockSpec((1, tn), lambda i, j: (0, j)))
        args.append(bias)
    if res is not None:
        rows_per_gate = M // gate.shape[0]
        assert rows_per_gate % tm == 0
        in_specs.append(pl.BlockSpec((tm, tn), lambda i, j: (i, j)))
        in_specs.append(pl.BlockSpec((1, 1, tn), lambda i, j: ((i * tm) // rows_per_gate, 0, j)))
        args += [res, gate]
    return pl.pallas_call(
        functools.partial(_mm_kernel, has_bias=bias is not None, has_res=res is not None, post=post),
        grid=(M // tm, pl.cdiv(N, tn)),
        in_specs=in_specs,
        out_specs=pl.BlockSpec((tm, tn), lambda i, j: (i, j)),
        out_shape=jax.ShapeDtypeStruct((M, N), out_dtype),
        compiler_params=_params("parallel", "parallel"),
        name=name,
    )(*args)


def _modulated_norm(x, g, sc, sh):
    y = x * lax.rsqrt(jnp.mean(x * x, axis=-1, keepdims=True) + NORM_EPS) * g
    return y * (1.0 + sc) + sh


def _normmod_kernel(x_ref, g_ref, sc_ref, sh_ref, *o_refs):
    h = _modulated_norm(x_ref[0], g_ref[...], sc_ref[0], sh_ref[0])
    for o_ref in o_refs:
        o_ref[0] = h.astype(o_ref.dtype)


def _norm_modulate(x, g, sc, sh, out_dtypes):
    B, T, D = x.shape
    tr = min(T, 512)
    row = pl.BlockSpec((1, tr, D), lambda b, i: (b, i, 0))
    per_batch = pl.BlockSpec((1, 1, D), lambda b, i: (b, 0, 0))
    return pl.pallas_call(
        _normmod_kernel,
        grid=(B, T // tr),
        in_specs=[row, pl.BlockSpec((1, D), lambda b, i: (0, 0)), per_batch, per_batch],
        out_specs=[row] * len(out_dtypes),
        out_shape=[jax.ShapeDtypeStruct((B, T, D), dt) for dt in out_dtypes],
        compiler_params=_params("parallel", "parallel"),
        name="norm_modulate",
    )(x, g.reshape(1, D), sc.reshape(B, 1, D), sh.reshape(B, 1, D))


def _rwkv_prep_kernel(x_ref, halo_ref, g_ref, sc_ref, sh_ref, mu_ref, *o_refs, halo_rows):
    h = _modulated_norm(x_ref[0], g_ref[...], sc_ref[0], sh_ref[0])
    h_halo = _modulated_norm(halo_ref[0], g_ref[...], sc_ref[0], sh_ref[0])
    first = jnp.where(pl.program_id(1) > 0, h_halo[halo_rows - 1:halo_rows, :], 0.0)
    row = lax.broadcasted_iota(jnp.int32, h.shape, 0)
    d = jnp.where(row == 0, first, pltpu.roll(h, 1, axis=0)) - h
    for j, o_ref in enumerate(o_refs):
        o_ref[0] = (h + d * mu_ref[j:j + 1, :]).astype(o_ref.dtype)


def _rwkv_prep(x, g, sc, sh, mu):
    B, T, D = x.shape
    tr = min(T, 512)
    hr = 8
    n_mix = mu.shape[0]
    row = pl.BlockSpec((1, tr, D), lambda b, i: (b, i, 0))
    per_batch = pl.BlockSpec((1, 1, D), lambda b, i: (b, 0, 0))
    return pl.pallas_call(
        functools.partial(_rwkv_prep_kernel, halo_rows=hr),
        grid=(B, T // tr),
        in_specs=[row,
                  pl.BlockSpec((1, hr, D), lambda b, i: (b, jnp.maximum(i * (tr // hr) - 1, 0), 0)),
                  pl.BlockSpec((1, D), lambda b, i: (0, 0)), per_batch, per_batch,
                  pl.BlockSpec((n_mix, D), lambda b, i: (0, 0))],
        out_specs=[row] * n_mix,
        out_shape=[jax.ShapeDtypeStruct((B, T, D), BF16)] * n_mix,
        compiler_params=_params("parallel", "parallel"),
        name="rwkv_shift_mix",
    )(x, x, g.reshape(1, D), sc.reshape(B, 1, D), sh.reshape(B, 1, D), mu)


def _rwkv_kernel(*refs, C, N, HB, SUB):
    h_ref = refs[-1]

    @pl.when(pl.program_id(2) == 0)
    def _():
        h_ref[...] = jnp.zeros_like(h_ref)

    for s in range(SUB):
        _rwkv_chunk(slice(s * C, (s + 1) * C), *refs, C=C, N=N, HB=HB)


def _rwkv_chunk(rows, r_ref, k_ref, v_ref, lw_ref, a_ref, g_ref, kk_ref, ka_ref, rk_ref, lng_ref,
                lnb_ref, o_ref, h_ref, *, C, N, HB):
    row = lax.broadcasted_iota(jnp.int32, (C, C), 0)
    col = lax.broadcasted_iota(jnp.int32, (C, C), 1)
    incl = row >= col
    strict = row > col
    eye = (row == col).astype(F32)
    eye_n = (lax.broadcasted_iota(jnp.int32, (N, N), 0) == lax.broadcasted_iota(jnp.int32, (N, N), 1))
    n_double = int(math.log2(C)) - 1
    heads = range(HB)

    def head(x, i):
        return x[:, i * N:(i + 1) * N]

    r_all, k_all, v_all, a_all, lw_all = (ref[0, rows, :] for ref in (r_ref, k_ref, v_ref, a_ref, lw_ref))
    cum_all = _exact_lhs_dot(incl.astype(F32), lw_all)
    tot_all = cum_all[C - 1:C, :]
    e_neg = jnp.exp(-cum_all)
    e_tail = jnp.exp(tot_all - cum_all)
    kx_all = k_all * kk_ref[...]
    kp_all = k_all * (1.0 + (a_all - 1.0) * ka_ref[...])
    rt_all = r_all * jnp.exp(cum_all)
    kn_all = kp_all * e_neg
    kt_all = kp_all * e_tail
    ep_all = jnp.exp(cum_all - lw_all)
    rkr_all = r_all * kp_all * rk_ref[...]

    kappa = [head(kx_all, i) / jnp.maximum(
        jnp.sqrt(jnp.sum(jnp.square(head(kx_all, i)), axis=-1, keepdims=True)), 1e-12) for i in heads]
    b = [kappa[i] * head(a_all, i) for i in heads]
    kap_t = [kappa[i] * head(ep_all, i) for i in heads]
    r_t = [head(rt_all, i) for i in heads]
    v = [head(v_all, i) for i in heads]
    big = [_bdot_nt(jnp.concatenate([kap_t[i], r_t[i]], axis=0),
                    jnp.concatenate([b[i] * head(e_neg, i), head(kn_all, i)], axis=0))
           for i in heads]
    t_k = [jnp.where(strict, big[i][:C, C:], 0.0) for i in heads]
    m_b = [jnp.where(incl, big[i][C:, :C], 0.0) for i in heads]
    m_k = [jnp.where(incl, big[i][C:, C:], 0.0) for i in heads]
    p = [jnp.where(strict, -big[i][:C, :C], 0.0) for i in heads]
    inv = [eye + p[i] for i in heads]
    tkv = [_bdot(t_k[i], v[i]) for i in heads]
    for _ in range(n_double):
        p = [_bdot(p[i], p[i]) for i in heads]
        inv = [inv[i] + _bdot(inv[i], p[i]) for i in heads]
    h0 = [h_ref[i] for i in heads]
    aw = [_bdot(inv[i], jnp.concatenate([kap_t[i], tkv[i]], axis=1)) for i in heads]
    u = [_bdot(aw[i][:, :N], h0[i]) + aw[i][:, N:] for i in heads]
    y = [_bdot(jnp.concatenate([r_t[i], m_k[i], -m_b[i]], axis=1),
               jnp.concatenate([h0[i], v[i], u[i]], axis=0)) for i in heads]
    for i in heads:
        decay_tot = jnp.exp(jnp.sum(jnp.where(eye_n, head(tot_all, i), 0.0), axis=1, keepdims=True))
        h_ref[i] = decay_tot * h0[i] + _bdot_tn(
            jnp.concatenate([head(kt_all, i), -(b[i] * head(e_tail, i))], axis=0),
            jnp.concatenate([v[i], u[i]], axis=0))
    outs = []
    for i in heads:
        mu = jnp.mean(y[i], axis=-1, keepdims=True)
        var = jnp.mean(jnp.square(y[i] - mu), axis=-1, keepdims=True)
        yn = (y[i] - mu) * lax.rsqrt(var + RW_GN_EPS)
        outs.append(yn * head(lng_ref[...], i) + head(lnb_ref[...], i)
                    + jnp.sum(head(rkr_all, i), axis=-1, keepdims=True) * v[i])
    o_ref[0, rows, :] = (jnp.concatenate(outs, axis=-1) * g_ref[0, rows, :]).astype(o_ref.dtype)


def _rwkv_core(r, k, v, lw, a, g, k_k, k_a, r_k, ln_g, ln_b):
    B, T, D = r.shape
    C, N, HB, SUB = RW_CHUNK, RW_HEAD_DIM, RW_HEADS_PER_STEP, RW_CHUNKS_PER_STEP
    W = HB * N
    seq = pl.BlockSpec((1, SUB * C, W), lambda b, h, c: (b, c, h))
    par = pl.BlockSpec((1, W), lambda b, h, c: (0, h))
    return pl.pallas_call(
        functools.partial(_rwkv_kernel, C=C, N=N, HB=HB, SUB=SUB),
        grid=(B, D // W, T // (SUB * C)),
        in_specs=[seq] * 6 + [par] * 5,
        out_specs=seq,
        out_shape=jax.ShapeDtypeStruct((B, T, D), BF16),
        scratch_shapes=[pltpu.VMEM((HB, N, N), F32)],
        compiler_params=_params("parallel", "parallel", "arbitrary"),
        name="rwkv_core",
    )(r, k, v, lw, a, g, k_k.reshape(1, D), k_a.reshape(1, D), r_k.reshape(1, D),
      ln_g.reshape(1, D), ln_b.reshape(1, D))


def _rwkv7_mix(x_res, norm_g, sc, shift, gate, mu, w_rkv, w0, w1, w2, a0, a1, a2, g1, g2, k_k, k_a,
               r_k, ln_g, ln_b, w_o):
    B, T, D = x_res.shape
    M = B * T
    xr, xk, xv, xw, xa, xg = [z.reshape(M, D) for z in _rwkv_prep(x_res, norm_g, sc, shift, mu)]
    r = _matmul(xr, w_rkv, (0, 0), name="rwkv_r")
    k = _matmul(xk, w_rkv, (0, 1), name="rwkv_k")
    v = _matmul(xv, w_rkv, (0, 2), name="rwkv_v")
    log_decay = _matmul(_matmul(xw, w1, (0,), post="tanh", out_dtype=BF16), w2, (0,),
                        bias=w0.reshape(1, D), post="rwkv_log_decay", name="rwkv_decay")
    a = _matmul(_matmul(xa, a1, (0,), out_dtype=BF16), a2, (0,), bias=a0.reshape(1, D),
                post="sigmoid", name="rwkv_a")
    g = _matmul(_matmul(xg, g1, (0,), post="sigmoid", out_dtype=BF16), g2, (0,), name="rwkv_g")
    sh = (B, T, D)
    o = _rwkv_core(r.reshape(sh), k.reshape(sh), v.reshape(sh), log_decay.reshape(sh),
                   a.reshape(sh), g.reshape(sh), k_k, k_a, r_k, ln_g, ln_b)
    return _matmul(o.reshape(M, D), w_o, (0,), res=x_res.reshape(M, D), gate=gate).reshape(sh)


def _ret_kernel(q_ref, k_ref, v_ref, g_ref, cos_ref, sin_ref, dm_ref, xz_ref, gng_ref, gnb_ref,
                o_ref, s_ref, *, dk):
    @pl.when(pl.program_id(2) == 0)
    def _():
        s_ref[...] = jnp.zeros_like(s_ref)

    cos, sin = cos_ref[...], sin_ref[...]
    half = dk // 2

    def rot(z):
        z1, z2 = z[:, :half], z[:, half:]
        return jnp.concatenate([z1 * cos - z2 * sin, z1 * sin + z2 * cos], axis=-1)

    q = rot(q_ref[0])
    k = rot(k_ref[0] * (dk ** -0.5))
    v = v_ref[0]
    xi, zeta, g_chunk = xz_ref[0, :, 0:1], xz_ref[0, :, 1:2], xz_ref[0, 0:1, 2:3]
    scores = _bdot_nt(q, k) * dm_ref[0]
    state = s_ref[...]
    o = _bdot(scores, v) + _bdot(q * xi, state)
    s_ref[...] = g_chunk * state + _bdot_tn(k * zeta, v)
    mu = jnp.mean(o, axis=-1, keepdims=True)
    var = jnp.mean(jnp.square(o - mu), axis=-1, keepdims=True)
    y = (o - mu) * lax.rsqrt(var + NORM_EPS) * gng_ref[0] + gnb_ref[0]
    gate = g_ref[0]
    o_ref[0] = (y * (gate * jax.nn.sigmoid(gate))).astype(o_ref.dtype)


def _retention_mix(h, x_res, gate, w_in, gn_g, gn_b, w_o):
    B, T, D = h.shape
    M = B * T
    H, dk, dv, C = RET_HEADS, RET_DK, RET_DV, RET_CHUNK
    proj = _matmul(h.reshape(M, D), w_in, (0,)).reshape(B, T, H * (2 * dk + 2 * dv))
    qb, kb, vb, gb = 0, (H * dk) // dk, (2 * H * dk) // dv, (2 * H * dk + H * dv) // dv
    theta = 1.0 / (RET_ROT_BASE ** jnp.linspace(0.0, 1.0, dk // 2, dtype=F32))
    ang = jnp.arange(T, dtype=F32)[:, None] * theta[None, :]
    log_gamma = jnp.log(1.0 - 2.0 ** (-5.0 - jnp.arange(H, dtype=F32)))
    pos = jnp.arange(C, dtype=F32)
    rel = pos[:, None] - pos[None, :]
    dmask = jnp.where(rel >= 0, jnp.exp(jnp.maximum(rel, 0.0) * log_gamma[:, None, None]), 0.0)
    xi = jnp.exp((pos + 1.0)[None, :] * log_gamma[:, None])
    zeta = jnp.exp((C - 1.0 - pos)[None, :] * log_gamma[:, None])
    g_chunk = jnp.broadcast_to(jnp.exp(C * log_gamma)[:, None], (H, C))
    xz = jnp.concatenate([jnp.stack([xi, zeta, g_chunk], axis=-1), jnp.zeros((H, C, 125), F32)], axis=-1)
    o = pl.pallas_call(
        functools.partial(_ret_kernel, dk=dk),
        grid=(B, H, T // C),
        in_specs=[
            pl.BlockSpec((1, C, dk), lambda b, h, c: (b, c, qb + h)),
            pl.BlockSpec((1, C, dk), lambda b, h, c: (b, c, kb + h)),
            pl.BlockSpec((1, C, dv), lambda b, h, c: (b, c, vb + h)),
            pl.BlockSpec((1, C, dv), lambda b, h, c: (b, c, gb + h)),
            pl.BlockSpec((C, dk // 2), lambda b, h, c: (c, 0)),
            pl.BlockSpec((C, dk // 2), lambda b, h, c: (c, 0)),
            pl.BlockSpec((1, C, C), lambda b, h, c: (h, 0, 0)),
            pl.BlockSpec((1, C, 128), lambda b, h, c: (h, 0, 0)),
            pl.BlockSpec((1, 1, dv), lambda b, h, c: (h, 0, 0)),
            pl.BlockSpec((1, 1, dv), lambda b, h, c: (h, 0, 0)),
        ],
        out_specs=pl.BlockSpec((1, C, dv), lambda b, h, c: (b, c, h)),
        out_shape=jax.ShapeDtypeStruct((B, T, H * dv), BF16),
        scratch_shapes=[pltpu.VMEM((dk, dv), F32)],
        compiler_params=_params("parallel", "parallel", "arbitrary"),
        name="retention_core",
    )(proj, proj, proj, proj, jnp.cos(ang), jnp.sin(ang), dmask, xz,
      gn_g.reshape(H, 1, dv), gn_b.reshape(H, 1, dv))
    return _matmul(o.reshape(M, H * dv), w_o, (0,), res=x_res.reshape(M, D), gate=gate).reshape(B, T, D)


def _gla_kernel(q_ref, k_ref, v_ref, g_ref, la_ref, gn_ref, o_ref, s_ref, *, C, dk):
    @pl.when(pl.program_id(2) == 0)
    def _():
        s_ref[...] = jnp.zeros_like(s_ref)

    row = lax.broadcasted_iota(jnp.int32, (C, C), 0)
    col = lax.broadcasted_iota(jnp.int32, (C, C), 1)
    causal = row >= col
    la = la_ref[0]
    b = _exact_lhs_dot(causal.astype(F32), la)
    b_last = b[C - 1:C, :]
    eye_k = (lax.broadcasted_iota(jnp.int32, (dk, dk), 0) == lax.broadcasted_iota(jnp.int32, (dk, dk), 1))
    d_last_col = jnp.exp(jnp.sum(jnp.where(eye_k, b_last, 0.0), axis=1, keepdims=True))
    k, v = k_ref[0], v_ref[0]
    q_in = q_ref[0] * (dk ** -0.5) * jnp.exp(b)
    k_in = k * jnp.exp(-b)
    att = jnp.where(causal, _bdot_nt(q_in, k_in), 0.0)
    state = s_ref[...]
    o = _bdot(att, v) + _bdot(q_in, state)
    s_ref[...] = d_last_col * state + _bdot_tn(k * jnp.exp(b_last - b), v)
    y = o * lax.rsqrt(jnp.mean(o * o, axis=-1, keepdims=True) + NORM_EPS) * gn_ref[...]
    gate = g_ref[0]
    o_ref[0] = (y * (gate * jax.nn.sigmoid(gate))).astype(o_ref.dtype)


def _gla_mix(h, x_res, gate, w_in, w_a1, w_a2, b_a, gn_g, w_o):
    B, T, D = h.shape
    M = B * T
    H, dk, dv, C = GLA_HEADS, GLA_DK, GLA_DV, GLA_CHUNK
    h2 = h.reshape(M, D)
    proj = _matmul(h2, w_in, (0,)).reshape(B, T, H * (2 * dk + 2 * dv))
    log_a = _matmul(_matmul(h2, w_a1, (0,), out_dtype=BF16), w_a2, (0,), bias=b_a.reshape(1, H * dk),
                    post="gla_log_gate", name="gla_log_gate").reshape(B, T, H * dk)
    qb, kb, vb, gb = 0, H, (2 * H * dk) // dv, (2 * H * dk + H * dv) // dv
    o = pl.pallas_call(
        functools.partial(_gla_kernel, C=C, dk=dk),
        grid=(B, H, T // C),
        in_specs=[
            pl.BlockSpec((1, C, dk), lambda b, h, c: (b, c, qb + h)),
            pl.BlockSpec((1, C, dk), lambda b, h, c: (b, c, kb + h)),
            pl.BlockSpec((1, C, dv), lambda b, h, c: (b, c, vb + h)),
            pl.BlockSpec((1, C, dv), lambda b, h, c: (b, c, gb + h)),
            pl.BlockSpec((1, C, dk), lambda b, h, c: (b, c, h)),
            pl.BlockSpec((1, dv), lambda b, h, c: (0, 0)),
        ],
        out_specs=pl.BlockSpec((1, C, dv), lambda b, h, c: (b, c, h)),
        out_shape=jax.ShapeDtypeStruct((B, T, H * dv), BF16),
        scratch_shapes=[pltpu.VMEM((dk, dv), F32)],
        compiler_params=_params("parallel", "parallel", "arbitrary"),
        name="gla_core",
    )(proj, proj, proj, proj, log_a, gn_g.reshape(1, dv))
    return _matmul(o.reshape(M, H * dv), w_o, (0,), res=x_res.reshape(M, D), gate=gate).reshape(B, T, D)


def _nsa_cmp_kernel(q_ref, kc_ref, vc_ref, o_ref, sel_ref, *, tq, n_cmp, n_pad, n_slc, hpg):
    L, S, Ls, hd = NSA_CMP_BLK, NSA_CMP_STRIDE, NSA_SEL_BLK, NSA_HD
    t_pos = pl.program_id(2) * tq + lax.broadcasted_iota(jnp.int32, (tq, 1), 0)
    n_ix = lax.broadcasted_iota(jnp.int32, (1, n_pad), 1)
    valid = (n_ix * S + (L - 1) <= t_pos) & (n_ix < n_cmp)
    validf = valid.astype(F32)
    c_start = lax.broadcasted_iota(jnp.int32, (n_pad, n_slc), 0) * S
    s_start = lax.broadcasted_iota(jnp.int32, (n_pad, n_slc), 1) * Ls
    overlap = ((c_start <= s_start + (Ls - 1)) & (c_start + (L - 1) >= s_start)
               & (c_start < n_cmp * S)).astype(F32)
    kc, vc = kc_ref[0, 0], vc_ref[0, 0]
    imp = jnp.zeros((tq, n_slc), F32)
    outs = []
    for hh in range(hpg):
        s = _fdot_nt(q_ref[0, :, hh * hd:(hh + 1) * hd], kc)
        s = jnp.where(valid, s, NEG_INF)
        e = jnp.exp(s - jnp.max(s, axis=-1, keepdims=True))
        p = e / jnp.sum(e, axis=-1, keepdims=True) * validf
        outs.append(_bdot(p, vc))
        imp = imp + _exact_rhs_dot(p, overlap)
    o_ref[0] = jnp.concatenate(outs, axis=-1)

    j = lax.broadcasted_iota(jnp.int32, (1, n_slc), 1)
    cur = t_pos // Ls
    forced = (j == 0) | (j == cur) | (j == cur - 1)
    imp = jnp.where(j > cur, -IMP_BIG, jnp.where(forced, IMP_BIG, imp))
    rank = jnp.zeros((tq, n_slc), jnp.int32)
    for jp in range(n_slc):
        c = imp[:, jp:jp + 1]
        rank = rank + ((c > imp) | ((c == imp) & (jp < j))).astype(jnp.int32)
    sel_ref[0, 0] = (rank < min(NSA_SEL_TOPK, n_slc)).astype(F32)


def _nsa_att_kernel(q_ref, ks_ref, vs_ref, kw_ref, vw_ref, sel_ref, oc_ref, gt_ref, o_ref,
                    m_ref, l_ref, acc_ref, *, tq, tk, T, n_slc, hpg):
    Ls, W, hd = NSA_SEL_BLK, NSA_WINDOW, NSA_HD
    qi = pl.program_id(2)
    t0 = qi * tq
    q = jnp.concatenate([q_ref[0, :, hh * hd:(hh + 1) * hd] for hh in range(hpg)], axis=0)
    t_q = t0 + lax.broadcasted_iota(jnp.int32, (tq, 1), 0)
    sel = sel_ref[0, 0]

    m_ref[...] = jnp.full_like(m_ref, NEG_INF)
    l_ref[...] = jnp.zeros_like(l_ref)
    acc_ref[...] = jnp.zeros_like(acc_ref)

    def body(kb, carry):
        k0 = pl.multiple_of(kb * tk, tk)
        kt = ks_ref[0, pl.ds(k0, tk), :]
        vt = vs_ref[0, pl.ds(k0, tk), :]
        blk_of_key = (k0 + lax.broadcasted_iota(jnp.int32, (n_slc, tk), 1)) // Ls
        expand = (blk_of_key == lax.broadcasted_iota(jnp.int32, (n_slc, tk), 0)).astype(F32)
        kpos = k0 + lax.broadcasted_iota(jnp.int32, (1, tk), 1)
        bias = jnp.where((_bdot(sel, expand) > 0.5) & (kpos <= t_q), 0.0, NEG_INF)
        s_all = _bdot_nt(q, kt)
        ps = []
        for hh in range(hpg):
            rows = slice(hh * tq, (hh + 1) * tq)
            s = s_all[rows] + bias
            m_old = m_ref[rows]
            m_new = jnp.maximum(m_old, jnp.max(s, axis=-1, keepdims=True))
            alpha = jnp.exp(m_old - m_new)
            p = jnp.exp(s - m_new)
            l_ref[rows] = alpha * l_ref[rows] + jnp.sum(p, axis=-1, keepdims=True)
            acc_ref[rows] = alpha * acc_ref[rows]
            m_ref[rows] = m_new
            ps.append(p.astype(BF16))
        acc_ref[...] += jnp.dot(jnp.concatenate(ps, axis=0), vt, preferred_element_type=F32)
        return carry

    lax.fori_loop(0, (t0 + tq + tk - 1) // tk, body, 0)
    o_sel = acc_ref[...] / l_ref[...]

    span = W + tq
    w0 = pl.multiple_of(jnp.maximum(t0 - W, 0), tq)
    kw = kw_ref[0, pl.ds(w0, span), :]
    vw = vw_ref[0, pl.ds(w0, span), :]
    wpos = w0 + lax.broadcasted_iota(jnp.int32, (1, span), 1)
    bias_w = jnp.where((wpos <= t_q) & (wpos > t_q - W), 0.0, NEG_INF)
    s_all = _bdot_nt(q, kw)
    es, ls = [], []
    for hh in range(hpg):
        s = s_all[hh * tq:(hh + 1) * tq] + bias_w
        e = jnp.exp(s - jnp.max(s, axis=-1, keepdims=True))
        ls.append(jnp.sum(e, axis=-1, keepdims=True))
        es.append(e.astype(BF16))
    o_win = (jnp.dot(jnp.concatenate(es, axis=0), vw, preferred_element_type=F32)
             / jnp.concatenate(ls, axis=0))

    gt = jax.nn.sigmoid(gt_ref[0, 0])
    outs = []
    for hh in range(hpg):
        rows = slice(hh * tq, (hh + 1) * tq)
        outs.append(gt[:, hh:hh + 1] * oc_ref[0, :, hh * hd:(hh + 1) * hd]
                    + gt[:, hpg + hh:hpg + hh + 1] * o_sel[rows]
                    + gt[:, 2 * hpg + hh:2 * hpg + hh + 1] * o_win[rows])
    o_ref[0] = jnp.concatenate(outs, axis=-1).astype(o_ref.dtype)


def _rms_norm(x, g):
    return x * lax.rsqrt(jnp.mean(x * x, axis=-1, keepdims=True) + NORM_EPS) * g


def _nsa_mix(h, x_res, gate, w_in, q_g, k_g, cmp_pos, cmp_w1, cmp_w2, w_o):
    B, T, D = h.shape
    M = B * T
    H, G, hd = NSA_HEADS, NSA_KV_HEADS, NSA_HD
    hpg = H // G
    L, S, Ls = NSA_CMP_BLK, NSA_CMP_STRIDE, NSA_SEL_BLK
    kvw = G * hd
    n_main = H * hd + 6 * kvw
    h2 = h.reshape(M, D)
    proj = _matmul(h2, w_in, (0,), n_cols=n_main)
    gate_w = jnp.pad(w_in[0, :, n_main:], ((0, 0), (0, 128 - 3 * H)))
    gates = _matmul(h2, gate_w)[:, :3 * H]
    q = _rms_norm(proj[:, :H * hd].reshape(B, T, H, hd), q_g) * (hd ** -0.5)
    q = q.reshape(B, T, H * hd)

    def kv(i):
        return proj[:, H * hd + i * kvw:H * hd + (i + 1) * kvw].reshape(B, T, G, hd)

    n_cmp = (T - L) // S + 1
    n_grp = T // S
    assert L == 2 * S and n_cmp == n_grp - 1

    def compress(z, pos, w1, w2):
        zg = z.reshape(B, n_grp, S, G, hd).transpose(0, 3, 1, 2, 4).reshape(B * G * n_grp, S * hd)
        w_halves = jnp.concatenate([w1[:S * hd], w1[S * hd:]], axis=1)
        part = _matmul(zg, w_halves, tm=min(512, B * G * n_grp)).reshape(B, G, n_grp, 2 * hd)
        pos_term = _matmul(jnp.broadcast_to(pos.reshape(1, L * hd), (8, L * hd)), w1, tm=8)[0]
        pre = part[:, :, :-1, :hd] + part[:, :, 1:, hd:] + pos_term
        pre = jnp.pad(pre, ((0, 0), (0, 0), (0, 1), (0, 0))).reshape(B * G * n_grp, hd)
        return _matmul(jax.nn.silu(pre), w2, tm=min(512, B * G * n_grp)).reshape(B, G, n_grp, hd)

    kc = _rms_norm(compress(kv(0), cmp_pos[0], cmp_w1[0], cmp_w2[0]), k_g[0])
    vc = compress(kv(1), cmp_pos[1], cmp_w1[1], cmp_w2[1])
    k_sel = _rms_norm(kv(2), k_g[1]).reshape(B, T, kvw).astype(BF16)
    v_sel = kv(3).reshape(B, T, kvw).astype(BF16)
    k_win = _rms_norm(kv(4), k_g[2]).reshape(B, T, kvw).astype(BF16)
    v_win = kv(5).reshape(B, T, kvw).astype(BF16)

    n_slc = T // Ls
    tq = min(NSA_CMP_TQ, T)
    o_cmp, sel = pl.pallas_call(
        functools.partial(_nsa_cmp_kernel, tq=tq, n_cmp=n_cmp, n_pad=n_grp, n_slc=n_slc, hpg=hpg),
        grid=(B, G, T // tq),
        in_specs=[
            pl.BlockSpec((1, tq, hpg * hd), lambda b, g, i: (b, i, g)),
            pl.BlockSpec((1, 1, n_grp, hd), lambda b, g, i: (b, g, 0, 0)),
            pl.BlockSpec((1, 1, n_grp, hd), lambda b, g, i: (b, g, 0, 0)),
        ],
        out_specs=[
            pl.BlockSpec((1, tq, hpg * hd), lambda b, g, i: (b, i, g)),
            pl.BlockSpec((1, 1, tq, n_slc), lambda b, g, i: (b, g, i, 0)),
        ],
        out_shape=[jax.ShapeDtypeStruct((B, T, H * hd), F32),
                   jax.ShapeDtypeStruct((B, G, T, n_slc), F32)],
        compiler_params=_params("parallel", "parallel", "parallel"),
        name="nsa_compressed",
    )(q, kc, vc)

    gt = gates.reshape(B, T, 3, G, hpg).transpose(0, 3, 1, 2, 4).reshape(B, G, T, 3 * hpg)
    tq = min(NSA_ATT_TQ, T)
    tk = min(NSA_ATT_TK, T)
    assert T >= NSA_WINDOW + tq
    kv_spec = pl.BlockSpec((1, T, hd), lambda b, g, i: (b, 0, g))
    o = pl.pallas_call(
        functools.partial(_nsa_att_kernel, tq=tq, tk=tk, T=T, n_slc=n_slc, hpg=hpg),
        grid=(B, G, T // tq),
        in_specs=[
            pl.BlockSpec((1, tq, hpg * hd), lambda b, g, i: (b, i, g)),
            kv_spec, kv_spec, kv_spec, kv_spec,
            pl.BlockSpec((1, 1, tq, n_slc), lambda b, g, i: (b, g, i, 0)),
            pl.BlockSpec((1, tq, hpg * hd), lambda b, g, i: (b, i, g)),
            pl.BlockSpec((1, 1, tq, 3 * hpg), lambda b, g, i: (b, g, i, 0)),
        ],
        out_specs=pl.BlockSpec((1, tq, hpg * hd), lambda b, g, i: (b, i, g)),
        out_shape=jax.ShapeDtypeStruct((B, T, H * hd), BF16),
        scratch_shapes=[pltpu.VMEM((hpg * tq, 1), F32), pltpu.VMEM((hpg * tq, 1), F32),
                        pltpu.VMEM((hpg * tq, hd), F32)],
        compiler_params=_params("parallel", "parallel", "arbitrary"),
        name="nsa_selected_window",
    )(q.astype(BF16), k_sel, v_sel, k_win, v_win, sel, o_cmp, gt)
    return _matmul(o.reshape(M, H * hd), w_o, (0,), res=x_res.reshape(M, D), gate=gate).reshape(B, T, D)


def _clamped_swiglu(gu):
    x_glu = jnp.minimum(gu[:, :D_EXPERT], SWIGLU_LIMIT)
    x_lin = jnp.clip(gu[:, D_EXPERT:], -SWIGLU_LIMIT, SWIGLU_LIMIT)
    return x_glu * jax.nn.sigmoid(SWIGLU_ALPHA * x_glu) * (x_lin + 1.0)


def _expert_kernel(blk_e_ref, first_ref, next_ref, used_ref, x_ref, w_hbm, b_ref, o_ref, stage, w_bf16,
                   sem, *, layer, post):
    i = pl.program_id(0)

    def fetch(expert):
        return pltpu.make_async_copy(w_hbm.at[layer, expert], stage, sem.at[0])

    @pl.when(i == 0)
    def _():
        fetch(blk_e_ref[0]).start()

    @pl.when(first_ref[i] == 1)
    def _():
        fetch(blk_e_ref[i]).wait()
        w_bf16[...] = stage[...].astype(BF16)

        @pl.when(next_ref[i] >= 0)
        def _():
            fetch(next_ref[i]).start()

    @pl.when(used_ref[i] == 1)
    def _():
        acc = jnp.dot(x_ref[...], w_bf16[...], preferred_element_type=F32) + b_ref[0]
        o_ref[...] = (acc if post is None else post(acc)).astype(o_ref.dtype)

    @pl.when(used_ref[i] == 0)
    def _():
        o_ref[...] = jnp.zeros_like(o_ref)


def _expert_matmul(tables, x, w, b, layer, post, n_out, out_dtype, name):
    cap, Kd = x.shape
    E, Nd = w.shape[1], w.shape[3]
    R = MOE_ROWS
    return pl.pallas_call(
        functools.partial(_expert_kernel, layer=layer, post=post),
        grid_spec=pltpu.PrefetchScalarGridSpec(
            num_scalar_prefetch=4,
            grid=(cap // R,),
            in_specs=[
                pl.BlockSpec((R, Kd), lambda i, be, fi, nx, us: (i, 0)),
                pl.BlockSpec(memory_space=pl.ANY),
                pl.BlockSpec((None, 1, 1, Nd), lambda i, be, fi, nx, us: (layer, be[i], 0, 0)),
            ],
            out_specs=pl.BlockSpec((R, n_out), lambda i, be, fi, nx, us: (i, 0)),
            scratch_shapes=[pltpu.VMEM((Kd, Nd), F32), pltpu.VMEM((Kd, Nd), BF16),
                            pltpu.SemaphoreType.DMA((1,))],
        ),
        out_shape=jax.ShapeDtypeStruct((cap, n_out), out_dtype),
        compiler_params=_params("arbitrary"),
        name=name,
    )(*tables, x, w, b.reshape(b.shape[0], E, 1, Nd))


def _dispatch_kernel(tok_ref, used_ref, h_hbm, o_ref, buf, sem, *, R):
    i = pl.program_id(0)
    n_steps = pl.num_programs(0)

    def fetch(step, par):
        def one_row(r, carry):
            pltpu.make_async_copy(h_hbm.at[pl.ds(tok_ref[step * R + r], 1)], buf.at[par, pl.ds(r, 1)],
                                  sem.at[par]).start()
            return carry
        lax.fori_loop(0, R, one_row, 0, unroll=8)

    @pl.when((i == 0) & (used_ref[0] == 1))
    def _():
        fetch(0, 0)

    nxt = jnp.minimum(i + 1, n_steps - 1)

    @pl.when((i + 1 < n_steps) & (used_ref[nxt] == 1))
    def _():
        fetch(nxt, nxt % 2)

    @pl.when(used_ref[i] == 1)
    def _():
        par = i % 2
        pltpu.make_async_copy(h_hbm.at[pl.ds(0, R)], buf.at[par], sem.at[par]).wait()
        o_ref[...] = buf[par].astype(o_ref.dtype)

    @pl.when(used_ref[i] == 0)
    def _():
        o_ref[...] = jnp.zeros_like(o_ref)


def _moe_dispatch(h, tok, used):
    N, D = h.shape
    R = MOE_ROWS
    cap = tok.shape[0]
    return pl.pallas_call(
        functools.partial(_dispatch_kernel, R=R),
        grid_spec=pltpu.PrefetchScalarGridSpec(
            num_scalar_prefetch=2,
            grid=(cap // R,),
            in_specs=[pl.BlockSpec(memory_space=pl.ANY)],
            out_specs=pl.BlockSpec((R, D), lambda i, t, u: (i, 0)),
            scratch_shapes=[pltpu.VMEM((2, R, D), F32), pltpu.SemaphoreType.DMA((2,))],
        ),
        out_shape=jax.ShapeDtypeStruct((cap, D), BF16),
        compiler_params=_params("arbitrary"),
        name="moe_dispatch",
    )(tok, used, h)


def _router_kernel(x_ref, w_ref, b_ref, et_ref, p_ref, cnt_ref):
    logits = _fdot(x_ref[...], w_ref[...]) + b_ref[...]
    lane = lax.broadcasted_iota(jnp.int32, logits.shape, 1)
    vals, idxs = [], []
    for _ in range(TOP_K):
        m = jnp.max(logits, axis=-1, keepdims=True)
        idx = jnp.min(jnp.where(logits == m, lane, logits.shape[1]), axis=-1, keepdims=True)
        vals.append(m)
        idxs.append(idx)
        logits = jnp.where(lane == idx, -IMP_BIG, logits)
    es = [jnp.exp(v - vals[0]) for v in vals]
    total = sum(es)
    e_out = jnp.zeros(logits.shape, jnp.int32)
    p_out = jnp.zeros(logits.shape, F32)
    picks = jnp.zeros(logits.shape, jnp.int32)
    for k in range(TOP_K):
        e_out = jnp.where(lane == k, idxs[k], e_out)
        p_out = jnp.where(lane == k, es[k] / total, p_out)
        picks = picks + (lane == idxs[k]).astype(jnp.int32)
    et_ref[...] = e_out.T[:ROUTER_ROWS]
    p_ref[...] = p_out
    cnt_ref[0] = jnp.sum(picks, axis=0, keepdims=True)


def _route(x, w_router, b_router, layer):
    N, D = x.shape
    E, K = N_EXPERTS, TOP_K
    lanes = 128
    w_r = jnp.pad(w_router[layer], ((0, 0), (0, lanes - E)))
    b_r = jnp.concatenate([b_router[layer], jnp.full((lanes - E,), NEG_INF, F32)]).reshape(1, lanes)
    tm = min(N, 512)
    return pl.pallas_call(
        _router_kernel,
        grid=(N // tm,),
        in_specs=[pl.BlockSpec((tm, D), lambda i: (i, 0)),
                  pl.BlockSpec((D, lanes), lambda i: (0, 0)),
                  pl.BlockSpec((1, lanes), lambda i: (0, 0))],
        out_specs=[pl.BlockSpec((ROUTER_ROWS, tm), lambda i: (0, i)),
                   pl.BlockSpec((tm, lanes), lambda i: (i, 0)),
                   pl.BlockSpec((1, 1, lanes), lambda i: (i, 0, 0))],
        out_shape=[jax.ShapeDtypeStruct((ROUTER_ROWS, N), jnp.int32),
                   jax.ShapeDtypeStruct((N, lanes), F32),
                   jax.ShapeDtypeStruct((N // tm, 1, lanes), jnp.int32)],
        compiler_params=_params("parallel"),
        name="moe_router",
    )(x, w_r, b_r)


def _combine_kernel(slot_ref, out_hbm, p_ref, res_ref, gate_ref, o_ref, buf, sem, *, TT, K, N):
    i = pl.program_id(0)
    n_steps = pl.num_programs(0)

    def fetch(step, par):
        for k in range(K):
            def one_row(r, carry):
                row = slot_ref[k * N + step * TT + r]
                pltpu.make_async_copy(out_hbm.at[pl.ds(row, 1)], buf.at[par, pl.ds(k * TT + r, 1)],
                                      sem.at[par]).start()
                return carry
            lax.fori_loop(0, TT, one_row, 0, unroll=8)

    @pl.when(i == 0)
    def _():
        fetch(0, 0)

    @pl.when(i + 1 < n_steps)
    def _():
        fetch(i + 1, (i + 1) % 2)

    par = i % 2
    pltpu.make_async_copy(out_hbm.at[pl.ds(0, K * TT)], buf.at[par], sem.at[par]).wait()
    p = p_ref[...]
    y = p[:, 0:1] * buf[par, 0:TT, :]
    for k in range(1, K):
        y = y + p[:, k:k + 1] * buf[par, k * TT:(k + 1) * TT, :]
    o_ref[...] = res_ref[...] + gate_ref[0] * y


def _moe_combine(out, slot, top_w, x_res, gate, T):
    N, D = x_res.shape
    K, TT = TOP_K, MOE_COMBINE_TOKENS
    return pl.pallas_call(
        functools.partial(_combine_kernel, TT=TT, K=K, N=N),
        grid_spec=pltpu.PrefetchScalarGridSpec(
            num_scalar_prefetch=1,
            grid=(N // TT,),
            in_specs=[
                pl.BlockSpec(memory_space=pl.ANY),
                pl.BlockSpec((TT, top_w.shape[1]), lambda i, s: (i, 0)),
                pl.BlockSpec((TT, D), lambda i, s: (i, 0)),
                pl.BlockSpec((1, 1, D), lambda i, s: ((i * TT) // T, 0, 0)),
            ],
            out_specs=pl.BlockSpec((TT, D), lambda i, s: (i, 0)),
            scratch_shapes=[pltpu.VMEM((2, K * TT, D), F32), pltpu.SemaphoreType.DMA((2,))],
        ),
        out_shape=jax.ShapeDtypeStruct((N, D), F32),
        compiler_params=_params("arbitrary"),
        name="moe_combine",
    )(slot, out, top_w, x_res, gate)


def _moe_ffn(h, layer, x_res, gate, w_router, b_router, w_gu, b_gu, w_down, b_down):
    B, T, D = h.shape
    N = B * T
    E, K, R = N_EXPERTS, TOP_K, MOE_ROWS
    NK = N * K
    x = h.reshape(N, D)
    top_e_t, top_w, tile_counts = _route(x, w_router, b_router, layer)
    pair = jnp.arange(NK, dtype=jnp.int32)
    e_s, order = lax.sort((top_e_t[:K].reshape(-1), pair), num_keys=1, is_stable=True)
    counts = jnp.sum(tile_counts, axis=(0, 1))[:E]
    padded = (counts + R - 1) // R * R
    g_start = jnp.cumsum(counts) - counts
    p_end = jnp.cumsum(padded)
    shift = (p_end - padded) - g_start
    dest = pair + jnp.take(shift, e_s, axis=0)
    n_blocks = (NK + R - 1) // R + E
    cap = n_blocks * R
    experts = jnp.arange(E, dtype=jnp.int32)[None, :]
    blocks = jnp.arange(n_blocks, dtype=jnp.int32)
    blk_e = jnp.minimum(jnp.sum(p_end[None, :] <= (blocks * R)[:, None], axis=1), E - 1).astype(jnp.int32)
    blk_is = blk_e[:, None] == experts
    blk_shift = jnp.sum(jnp.where(blk_is, shift[None, :], 0), axis=1)
    used = (blocks * R < p_end[E - 1]).astype(jnp.int32)
    first = used * (blk_e != jnp.concatenate([jnp.full((1,), -1, jnp.int32), blk_e[:-1]])).astype(jnp.int32)
    after = jnp.sum(jnp.where(blk_is, p_end[None, :], 0), axis=1) // R
    after_e = jnp.sum(jnp.where(after[:, None] == blocks[None, :], blk_e[None, :], 0), axis=1)
    next_e = jnp.where(after * R < p_end[E - 1], after_e, -1).astype(jnp.int32)
    tables = (blk_e, first, next_e, used)
    src = jnp.clip(jnp.arange(cap, dtype=jnp.int32) - jnp.repeat(blk_shift, R), 0, NK - 1)
    tok = jnp.take(order, src, axis=0) % N
    xb = _moe_dispatch(x, tok, used)
    act = _expert_matmul(tables, xb, w_gu, b_gu, layer, _clamped_swiglu, D_EXPERT, BF16, "moe_up")
    out = _expert_matmul(tables, act, w_down, b_down, layer, None, D, F32, "moe_down")

    _, slot = lax.sort((order, dest), num_keys=1)
    return _moe_combine(out, slot, top_w, x_res.reshape(N, D), gate, T).reshape(B, T, D)


def kernel(x, c, ada_w, ada_b, norm_g, rw_mu, rw_w_rkv, rw_w0, rw_w1, rw_w2, rw_a0, rw_a1, rw_a2, rw_g1, rw_g2, rw_k_k, rw_k_a, rw_r_k, rw_ln_g, rw_ln_b, rw_w_o, ret_w_in, ret_gn_g, ret_gn_b, ret_w_o, gla_w_in, gla_w_a1, gla_w_a2, gla_b_a, gla_gn_g, gla_w_o, nsa_w_in, nsa_q_g, nsa_k_g, nsa_cmp_pos, nsa_cmp_w1, nsa_cmp_w2, nsa_w_o, moe_router_w, moe_router_b, moe_w_gu, moe_b_gu, moe_w_down, moe_b_down):
    B, T, D = x.shape
    depth = ada_w.shape[0]
    c_act = jnp.pad(jax.nn.silu(c), ((0, 8 - B), (0, 0)))
    for i in range(depth):
        mod = _matmul(c_act, ada_w, (i,), tm=8, tn=1024)[:B] + ada_b[i]
        sh1, sc1, gt1, sh2, sc2, gt2 = jnp.split(mod, 6, axis=-1)
        gt1, gt2 = gt1.reshape(B, 1, D), gt2.reshape(B, 1, D)
        m, j = i % 4, i // 4
        if m == 0:
            x = _rwkv7_mix(x, norm_g[i, 0], sc1, sh1, gt1, rw_mu[j], rw_w_rkv[j:j + 1], rw_w0[j],
                           rw_w1[j:j + 1], rw_w2[j:j + 1], rw_a0[j], rw_a1[j:j + 1], rw_a2[j:j + 1],
                           rw_g1[j:j + 1], rw_g2[j:j + 1], rw_k_k[j], rw_k_a[j], rw_r_k[j], rw_ln_g[j],
                           rw_ln_b[j], rw_w_o[j:j + 1])
        elif m == 1:
            h, = _norm_modulate(x, norm_g[i, 0], sc1, sh1, (BF16,))
            x = _retention_mix(h, x, gt1, ret_w_in[j:j + 1], ret_gn_g[j], ret_gn_b[j], ret_w_o[j:j + 1])
        elif m == 2:
            h, = _norm_modulate(x, norm_g[i, 0], sc1, sh1, (BF16,))
            x = _gla_mix(h, x, gt1, gla_w_in[j:j + 1], gla_w_a1[j:j + 1], gla_w_a2[j:j + 1], gla_b_a[j],
                         gla_gn_g[j], gla_w_o[j:j + 1])
        else:
            h, = _norm_modulate(x, norm_g[i, 0], sc1, sh1, (BF16,))
            x = _nsa_mix(h, x, gt1, nsa_w_in[j:j + 1], nsa_q_g[j], nsa_k_g[j], nsa_cmp_pos[j],
                         nsa_cmp_w1[j], nsa_cmp_w2[j], nsa_w_o[j:j + 1])
        h, = _norm_modulate(x, norm_g[i, 1], sc2, sh2, (F32,))
        x = _moe_ffn(h, i, x, gt2, moe_router_w, moe_router_b, moe_w_gu, moe_b_gu, moe_w_down,
                     moe_b_down)
    return x
```

```python
import functools
import math

import jax
import jax.numpy as jnp
from jax import lax
from jax.experimental import pallas as pl
from jax.experimental.pallas import tpu as pltpu

F32 = jnp.float32
BF16 = jnp.bfloat16
HIGHEST = lax.Precision.HIGHEST

NORM_EPS = 1e-6
NEG_INF = -1e30

RW_HEAD_DIM = 64
RW_GN_EPS = 64e-5
RW_CHUNK = 128
RW_HEADS_PER_STEP = 8
RW_CHUNKS_PER_STEP = 1
RW_INV_BASE = 8

RET_HEADS = 8
RET_DK = 256
RET_DV = 512
RET_CHUNK = 128
RET_ROT_BASE = 10000.0

GLA_HEADS = 4
GLA_DK = 256
GLA_DV = 512
GLA_GATE_NORM = 16.0
GLA_CHUNK = 64

NSA_HEADS = 16
NSA_KV_HEADS = 4
NSA_HD = 128
NSA_CMP_BLK = 32
NSA_CMP_STRIDE = 16
NSA_SEL_BLK = 64
NSA_SEL_TOPK = 16
NSA_WINDOW = 512
NSA_CMP_TQ = 256
NSA_ATT_TQ = 128
NSA_ATT_TK = 512
IMP_BIG = 3e38

N_EXPERTS = 32
TOP_K = 4
D_EXPERT = 768
SWIGLU_ALPHA = 1.702
SWIGLU_LIMIT = 7.0
MOE_ROWS = 256
MOE_COMBINE_TOKENS = 64
ROUTER_ROWS = 8

VMEM_LIMIT_BYTES = 52 * 1024 * 1024


def _params(*sem):
    return pltpu.CompilerParams(dimension_semantics=sem, vmem_limit_bytes=VMEM_LIMIT_BYTES)


def _bdot(a, b):
    return jnp.dot(a.astype(BF16), b.astype(BF16), preferred_element_type=F32)


def _bdot_nt(a, b):
    return lax.dot_general(a.astype(BF16), b.astype(BF16), (((1,), (1,)), ((), ())),
                           preferred_element_type=F32)


def _bdot_tn(a, b):
    return lax.dot_general(a.astype(BF16), b.astype(BF16), (((0,), (0,)), ((), ())),
                           preferred_element_type=F32)


def _fdot(a, b):
    return jnp.dot(a, b, precision=HIGHEST, preferred_element_type=F32)


def _fdot_nt(a, b):
    return lax.dot_general(a, b, (((1,), (1,)), ((), ())), precision=HIGHEST,
                           preferred_element_type=F32)


def _split3(x):
    hi = x.astype(BF16)
    rem = x - hi.astype(F32)
    mid = rem.astype(BF16)
    return hi, mid, (rem - mid.astype(F32)).astype(BF16)


def _exact_lhs_dot(m, x):
    mb = m.astype(BF16)
    return sum(jnp.dot(mb, part, preferred_element_type=F32) for part in _split3(x))


def _exact_rhs_dot(x, m):
    mb = m.astype(BF16)
    return sum(jnp.dot(part, mb, preferred_element_type=F32) for part in _split3(x))


def _softplus(z):
    return jnp.maximum(z, 0.0) + jnp.log(1.0 + jnp.exp(-jnp.abs(z)))


_POST = {
    None: lambda z: z,
    "tanh": jnp.tanh,
    "sigmoid": jax.nn.sigmoid,
    "silu": jax.nn.silu,
    "rwkv_log_decay": lambda z: -jnp.exp(-_softplus(-z) - 0.5),
    "gla_log_gate": lambda z: -_softplus(-z) / GLA_GATE_NORM,
}


def _mm_kernel(*refs, has_bias, has_res, post):
    x_ref, w_ref = refs[0], refs[1]
    pos = 2
    acc = _bdot(x_ref[...], w_ref[...])
    if has_bias:
        acc = acc + refs[pos][...]
        pos += 1
    acc = _POST[post](acc)
    if has_res:
        acc = refs[pos][...] + refs[pos + 1][0] * acc
        pos += 2
    o_ref = refs[pos]
    o_ref[...] = acc.astype(o_ref.dtype)


def _matmul(x, w, lead=(), *, bias=None, res=None, gate=None, n_cols=None, tm=None, tn=None,
            post=None, out_dtype=F32, name="matmul"):
    M, K = x.shape
    N = n_cols if n_cols is not None else w.shape[-1]
    if tm is None:
        tm = min(M, 1024 if x.dtype == BF16 else 512)
    if tn is None:
        tn = min(N, 512) if N % 128 == 0 else N
    assert M % tm == 0
    nlead = len(lead)
    in_specs = [
        pl.BlockSpec((tm, K), lambda i, j: (i, 0)),
        pl.BlockSpec((None,) * nlead + (K, tn), lambda i, j: tuple(lead) + (0, j)),
    ]
    args = [x, w]
    if bias is not None:
        in_specs.append(pl.BlockSpec((1, tn), lambda i, j: (0, j)))
        args.append(bias)
    if res is not None:
        rows_per_gate = M // gate.shape[0]
        assert rows_per_gate % tm == 0
        in_specs.append(pl.BlockSpec((tm, tn), lambda i, j: (i, j)))
        in_specs.append(pl.BlockSpec((1, 1, tn), lambda i, j: ((i * tm) // rows_per_gate, 0, j)))
        args += [res, gate]
    return pl.pallas_call(
        functools.partial(_mm_kernel, has_bias=bias is not None, has_res=res is not None, post=post),
        grid=(M // tm, pl.cdiv(N, tn)),
        in_specs=in_specs,
        out_specs=pl.BlockSpec((tm, tn), lambda i, j: (i, j)),
        out_shape=jax.ShapeDtypeStruct((M, N), out_dtype),
        compiler_params=_params("parallel", "parallel"),
        name=name,
    )(*args)


def _modulated_norm(x, g, sc, sh):
    y = x * lax.rsqrt(jnp.mean(x * x, axis=-1, keepdims=True) + NORM_EPS) * g
    return y * (1.0 + sc) + sh


def _normmod_kernel(x_ref, g_ref, sc_ref, sh_ref, *o_refs):
    h = _modulated_norm(x_ref[0], g_ref[...], sc_ref[0], sh_ref[0])
    for o_ref in o_refs:
        o_ref[0] = h.astype(o_ref.dtype)


def _norm_modulate(x, g, sc, sh, out_dtypes):
    B, T, D = x.shape
    tr = min(T, 512)
    row = pl.BlockSpec((1, tr, D), lambda b, i: (b, i, 0))
    per_batch = pl.BlockSpec((1, 1, D), lambda b, i: (b, 0, 0))
    return pl.pallas_call(
        _normmod_kernel,
        grid=(B, T // tr),
        in_specs=[row, pl.BlockSpec((1, D), lambda b, i: (0, 0)), per_batch, per_batch],
        out_specs=[row] * len(out_dtypes),
        out_shape=[jax.ShapeDtypeStruct((B, T, D), dt) for dt in out_dtypes],
        compiler_params=_params("parallel", "parallel"),
        name="norm_modulate",
    )(x, g.reshape(1, D), sc.reshape(B, 1, D), sh.reshape(B, 1, D))


def _rwkv_prep_kernel(x_ref, halo_ref, g_ref, sc_ref, sh_ref, mu_ref, *o_refs, halo_rows):
    h = _modulated_norm(x_ref[0], g_ref[...], sc_ref[0], sh_ref[0])
    h_halo = _modulated_norm(halo_ref[0], g_ref[...], sc_ref[0], sh_ref[0])
    first = jnp.where(pl.program_id(1) > 0, h_halo[halo_rows - 1:halo_rows, :], 0.0)
    row = lax.broadcasted_iota(jnp.int32, h.shape, 0)
    d = jnp.where(row == 0, first, pltpu.roll(h, 1, axis=0)) - h
    for j, o_ref in enumerate(o_refs):
        o_ref[0] = (h + d * mu_ref[j:j + 1, :]).astype(o_ref.dtype)


def _rwkv_prep(x, g, sc, sh, mu):
    B, T, D = x.shape
    tr = min(T, 512)
    hr = 8
    n_mix = mu.shape[0]
    row = pl.BlockSpec((1, tr, D), lambda b, i: (b, i, 0))
    per_batch = pl.BlockSpec((1, 1, D), lambda b, i: (b, 0, 0))
    return pl.pallas_call(
        functools.partial(_rwkv_prep_kernel, halo_rows=hr),
        grid=(B, T // tr),
        in_specs=[row,
                  pl.BlockSpec((1, hr, D), lambda b, i: (b, jnp.maximum(i * (tr // hr) - 1, 0), 0)),
                  pl.BlockSpec((1, D), lambda b, i: (0, 0)), per_batch, per_batch,
                  pl.BlockSpec((n_mix, D), lambda b, i: (0, 0))],
        out_specs=[row] * n_mix,
        out_shape=[jax.ShapeDtypeStruct((B, T, D), BF16)] * n_mix,
        compiler_params=_params("parallel", "parallel"),
        name="rwkv_shift_mix",
    )(x, x, g.reshape(1, D), sc.reshape(B, 1, D), sh.reshape(B, 1, D), mu)


def _rwkv_kernel(*refs, C, N, HB, SUB):
    h_ref = refs[-1]

    @pl.when(pl.program_id(2) == 0)
    def _():
        h_ref[...] = jnp.zeros_like(h_ref)

    for s in range(SUB):
        _rwkv_chunk(slice(s * C, (s + 1) * C), *refs, C=C, N=N, HB=HB)


def _rwkv_chunk(rows, r_ref, k_ref, v_ref, lw_ref, a_ref, g_ref, kk_ref, ka_ref, rk_ref, lng_ref,
                lnb_ref, o_ref, h_ref, *, C, N, HB):
    row = lax.broadcasted_iota(jnp.int32, (C, C), 0)
    col = lax.broadcasted_iota(jnp.int32, (C, C), 1)
    incl = row >= col
    strict = row > col
    eye = (row == col).astype(F32)
    eye_n = (lax.broadcasted_iota(jnp.int32, (N, N), 0) == lax.broadcasted_iota(jnp.int32, (N, N), 1))
    heads = range(HB)

    def head(x, i):
        return x[:, i * N:(i + 1) * N]

    r_all, k_all, v_all, a_all, lw_all = (ref[0, rows, :] for ref in (r_ref, k_ref, v_ref, a_ref, lw_ref))
    cum_all = _exact_lhs_dot(incl.astype(F32), lw_all)
    tot_all = cum_all[C - 1:C, :]
    mid_all = cum_all[C // 2 - 1:C // 2, :]
    cen_all = cum_all - mid_all
    e_neg = jnp.exp(-cen_all)
    e_tail = jnp.exp(tot_all - cum_all)
    kx_all = k_all * kk_ref[...]
    kp_all = k_all * (1.0 + (a_all - 1.0) * ka_ref[...])
    rt_all = r_all * jnp.exp(cen_all)
    kn_all = kp_all * e_neg
    kt_all = kp_all * e_tail
    ep_all = jnp.exp(cen_all - lw_all)
    rkr_all = r_all * kp_all * rk_ref[...]

    def column(row_vec):
        return jnp.sum(jnp.where(eye_n, row_vec, 0.0), axis=1, keepdims=True)

    kappa = [head(kx_all, i) / jnp.maximum(
        jnp.sqrt(jnp.sum(jnp.square(head(kx_all, i)), axis=-1, keepdims=True)), 1e-12) for i in heads]
    b = [kappa[i] * head(a_all, i) for i in heads]
    kap_t = [kappa[i] * head(ep_all, i) for i in heads]
    r_t = [head(rt_all, i) for i in heads]
    v = [head(v_all, i) for i in heads]
    big = [_bdot_nt(jnp.concatenate([kap_t[i], r_t[i]], axis=0),
                    jnp.concatenate([b[i] * head(e_neg, i), head(kn_all, i)], axis=0))
           for i in heads]
    t_k = [jnp.where(strict, big[i][:C, C:], 0.0) for i in heads]
    m_b = [jnp.where(incl, big[i][C:, :C], 0.0) for i in heads]
    m_k = [jnp.where(incl, big[i][C:, C:], 0.0) for i in heads]
    t_b = [jnp.where(strict, big[i][:C, :C], 0.0) for i in heads]
    base = RW_INV_BASE
    p = [jnp.where((row // base) == (col // base), -t_b[i], 0.0) for i in heads]
    inv = [eye + p[i] for i in heads]
    tkv = [_bdot(t_k[i], v[i]) for i in heads]
    for _ in range(int(math.log2(base)) - 1):
        p = [_bdot(p[i], p[i]) for i in heads]
        inv = [inv[i] + _bdot(inv[i], p[i]) for i in heads]
    size = base
    while size < C:
        couple = ((row // (2 * size)) == (col // (2 * size))) & ((row // size) != (col // size))
        lower = [jnp.where(couple, t_b[i], 0.0) for i in heads]
        inv = [inv[i] - _bdot(inv[i], _bdot(lower[i], inv[i])) for i in heads]
        size *= 2
    h0 = [h_ref[i] for i in heads]
    h0c = [h0[i] * jnp.exp(column(head(mid_all, i))) for i in heads]
    aw = [_bdot(inv[i], jnp.concatenate([kap_t[i], tkv[i]], axis=1)) for i in heads]
    u = [_bdot(aw[i][:, :N], h0c[i]) + aw[i][:, N:] for i in heads]
    y = [_bdot(jnp.concatenate([r_t[i], m_k[i], -m_b[i]], axis=1),
               jnp.concatenate([h0c[i], v[i], u[i]], axis=0)) for i in heads]
    for i in heads:
        h_ref[i] = jnp.exp(column(head(tot_all, i))) * h0[i] + _bdot_tn(
            jnp.concatenate([head(kt_all, i), -(b[i] * head(e_tail, i))], axis=0),
            jnp.concatenate([v[i], u[i]], axis=0))
    outs = []
    for i in heads:
        mu = jnp.mean(y[i], axis=-1, keepdims=True)
        var = jnp.mean(jnp.square(y[i] - mu), axis=-1, keepdims=True)
        yn = (y[i] - mu) * lax.rsqrt(var + RW_GN_EPS)
        outs.append(yn * head(lng_ref[...], i) + head(lnb_ref[...], i)
                    + jnp.sum(head(rkr_all, i), axis=-1, keepdims=True) * v[i])
    o_ref[0, rows, :] = (jnp.concatenate(outs, axis=-1) * g_ref[0, rows, :]).astype(o_ref.dtype)


def _rwkv_core(r, k, v, lw, a, g, k_k, k_a, r_k, ln_g, ln_b):
    B, T, D = r.shape
    C, N, HB, SUB = RW_CHUNK, RW_HEAD_DIM, RW_HEADS_PER_STEP, RW_CHUNKS_PER_STEP
    W = HB * N
    seq = pl.BlockSpec((1, SUB * C, W), lambda b, h, c: (b, c, h))
    par = pl.BlockSpec((1, W), lambda b, h, c: (0, h))
    return pl.pallas_call(
        functools.partial(_rwkv_kernel, C=C, N=N, HB=HB, SUB=SUB),
        grid=(B, D // W, T // (SUB * C)),
        in_specs=[seq] * 6 + [par] * 5,
        out_specs=seq,
        out_shape=jax.ShapeDtypeStruct((B, T, D), BF16),
        scratch_shapes=[pltpu.VMEM((HB, N, N), F32)],
        compiler_params=_params("parallel", "parallel", "arbitrary"),
        name="rwkv_core",
    )(r, k, v, lw, a, g, k_k.reshape(1, D), k_a.reshape(1, D), r_k.reshape(1, D),
      ln_g.reshape(1, D), ln_b.reshape(1, D))


def _rwkv7_mix(x_res, norm_g, sc, shift, gate, mu, w_rkv, w0, w1, w2, a0, a1, a2, g1, g2, k_k, k_a,
               r_k, ln_g, ln_b, w_o):
    B, T, D = x_res.shape
    M = B * T
    xr, xk, xv, xw, xa, xg = [z.reshape(M, D) for z in _rwkv_prep(x_res, norm_g, sc, shift, mu)]
    r = _matmul(xr, w_rkv, (0, 0), name="rwkv_r")
    k = _matmul(xk, w_rkv, (0, 1), name="rwkv_k")
    v = _matmul(xv, w_rkv, (0, 2), name="rwkv_v")
    log_decay = _matmul(_matmul(xw, w1, (0,), post="tanh", out_dtype=BF16), w2, (0,),
                        bias=w0.reshape(1, D), post="rwkv_log_decay", name="rwkv_decay")
    a = _matmul(_matmul(xa, a1, (0,), out_dtype=BF16), a2, (0,), bias=a0.reshape(1, D),
                post="sigmoid", name="rwkv_a")
    g = _matmul(_matmul(xg, g1, (0,), post="sigmoid", out_dtype=BF16), g2, (0,), name="rwkv_g")
    sh = (B, T, D)
    o = _rwkv_core(r.reshape(sh), k.reshape(sh), v.reshape(sh), log_decay.reshape(sh),
                   a.reshape(sh), g.reshape(sh), k_k, k_a, r_k, ln_g, ln_b)
    return _matmul(o.reshape(M, D), w_o, (0,), res=x_res.reshape(M, D), gate=gate).reshape(sh)


def _ret_kernel(q_ref, k_ref, v_ref, g_ref, cos_ref, sin_ref, dm_ref, xz_ref, gng_ref, gnb_ref,
                o_ref, s_ref, *, dk):
    @pl.when(pl.program_id(2) == 0)
    def _():
        s_ref[...] = jnp.zeros_like(s_ref)

    cos, sin = cos_ref[...], sin_ref[...]
    half = dk // 2

    def rot(z):
        z1, z2 = z[:, :half], z[:, half:]
        return jnp.concatenate([z1 * cos - z2 * sin, z1 * sin + z2 * cos], axis=-1)

    q = rot(q_ref[0])
    k = rot(k_ref[0] * (dk ** -0.5))
    v = v_ref[0]
    xi, zeta, g_chunk = xz_ref[0, :, 0:1], xz_ref[0, :, 1:2], xz_ref[0, 0:1, 2:3]
    scores = _bdot_nt(q, k) * dm_ref[0]
    state = s_ref[...]
    o = _bdot(scores, v) + _bdot(q * xi, state)
    s_ref[...] = g_chunk * state + _bdot_tn(k * zeta, v)
    mu = jnp.mean(o, axis=-1, keepdims=True)
    var = jnp.mean(jnp.square(o - mu), axis=-1, keepdims=True)
    y = (o - mu) * lax.rsqrt(var + NORM_EPS) * gng_ref[0] + gnb_ref[0]
    gate = g_ref[0]
    o_ref[0] = (y * (gate * jax.nn.sigmoid(gate))).astype(o_ref.dtype)


def _retention_mix(h, x_res, gate, w_in, gn_g, gn_b, w_o):
    B, T, D = h.shape
    M = B * T
    H, dk, dv, C = RET_HEADS, RET_DK, RET_DV, RET_CHUNK
    proj = _matmul(h.reshape(M, D), w_in, (0,)).reshape(B, T, H * (2 * dk + 2 * dv))
    qb, kb, vb, gb = 0, (H * dk) // dk, (2 * H * dk) // dv, (2 * H * dk + H * dv) // dv
    theta = 1.0 / (RET_ROT_BASE ** jnp.linspace(0.0, 1.0, dk // 2, dtype=F32))
    ang = jnp.arange(T, dtype=F32)[:, None] * theta[None, :]
    log_gamma = jnp.log(1.0 - 2.0 ** (-5.0 - jnp.arange(H, dtype=F32)))
    pos = jnp.arange(C, dtype=F32)
    rel = pos[:, None] - pos[None, :]
    dmask = jnp.where(rel >= 0, jnp.exp(jnp.maximum(rel, 0.0) * log_gamma[:, None, None]), 0.0)
    xi = jnp.exp((pos + 1.0)[None, :] * log_gamma[:, None])
    zeta = jnp.exp((C - 1.0 - pos)[None, :] * log_gamma[:, None])
    g_chunk = jnp.broadcast_to(jnp.exp(C * log_gamma)[:, None], (H, C))
    xz = jnp.concatenate([jnp.stack([xi, zeta, g_chunk], axis=-1), jnp.zeros((H, C, 125), F32)], axis=-1)
    o = pl.pallas_call(
        functools.partial(_ret_kernel, dk=dk),
        grid=(B, H, T // C),
        in_specs=[
            pl.BlockSpec((1, C, dk), lambda b, h, c: (b, c, qb + h)),
            pl.BlockSpec((1, C, dk), lambda b, h, c: (b, c, kb + h)),
            pl.BlockSpec((1, C, dv), lambda b, h, c: (b, c, vb + h)),
            pl.BlockSpec((1, C, dv), lambda b, h, c: (b, c, gb + h)),
            pl.BlockSpec((C, dk // 2), lambda b, h, c: (c, 0)),
            pl.BlockSpec((C, dk // 2), lambda b, h, c: (c, 0)),
            pl.BlockSpec((1, C, C), lambda b, h, c: (h, 0, 0)),
            pl.BlockSpec((1, C, 128), lambda b, h, c: (h, 0, 0)),
            pl.BlockSpec((1, 1, dv), lambda b, h, c: (h, 0, 0)),
            pl.BlockSpec((1, 1, dv), lambda b, h, c: (h, 0, 0)),
        ],
        out_specs=pl.BlockSpec((1, C, dv), lambda b, h, c: (b, c, h)),
        out_shape=jax.ShapeDtypeStruct((B, T, H * dv), BF16),
        scratch_shapes=[pltpu.VMEM((dk, dv), F32)],
        compiler_params=_params("parallel", "parallel", "arbitrary"),
        name="retention_core",
    )(proj, proj, proj, proj, jnp.cos(ang), jnp.sin(ang), dmask, xz,
      gn_g.reshape(H, 1, dv), gn_b.reshape(H, 1, dv))
    return _matmul(o.reshape(M, H * dv), w_o, (0,), res=x_res.reshape(M, D), gate=gate).reshape(B, T, D)


def _gla_kernel(q_ref, k_ref, v_ref, g_ref, la_ref, gn_ref, o_ref, s_ref, *, C, dk):
    @pl.when(pl.program_id(2) == 0)
    def _():
        s_ref[...] = jnp.zeros_like(s_ref)

    row = lax.broadcasted_iota(jnp.int32, (C, C), 0)
    col = lax.broadcasted_iota(jnp.int32, (C, C), 1)
    causal = row >= col
    la = la_ref[0]
    b = _exact_lhs_dot(causal.astype(F32), la)
    b_last = b[C - 1:C, :]
    eye_k = (lax.broadcasted_iota(jnp.int32, (dk, dk), 0) == lax.broadcasted_iota(jnp.int32, (dk, dk), 1))
    d_last_col = jnp.exp(jnp.sum(jnp.where(eye_k, b_last, 0.0), axis=1, keepdims=True))
    k, v = k_ref[0], v_ref[0]
    q_in = q_ref[0] * (dk ** -0.5) * jnp.exp(b)
    k_in = k * jnp.exp(-b)
    att = jnp.where(causal, _bdot_nt(q_in, k_in), 0.0)
    state = s_ref[...]
    o = _bdot(att, v) + _bdot(q_in, state)
    s_ref[...] = d_last_col * state + _bdot_tn(k * jnp.exp(b_last - b), v)
    y = o * lax.rsqrt(jnp.mean(o * o, axis=-1, keepdims=True) + NORM_EPS) * gn_ref[...]
    gate = g_ref[0]
    o_ref[0] = (y * (gate * jax.nn.sigmoid(gate))).astype(o_ref.dtype)


def _gla_mix(h, x_res, gate, w_in, w_a1, w_a2, b_a, gn_g, w_o):
    B, T, D = h.shape
    M = B * T
    H, dk, dv, C = GLA_HEADS, GLA_DK, GLA_DV, GLA_CHUNK
    h2 = h.reshape(M, D)
    proj = _matmul(h2, w_in, (0,)).reshape(B, T, H * (2 * dk + 2 * dv))
    log_a = _matmul(_matmul(h2, w_a1, (0,), out_dtype=BF16), w_a2, (0,), bias=b_a.reshape(1, H * dk),
                    post="gla_log_gate", name="gla_log_gate").reshape(B, T, H * dk)
    qb, kb, vb, gb = 0, H, (2 * H * dk) // dv, (2 * H * dk + H * dv) // dv
    o = pl.pallas_call(
        functools.partial(_gla_kernel, C=C, dk=dk),
        grid=(B, H, T // C),
        in_specs=[
            pl.BlockSpec((1, C, dk), lambda b, h, c: (b, c, qb + h)),
            pl.BlockSpec((1, C, dk), lambda b, h, c: (b, c, kb + h)),
            pl.BlockSpec((1, C, dv), lambda b, h, c: (b, c, vb + h)),
            pl.BlockSpec((1, C, dv), lambda b, h, c: (b, c, gb + h)),
            pl.BlockSpec((1, C, dk), lambda b, h, c: (b, c, h)),
            pl.BlockSpec((1, dv), lambda b, h, c: (0, 0)),
        ],
        out_specs=pl.BlockSpec((1, C, dv), lambda b, h, c: (b, c, h)),
        out_shape=jax.ShapeDtypeStruct((B, T, H * dv), BF16),
        scratch_shapes=[pltpu.VMEM((dk, dv), F32)],
        compiler_params=_params("parallel", "parallel", "arbitrary"),
        name="gla_core",
    )(proj, proj, proj, proj, log_a, gn_g.reshape(1, dv))
    return _matmul(o.reshape(M, H * dv), w_o, (0,), res=x_res.reshape(M, D), gate=gate).reshape(B, T, D)


def _nsa_cmp_kernel(q_ref, kc_ref, vc_ref, o_ref, sel_ref, *, tq, n_cmp, n_pad, n_slc, hpg):
    L, S, Ls, hd = NSA_CMP_BLK, NSA_CMP_STRIDE, NSA_SEL_BLK, NSA_HD
    t_pos = pl.program_id(2) * tq + lax.broadcasted_iota(jnp.int32, (tq, 1), 0)
    n_ix = lax.broadcasted_iota(jnp.int32, (1, n_pad), 1)
    valid = (n_ix * S + (L - 1) <= t_pos) & (n_ix < n_cmp)
    validf = valid.astype(F32)
    c_start = lax.broadcasted_iota(jnp.int32, (n_pad, n_slc), 0) * S
    s_start = lax.broadcasted_iota(jnp.int32, (n_pad, n_slc), 1) * Ls
    overlap = ((c_start <= s_start + (Ls - 1)) & (c_start + (L - 1) >= s_start)
               & (c_start < n_cmp * S)).astype(F32)
    kc, vc = kc_ref[0, 0], vc_ref[0, 0]
    kc_hi = kc.astype(BF16)
    kc_lo = (kc - kc_hi.astype(F32)).astype(BF16)
    heads = range(hpg)
    q = [q_ref[0, :, hh * hd:(hh + 1) * hd] for hh in heads]
    q_hi = [q[hh].astype(BF16) for hh in heads]
    q_lo = [(q[hh] - q_hi[hh].astype(F32)).astype(BF16) for hh in heads]
    s = [_bdot_nt(q_hi[hh], kc_hi) + _bdot_nt(q_hi[hh], kc_lo) + _bdot_nt(q_lo[hh], kc_hi) for hh in heads]
    s = [jnp.where(valid, s[hh], NEG_INF) for hh in heads]
    e = [jnp.exp(s[hh] - jnp.max(s[hh], axis=-1, keepdims=True)) for hh in heads]
    p = [e[hh] / jnp.sum(e[hh], axis=-1, keepdims=True) * validf for hh in heads]
    o_ref[0] = jnp.concatenate([_bdot(p[hh], vc) for hh in heads], axis=-1)
    imp = _exact_rhs_dot(sum(p), overlap)

    imp = imp.T
    j = lax.broadcasted_iota(jnp.int32, (n_slc, 1), 0)
    cur = (pl.program_id(2) * tq + lax.broadcasted_iota(jnp.int32, (1, tq), 1)) // Ls
    forced = (j == 0) | (j == cur) | (j == cur - 1)
    imp = jnp.where(j > cur, -IMP_BIG, jnp.where(forced, IMP_BIG, imp))
    rank = jnp.zeros((n_slc, tq), jnp.int32)
    for jp in range(n_slc):
        c = imp[jp:jp + 1, :]
        rank = rank + ((c > imp) | ((c == imp) & (jp < j))).astype(jnp.int32)
    sel_ref[0, 0] = (rank < min(NSA_SEL_TOPK, n_slc)).astype(F32)


def _nsa_att_kernel(qt_ref, ks_ref, vst_ref, kw_ref, vwt_ref, sel_ref, oc_ref, gt_ref, gtt_ref, o_ref,
                    m_ref, l_ref, acc_ref, *, tq, tk, T, n_slc, hpg):
    Ls, W, hd = NSA_SEL_BLK, NSA_WINDOW, NSA_HD
    qi = pl.program_id(2)
    t0 = qi * tq
    heads = range(hpg)
    cols = [slice(hh * tq, (hh + 1) * tq) for hh in heads]
    qt = jnp.concatenate([qt_ref[0, hh * hd:(hh + 1) * hd, :] for hh in heads], axis=1)
    t_q = t0 + lax.broadcasted_iota(jnp.int32, (1, tq), 1)
    sel_t = sel_ref[0, 0]

    m_ref[...] = jnp.full_like(m_ref, NEG_INF)
    l_ref[...] = jnp.zeros_like(l_ref)
    acc_ref[...] = jnp.zeros_like(acc_ref)

    pad = ks_ref.shape[2] - hd - n_slc
    sel_bias = jnp.where(sel_t > 0.5, 0.0, NEG_INF).astype(BF16)
    q_aug = jnp.concatenate([qt, jnp.concatenate([sel_bias] * hpg, axis=1),
                             jnp.zeros((pad, hpg * tq), BF16)], axis=0)

    def key_tile(kb, causal):
        k0 = pl.multiple_of(kb * tk, tk)
        vt_t = vst_ref[0, :, pl.ds(k0, tk)]
        s_all = jnp.dot(ks_ref[0, pl.ds(k0, tk), :], q_aug, preferred_element_type=F32)
        kpos = k0 + lax.broadcasted_iota(jnp.int32, (tk, 1), 0)
        ps = []
        for hh in heads:
            s = s_all[:, cols[hh]]
            if causal:
                s = jnp.where(kpos <= t_q, s, NEG_INF)
            m_old = m_ref[:, cols[hh]]
            m_new = jnp.maximum(m_old, jnp.max(s, axis=0, keepdims=True))
            alpha = jnp.exp2(m_old - m_new)
            p = jnp.exp2(s - m_new)
            l_ref[:, cols[hh]] = alpha * l_ref[:, cols[hh]] + jnp.sum(p, axis=0, keepdims=True)
            acc_ref[:, cols[hh]] = alpha * acc_ref[:, cols[hh]]
            m_ref[:, cols[hh]] = m_new
            ps.append(p.astype(BF16))
        acc_ref[...] += jnp.dot(vt_t, jnp.concatenate(ps, axis=1), preferred_element_type=F32)

    n_full = t0 // tk

    def full_tile(kb, carry):
        key_tile(kb, causal=False)
        return carry

    lax.fori_loop(0, n_full, full_tile, 0)
    key_tile(n_full, causal=True)
    o_sel_t = acc_ref[...] / l_ref[...]

    span = W + tq
    w0 = pl.multiple_of(jnp.maximum(t0 - W, 0), tq)
    kw = kw_ref[0, pl.ds(w0, span), :]
    vw_t = vwt_ref[0, :, pl.ds(w0, span)]
    wpos = w0 + lax.broadcasted_iota(jnp.int32, (span, 1), 0)
    bias_w = jnp.where((wpos <= t_q) & (wpos > t_q - W), 0.0, NEG_INF)
    s_all = jnp.dot(kw, qt, preferred_element_type=F32)
    es, ls = [], []
    for hh in heads:
        s = s_all[:, cols[hh]] + bias_w
        e = jnp.exp2(s - jnp.max(s, axis=0, keepdims=True))
        ls.append(jnp.sum(e, axis=0, keepdims=True))
        es.append(e.astype(BF16))
    o_win_t = (jnp.dot(vw_t, jnp.concatenate(es, axis=1), preferred_element_type=F32)
               / jnp.concatenate(ls, axis=1))

    gt = jax.nn.sigmoid(gt_ref[0, 0])
    gt_t = jax.nn.sigmoid(gtt_ref[0, 0])
    outs = []
    for hh in heads:
        mixed_t = (gt_t[hpg + hh:hpg + hh + 1, :] * o_sel_t[:, cols[hh]]
                   + gt_t[2 * hpg + hh:2 * hpg + hh + 1, :] * o_win_t[:, cols[hh]])
        outs.append(gt[:, hh:hh + 1] * oc_ref[0, :, hh * hd:(hh + 1) * hd] + mixed_t.T)
    o_ref[0] = jnp.concatenate(outs, axis=-1).astype(o_ref.dtype)


def _rms_norm(x, g):
    return x * lax.rsqrt(jnp.mean(x * x, axis=-1, keepdims=True) + NORM_EPS) * g


def _nsa_mix(h, x_res, gate, w_in, q_g, k_g, cmp_pos, cmp_w1, cmp_w2, w_o):
    B, T, D = h.shape
    M = B * T
    H, G, hd = NSA_HEADS, NSA_KV_HEADS, NSA_HD
    hpg = H // G
    L, S, Ls = NSA_CMP_BLK, NSA_CMP_STRIDE, NSA_SEL_BLK
    kvw = G * hd
    n_main = H * hd + 6 * kvw
    h2 = h.reshape(M, D)
    proj = _matmul(h2, w_in, (0,), n_cols=n_main)
    gate_w = jnp.pad(w_in[0, :, n_main:], ((0, 0), (0, 128 - 3 * H)))
    gates = _matmul(h2, gate_w)[:, :3 * H]
    q = _rms_norm(proj[:, :H * hd].reshape(B, T, H, hd), q_g) * (hd ** -0.5)
    q = q.reshape(B, T, H * hd)

    def kv(i):
        return proj[:, H * hd + i * kvw:H * hd + (i + 1) * kvw].reshape(B, T, G, hd)

    n_cmp = (T - L) // S + 1
    n_grp = T // S
    assert L == 2 * S and n_cmp == n_grp - 1

    def compress(z, pos, w1, w2):
        zg = z.reshape(B, n_grp, S, G, hd).transpose(0, 3, 1, 2, 4).reshape(B * G * n_grp, S * hd)
        w_halves = jnp.concatenate([w1[:S * hd], w1[S * hd:]], axis=1)
        part = _matmul(zg, w_halves, tm=min(512, B * G * n_grp)).reshape(B, G, n_grp, 2 * hd)
        pos_term = _matmul(jnp.broadcast_to(pos.reshape(1, L * hd), (8, L * hd)), w1, tm=8)[0]
        pre = part[:, :, :-1, :hd] + part[:, :, 1:, hd:] + pos_term
        pre = jnp.pad(pre, ((0, 0), (0, 0), (0, 1), (0, 0))).reshape(B * G * n_grp, hd)
        return _matmul(jax.nn.silu(pre), w2, tm=min(512, B * G * n_grp)).reshape(B, G, n_grp, hd)

    kc = _rms_norm(compress(kv(0), cmp_pos[0], cmp_w1[0], cmp_w2[0]), k_g[0])
    vc = compress(kv(1), cmp_pos[1], cmp_w1[1], cmp_w2[1])
    k_feat = 2 * hd
    blk_one_hot = (jnp.arange(T)[:, None] // Ls == jnp.arange(k_feat - hd)[None, :]).astype(BF16)
    k_sel = jnp.concatenate([_rms_norm(kv(2), k_g[1]).astype(BF16),
                             jnp.broadcast_to(blk_one_hot[None, :, None, :], (B, T, G, k_feat - hd))],
                            axis=-1).reshape(B, T, G * k_feat)
    v_sel = kv(3).reshape(B, T, kvw).astype(BF16)
    k_win = _rms_norm(kv(4), k_g[2]).reshape(B, T, kvw).astype(BF16)
    v_win = kv(5).reshape(B, T, kvw).astype(BF16)

    n_slc = T // Ls
    tq = min(NSA_CMP_TQ, T)
    o_cmp, sel = pl.pallas_call(
        functools.partial(_nsa_cmp_kernel, tq=tq, n_cmp=n_cmp, n_pad=n_grp, n_slc=n_slc, hpg=hpg),
        grid=(B, G, T // tq),
        in_specs=[
            pl.BlockSpec((1, tq, hpg * hd), lambda b, g, i: (b, i, g)),
            pl.BlockSpec((1, 1, n_grp, hd), lambda b, g, i: (b, g, 0, 0)),
            pl.BlockSpec((1, 1, n_grp, hd), lambda b, g, i: (b, g, 0, 0)),
        ],
        out_specs=[
            pl.BlockSpec((1, tq, hpg * hd), lambda b, g, i: (b, i, g)),
            pl.BlockSpec((1, 1, n_slc, tq), lambda b, g, i: (b, g, 0, i)),
        ],
        out_shape=[jax.ShapeDtypeStruct((B, T, H * hd), F32),
                   jax.ShapeDtypeStruct((B, G, n_slc, T), F32)],
        compiler_params=_params("parallel", "parallel", "parallel"),
        name="nsa_compressed",
    )(q, kc, vc)

    gt = gates.reshape(B, T, 3, G, hpg).transpose(0, 3, 1, 2, 4).reshape(B, G, T, 3 * hpg)
    tq = min(NSA_ATT_TQ, T)
    tk = min(NSA_ATT_TK, T)
    assert T >= NSA_WINDOW + tq
    k_spec = pl.BlockSpec((1, T, hd), lambda b, g, i: (b, 0, g))
    vt_spec = pl.BlockSpec((1, hd, T), lambda b, g, i: (b, g, 0))

    def feature_major(z):
        return jnp.swapaxes(z, 1, 2)

    o = pl.pallas_call(
        functools.partial(_nsa_att_kernel, tq=tq, tk=tk, T=T, n_slc=n_slc, hpg=hpg),
        grid=(B, G, T // tq),
        in_specs=[
            pl.BlockSpec((1, hpg * hd, tq), lambda b, g, i: (b, g, i)),
            pl.BlockSpec((1, T, k_feat), lambda b, g, i: (b, 0, g)), vt_spec, k_spec, vt_spec,
            pl.BlockSpec((1, 1, n_slc, tq), lambda b, g, i: (b, g, 0, i)),
            pl.BlockSpec((1, tq, hpg * hd), lambda b, g, i: (b, i, g)),
            pl.BlockSpec((1, 1, tq, 3 * hpg), lambda b, g, i: (b, g, i, 0)),
            pl.BlockSpec((1, 1, 3 * hpg, tq), lambda b, g, i: (b, g, 0, i)),
        ],
        out_specs=pl.BlockSpec((1, tq, hpg * hd), lambda b, g, i: (b, i, g)),
        out_shape=jax.ShapeDtypeStruct((B, T, H * hd), BF16),
        scratch_shapes=[pltpu.VMEM((1, hpg * tq), F32), pltpu.VMEM((1, hpg * tq), F32),
                        pltpu.VMEM((hd, hpg * tq), F32)],
        compiler_params=_params("parallel", "parallel", "arbitrary"),
        name="nsa_selected_window",
    )(feature_major((q * math.log2(math.e)).astype(BF16)), k_sel, feature_major(v_sel), k_win, feature_major(v_win), sel,
      o_cmp, gt, jnp.swapaxes(gt, 2, 3))
    return _matmul(o.reshape(M, H * hd), w_o, (0,), res=x_res.reshape(M, D), gate=gate).reshape(B, T, D)


def _clamped_swiglu(gu):
    x_glu = jnp.minimum(gu[:, :D_EXPERT], SWIGLU_LIMIT)
    x_lin = jnp.clip(gu[:, D_EXPERT:], -SWIGLU_LIMIT, SWIGLU_LIMIT)
    return x_glu * jax.nn.sigmoid(SWIGLU_ALPHA * x_glu) * (x_lin + 1.0)


def _expert_kernel(blk_e_ref, first_ref, next_ref, used_ref, x_ref, w_hbm, b_ref, o_ref, stage, w_bf16,
                   sem, *, layer, post):
    i = pl.program_id(0)

    def fetch(expert):
        return pltpu.make_async_copy(w_hbm.at[layer, expert], stage, sem.at[0])

    @pl.when(i == 0)
    def _():
        fetch(blk_e_ref[0]).start()

    @pl.when(first_ref[i] == 1)
    def _():
        fetch(blk_e_ref[i]).wait()
        w_bf16[...] = stage[...].astype(BF16)

        @pl.when(next_ref[i] >= 0)
        def _():
            fetch(next_ref[i]).start()

    @pl.when(used_ref[i] == 1)
    def _():
        acc = jnp.dot(x_ref[...], w_bf16[...], preferred_element_type=F32) + b_ref[0]
        o_ref[...] = (acc if post is None else post(acc)).astype(o_ref.dtype)

    @pl.when(used_ref[i] == 0)
    def _():
        o_ref[...] = jnp.zeros_like(o_ref)


def _expert_matmul(tables, x, w, b, layer, post, n_out, out_dtype, name):
    cap, Kd = x.shape
    E, Nd = w.shape[1], w.shape[3]
    R = MOE_ROWS
    return pl.pallas_call(
        functools.partial(_expert_kernel, layer=layer, post=post),
        grid_spec=pltpu.PrefetchScalarGridSpec(
            num_scalar_prefetch=4,
            grid=(cap // R,),
            in_specs=[
                pl.BlockSpec((R, Kd), lambda i, be, fi, nx, us: (i, 0)),
                pl.BlockSpec(memory_space=pl.ANY),
                pl.BlockSpec((None, 1, 1, Nd), lambda i, be, fi, nx, us: (layer, be[i], 0, 0)),
            ],
            out_specs=pl.BlockSpec((R, n_out), lambda i, be, fi, nx, us: (i, 0)),
            scratch_shapes=[pltpu.VMEM((Kd, Nd), F32), pltpu.VMEM((Kd, Nd), BF16),
                            pltpu.SemaphoreType.DMA((1,))],
        ),
        out_shape=jax.ShapeDtypeStruct((cap, n_out), out_dtype),
        compiler_params=_params("arbitrary"),
        name=name,
    )(*tables, x, w, b.reshape(b.shape[0], E, 1, Nd))


def _dispatch_kernel(tok_ref, used_ref, h_hbm, o_ref, buf, sem, *, R):
    i = pl.program_id(0)
    n_steps = pl.num_programs(0)

    def fetch(step, par):
        def one_row(r, carry):
            pltpu.make_async_copy(h_hbm.at[pl.ds(tok_ref[step * R + r], 1)], buf.at[par, pl.ds(r, 1)],
                                  sem.at[par]).start()
            return carry
        lax.fori_loop(0, R, one_row, 0, unroll=8)

    @pl.when((i == 0) & (used_ref[0] == 1))
    def _():
        fetch(0, 0)

    nxt = jnp.minimum(i + 1, n_steps - 1)

    @pl.when((i + 1 < n_steps) & (used_ref[nxt] == 1))
    def _():
        fetch(nxt, nxt % 2)

    @pl.when(used_ref[i] == 1)
    def _():
        par = i % 2
        pltpu.make_async_copy(h_hbm.at[pl.ds(0, R)], buf.at[par], sem.at[par]).wait()
        o_ref[...] = buf[par].astype(o_ref.dtype)

    @pl.when(used_ref[i] == 0)
    def _():
        o_ref[...] = jnp.zeros_like(o_ref)


def _moe_dispatch(h, tok, used):
    N, D = h.shape
    R = MOE_ROWS
    cap = tok.shape[0]
    return pl.pallas_call(
        functools.partial(_dispatch_kernel, R=R),
        grid_spec=pltpu.PrefetchScalarGridSpec(
            num_scalar_prefetch=2,
            grid=(cap // R,),
            in_specs=[pl.BlockSpec(memory_space=pl.ANY)],
            out_specs=pl.BlockSpec((R, D), lambda i, t, u: (i, 0)),
            scratch_shapes=[pltpu.VMEM((2, R, D), F32), pltpu.SemaphoreType.DMA((2,))],
        ),
        out_shape=jax.ShapeDtypeStruct((cap, D), BF16),
        compiler_params=_params("arbitrary"),
        name="moe_dispatch",
    )(tok, used, h)


def _router_kernel(x_ref, w_ref, b_ref, et_ref, p_ref, cnt_ref):
    logits = _fdot(x_ref[...], w_ref[...]) + b_ref[...]
    lane = lax.broadcasted_iota(jnp.int32, logits.shape, 1)
    vals, idxs = [], []
    for _ in range(TOP_K):
        m = jnp.max(logits, axis=-1, keepdims=True)
        idx = jnp.min(jnp.where(logits == m, lane, logits.shape[1]), axis=-1, keepdims=True)
        vals.append(m)
        idxs.append(idx)
        logits = jnp.where(lane == idx, -IMP_BIG, logits)
    es = [jnp.exp(v - vals[0]) for v in vals]
    total = sum(es)
    e_out = jnp.zeros(logits.shape, jnp.int32)
    p_out = jnp.zeros(logits.shape, F32)
    picks = jnp.zeros(logits.shape, jnp.int32)
    for k in range(TOP_K):
        e_out = jnp.where(lane == k, idxs[k], e_out)
        p_out = jnp.where(lane == k, es[k] / total, p_out)
        picks = picks + (lane == idxs[k]).astype(jnp.int32)
    et_ref[...] = e_out.T[:ROUTER_ROWS]
    p_ref[...] = p_out
    cnt_ref[0] = jnp.sum(picks, axis=0, keepdims=True)


def _route(x, w_router, b_router, layer):
    N, D = x.shape
    E, K = N_EXPERTS, TOP_K
    lanes = 128
    w_r = jnp.pad(w_router[layer], ((0, 0), (0, lanes - E)))
    b_r = jnp.concatenate([b_router[layer], jnp.full((lanes - E,), NEG_INF, F32)]).reshape(1, lanes)
    tm = min(N, 512)
    return pl.pallas_call(
        _router_kernel,
        grid=(N // tm,),
        in_specs=[pl.BlockSpec((tm, D), lambda i: (i, 0)),
                  pl.BlockSpec((D, lanes), lambda i: (0, 0)),
                  pl.BlockSpec((1, lanes), lambda i: (0, 0))],
        out_specs=[pl.BlockSpec((ROUTER_ROWS, tm), lambda i: (0, i)),
                   pl.BlockSpec((tm, lanes), lambda i: (i, 0)),
                   pl.BlockSpec((1, 1, lanes), lambda i: (i, 0, 0))],
        out_shape=[jax.ShapeDtypeStruct((ROUTER_ROWS, N), jnp.int32),
                   jax.ShapeDtypeStruct((N, lanes), F32),
                   jax.ShapeDtypeStruct((N // tm, 1, lanes), jnp.int32)],
        compiler_params=_params("parallel"),
        name="moe_router",
    )(x, w_r, b_r)


def _combine_kernel(slot_ref, out_hbm, p_ref, res_ref, gate_ref, o_ref, buf, sem, *, TT, K, N):
    i = pl.program_id(0)
    n_steps = pl.num_programs(0)

    def fetch(step, par):
        for k in range(K):
            def one_row(r, carry):
                row = slot_ref[k * N + step * TT + r]
                pltpu.make_async_copy(out_hbm.at[pl.ds(row, 1)], buf.at[par, pl.ds(k * TT + r, 1)],
                                      sem.at[par]).start()
                return carry
            lax.fori_loop(0, TT, one_row, 0, unroll=8)

    @pl.when(i == 0)
    def _():
        fetch(0, 0)

    @pl.when(i + 1 < n_steps)
    def _():
        fetch(i + 1, (i + 1) % 2)

    par = i % 2
    pltpu.make_async_copy(out_hbm.at[pl.ds(0, K * TT)], buf.at[par], sem.at[par]).wait()
    p = p_ref[...]
    y = p[:, 0:1] * buf[par, 0:TT, :]
    for k in range(1, K):
        y = y + p[:, k:k + 1] * buf[par, k * TT:(k + 1) * TT, :]
    o_ref[...] = res_ref[...] + gate_ref[0] * y


def _moe_combine(out, slot, top_w, x_res, gate, T):
    N, D = x_res.shape
    K, TT = TOP_K, MOE_COMBINE_TOKENS
    return pl.pallas_call(
        functools.partial(_combine_kernel, TT=TT, K=K, N=N),
        grid_spec=pltpu.PrefetchScalarGridSpec(
            num_scalar_prefetch=1,
            grid=(N // TT,),
            in_specs=[
                pl.BlockSpec(memory_space=pl.ANY),
                pl.BlockSpec((TT, top_w.shape[1]), lambda i, s: (i, 0)),
                pl.BlockSpec((TT, D), lambda i, s: (i, 0)),
                pl.BlockSpec((1, 1, D), lambda i, s: ((i * TT) // T, 0, 0)),
            ],
            out_specs=pl.BlockSpec((TT, D), lambda i, s: (i, 0)),
            scratch_shapes=[pltpu.VMEM((2, K * TT, D), F32), pltpu.SemaphoreType.DMA((2,))],
        ),
        out_shape=jax.ShapeDtypeStruct((N, D), F32),
        compiler_params=_params("arbitrary"),
        name="moe_combine",
    )(slot, out, top_w, x_res, gate)


def _moe_ffn(h, layer, x_res, gate, w_router, b_router, w_gu, b_gu, w_down, b_down):
    B, T, D = h.shape
    N = B * T
    E, K, R = N_EXPERTS, TOP_K, MOE_ROWS
    NK = N * K
    x = h.reshape(N, D)
    top_e_t, top_w, tile_counts = _route(x, w_router, b_router, layer)
    pair = jnp.arange(NK, dtype=jnp.int32)
    e_s, order = lax.sort((top_e_t[:K].reshape(-1), pair), num_keys=1, is_stable=True)
    counts = jnp.sum(tile_counts, axis=(0, 1))[:E]
    padded = (counts + R - 1) // R * R
    g_start = jnp.cumsum(counts) - counts
    p_end = jnp.cumsum(padded)
    shift = (p_end - padded) - g_start
    dest = pair + jnp.take(shift, e_s, axis=0)
    n_blocks = (NK + R - 1) // R + E
    cap = n_blocks * R
    experts = jnp.arange(E, dtype=jnp.int32)[None, :]
    blocks = jnp.arange(n_blocks, dtype=jnp.int32)
    blk_e = jnp.minimum(jnp.sum(p_end[None, :] <= (blocks * R)[:, None], axis=1), E - 1).astype(jnp.int32)
    blk_is = blk_e[:, None] == experts
    blk_shift = jnp.sum(jnp.where(blk_is, shift[None, :], 0), axis=1)
    used = (blocks * R < p_end[E - 1]).astype(jnp.int32)
    first = used * (blk_e != jnp.concatenate([jnp.full((1,), -1, jnp.int32), blk_e[:-1]])).astype(jnp.int32)
    after = jnp.sum(jnp.where(blk_is, p_end[None, :], 0), axis=1) // R
    after_e = jnp.sum(jnp.where(after[:, None] == blocks[None, :], blk_e[None, :], 0), axis=1)
    next_e = jnp.where(after * R < p_end[E - 1], after_e, -1).astype(jnp.int32)
    tables = (blk_e, first, next_e, used)
    src = jnp.clip(jnp.arange(cap, dtype=jnp.int32) - jnp.repeat(blk_shift, R), 0, NK - 1)
    tok = jnp.take(order, src, axis=0) % N
    xb = _moe_dispatch(x, tok, used)
    act = _expert_matmul(tables, xb, w_gu, b_gu, layer, _clamped_swiglu, D_EXPERT, BF16, "moe_up")
    out = _expert_matmul(tables, act, w_down, b_down, layer, None, D, F32, "moe_down")

    _, slot = lax.sort((order, dest), num_keys=1)
    return _moe_combine(out, slot, top_w, x_res.reshape(N, D), gate, T).reshape(B, T, D)


def kernel(x, c, ada_w, ada_b, norm_g, rw_mu, rw_w_rkv, rw_w0, rw_w1, rw_w2, rw_a0, rw_a1, rw_a2, rw_g1, rw_g2, rw_k_k, rw_k_a, rw_r_k, rw_ln_g, rw_ln_b, rw_w_o, ret_w_in, ret_gn_g, ret_gn_b, ret_w_o, gla_w_in, gla_w_a1, gla_w_a2, gla_b_a, gla_gn_g, gla_w_o, nsa_w_in, nsa_q_g, nsa_k_g, nsa_cmp_pos, nsa_cmp_w1, nsa_cmp_w2, nsa_w_o, moe_router_w, moe_router_b, moe_w_gu, moe_b_gu, moe_w_down, moe_b_down):
    B, T, D = x.shape
    depth = ada_w.shape[0]
    c_act = jnp.pad(jax.nn.silu(c), ((0, 8 - B), (0, 0)))
    for i in range(depth):
        mod = _matmul(c_act, ada_w, (i,), tm=8, tn=1024)[:B] + ada_b[i]
        sh1, sc1, gt1, sh2, sc2, gt2 = jnp.split(mod, 6, axis=-1)
        gt1, gt2 = gt1.reshape(B, 1, D), gt2.reshape(B, 1, D)
        m, j = i % 4, i // 4
        if m == 0:
            x = _rwkv7_mix(x, norm_g[i, 0], sc1, sh1, gt1, rw_mu[j], rw_w_rkv[j:j + 1], rw_w0[j],
                           rw_w1[j:j + 1], rw_w2[j:j + 1], rw_a0[j], rw_a1[j:j + 1], rw_a2[j:j + 1],
                           rw_g1[j:j + 1], rw_g2[j:j + 1], rw_k_k[j], rw_k_a[j], rw_r_k[j], rw_ln_g[j],
                           rw_ln_b[j], rw_w_o[j:j + 1])
        elif m == 1:
            h, = _norm_modulate(x, norm_g[i, 0], sc1, sh1, (BF16,))
            x = _retention_mix(h, x, gt1, ret_w_in[j:j + 1], ret_gn_g[j], ret_gn_b[j], ret_w_o[j:j + 1])
        elif m == 2:
            h, = _norm_modulate(x, norm_g[i, 0], sc1, sh1, (BF16,))
            x = _gla_mix(h, x, gt1, gla_w_in[j:j + 1], gla_w_a1[j:j + 1], gla_w_a2[j:j + 1], gla_b_a[j],
                         gla_gn_g[j], gla_w_o[j:j + 1])
        else:
            h, = _norm_modulate(x, norm_g[i, 0], sc1, sh1, (BF16,))
            x = _nsa_mix(h, x, gt1, nsa_w_in[j:j + 1], nsa_q_g[j], nsa_k_g[j], nsa_cmp_pos[j],
                         nsa_cmp_w1[j], nsa_cmp_w2[j], nsa_w_o[j:j + 1])
        h, = _norm_modulate(x, norm_g[i, 1], sc2, sh2, (F32,))
        x = _moe_ffn(h, i, x, gt2, moe_router_w, moe_router_b, moe_w_gu, moe_b_gu, moe_w_down,
                     moe_b_down)
    return x
```

```python
import functools
import math

import jax
import jax.numpy as jnp
from jax import lax
from jax.experimental import pallas as pl
from jax.experimental.pallas import tpu as pltpu

F32 = jnp.float32
BF16 = jnp.bfloat16
HIGHEST = lax.Precision.HIGHEST

NORM_EPS = 1e-6
NEG_INF = -1e30

RW_HEAD_DIM = 64
RW_GN_EPS = 64e-5
RW_CHUNK = 128
RW_HEADS_PER_STEP = 8
RW_CHUNKS_PER_STEP = 1
RW_INV_BASE = 8

RET_HEADS = 8
RET_DK = 256
RET_DV = 512
RET_CHUNK = 128
RET_ROT_BASE = 10000.0
CHUNKS_PER_STEP = 4

GLA_HEADS = 4
GLA_DK = 256
GLA_DV = 512
GLA_GATE_NORM = 16.0
GLA_CHUNK = 64

NSA_HEADS = 16
NSA_KV_HEADS = 4
NSA_HD = 128
NSA_CMP_BLK = 32
NSA_CMP_STRIDE = 16
NSA_SEL_BLK = 64
NSA_SEL_TOPK = 16
NSA_WINDOW = 512
NSA_CMP_TQ = 256
NSA_ATT_TQ = 128
NSA_ATT_TK = 512
IMP_BIG = 3e38

N_EXPERTS = 32
TOP_K = 4
D_EXPERT = 768
SWIGLU_ALPHA = 1.702
SWIGLU_LIMIT = 7.0
MOE_ROWS = 256
MOE_COMBINE_TOKENS = 64
ROUTER_ROWS = 8

VMEM_LIMIT_BYTES = 52 * 1024 * 1024


def _params(*sem):
    return pltpu.CompilerParams(dimension_semantics=sem, vmem_limit_bytes=VMEM_LIMIT_BYTES)


def _bdot(a, b):
    return jnp.dot(a.astype(BF16), b.astype(BF16), preferred_element_type=F32)


def _bdot_nt(a, b):
    return lax.dot_general(a.astype(BF16), b.astype(BF16), (((1,), (1,)), ((), ())),
                           preferred_element_type=F32)


def _bdot_tn(a, b):
    return lax.dot_general(a.astype(BF16), b.astype(BF16), (((0,), (0,)), ((), ())),
                           preferred_element_type=F32)


def _fdot(a, b):
    return jnp.dot(a, b, precision=HIGHEST, preferred_element_type=F32)


def _fdot_nt(a, b):
    return lax.dot_general(a, b, (((1,), (1,)), ((), ())), precision=HIGHEST,
                           preferred_element_type=F32)


def _split3(x):
    hi = x.astype(BF16)
    rem = x - hi.astype(F32)
    mid = rem.astype(BF16)
    return hi, mid, (rem - mid.astype(F32)).astype(BF16)


def _exact_lhs_dot(m, x):
    mb = m.astype(BF16)
    return sum(jnp.dot(mb, part, preferred_element_type=F32) for part in _split3(x))


def _exact_rhs_dot(x, m):
    mb = m.astype(BF16)
    return sum(jnp.dot(part, mb, preferred_element_type=F32) for part in _split3(x))


def _softplus(z):
    return jnp.maximum(z, 0.0) + jnp.log(1.0 + jnp.exp(-jnp.abs(z)))


_POST = {
    None: lambda z: z,
    "tanh": jnp.tanh,
    "sigmoid": jax.nn.sigmoid,
    "silu": jax.nn.silu,
    "rwkv_log_decay": lambda z: -jnp.exp(-_softplus(-z) - 0.5),
    "gla_log_gate": lambda z: -_softplus(-z) / GLA_GATE_NORM,
}


def _mm_kernel(*refs, has_bias, has_res, post):
    x_ref, w_ref = refs[0], refs[1]
    pos = 2
    acc = _bdot(x_ref[...], w_ref[...])
    if has_bias:
        acc = acc + refs[pos][...]
        pos += 1
    acc = _POST[post](acc)
    if has_res:
        acc = refs[pos][...] + refs[pos + 1][0] * acc
        pos += 2
    o_ref = refs[pos]
    o_ref[...] = acc.astype(o_ref.dtype)


def _matmul(x, w, lead=(), *, bias=None, res=None, gate=None, n_cols=None, tm=None, tn=None,
            post=None, out_dtype=F32, name="matmul"):
    M, K = x.shape
    N = n_cols if n_cols is not None else w.shape[-1]
    if tm is None:
        tm = min(M, 1024 if x.dtype == BF16 else 512)
    if tn is None:
        tn = min(N, 512) if N % 128 == 0 else N
    assert M % tm == 0
    nlead = len(lead)
    in_specs = [
        pl.BlockSpec((tm, K), lambda i, j: (i, 0)),
        pl.BlockSpec((None,) * nlead + (K, tn), lambda i, j: tuple(lead) + (0, j)),
    ]
    args = [x, w]
    if bias is not None:
        in_specs.append(pl.BlockSpec((1, tn), lambda i, j: (0, j)))
        args.append(bias)
    if res is not None:
        rows_per_gate = M // gate.shape[0]
        assert rows_per_gate % tm == 0
        in_specs.append(pl.BlockSpec((tm, tn), lambda i, j: (i, j)))
        in_specs.append(pl.BlockSpec((1, 1, tn), lambda i, j: ((i * tm) // rows_per_gate, 0, j)))
        args += [res, gate]
    return pl.pallas_call(
        functools.partial(_mm_kernel, has_bias=bias is not None, has_res=res is not None, post=post),
        grid=(M // tm, pl.cdiv(N, tn)),
        in_specs=in_specs,
        out_specs=pl.BlockSpec((tm, tn), lambda i, j: (i, j)),
        out_shape=jax.ShapeDtypeStruct((M, N), out_dtype),
        compiler_params=_params("parallel", "parallel"),
        name=name,
    )(*args)


def _modulated_norm(x, g, sc, sh):
    y = x * lax.rsqrt(jnp.mean(x * x, axis=-1, keepdims=True) + NORM_EPS) * g
    return y * (1.0 + sc) + sh


def _normmod_kernel(x_ref, g_ref, sc_ref, sh_ref, *o_refs):
    h = _modulated_norm(x_ref[0], g_ref[...], sc_ref[0], sh_ref[0])
    for o_ref in o_refs:
        o_ref[0] = h.astype(o_ref.dtype)


def _norm_modulate(x, g, sc, sh, out_dtypes):
    B, T, D = x.shape
    tr = min(T, 512)
    row = pl.BlockSpec((1, tr, D), lambda b, i: (b, i, 0))
    per_batch = pl.BlockSpec((1, 1, D), lambda b, i: (b, 0, 0))
    return pl.pallas_call(
        _normmod_kernel,
        grid=(B, T // tr),
        in_specs=[row, pl.BlockSpec((1, D), lambda b, i: (0, 0)), per_batch, per_batch],
        out_specs=[row] * len(out_dtypes),
        out_shape=[jax.ShapeDtypeStruct((B, T, D), dt) for dt in out_dtypes],
        compiler_params=_params("parallel", "parallel"),
        name="norm_modulate",
    )(x, g.reshape(1, D), sc.reshape(B, 1, D), sh.reshape(B, 1, D))


def _rwkv_prep_kernel(x_ref, halo_ref, g_ref, sc_ref, sh_ref, mu_ref, *o_refs, halo_rows):
    h = _modulated_norm(x_ref[0], g_ref[...], sc_ref[0], sh_ref[0])
    h_halo = _modulated_norm(halo_ref[0], g_ref[...], sc_ref[0], sh_ref[0])
    first = jnp.where(pl.program_id(1) > 0, h_halo[halo_rows - 1:halo_rows, :], 0.0)
    row = lax.broadcasted_iota(jnp.int32, h.shape, 0)
    d = jnp.where(row == 0, first, pltpu.roll(h, 1, axis=0)) - h
    for j, o_ref in enumerate(o_refs):
        o_ref[0] = (h + d * mu_ref[j:j + 1, :]).astype(o_ref.dtype)


def _rwkv_prep(x, g, sc, sh, mu):
    B, T, D = x.shape
    tr = min(T, 512)
    hr = 8
    n_mix = mu.shape[0]
    row = pl.BlockSpec((1, tr, D), lambda b, i: (b, i, 0))
    per_batch = pl.BlockSpec((1, 1, D), lambda b, i: (b, 0, 0))
    return pl.pallas_call(
        functools.partial(_rwkv_prep_kernel, halo_rows=hr),
        grid=(B, T // tr),
        in_specs=[row,
                  pl.BlockSpec((1, hr, D), lambda b, i: (b, jnp.maximum(i * (tr // hr) - 1, 0), 0)),
                  pl.BlockSpec((1, D), lambda b, i: (0, 0)), per_batch, per_batch,
                  pl.BlockSpec((n_mix, D), lambda b, i: (0, 0))],
        out_specs=[row] * n_mix,
        out_shape=[jax.ShapeDtypeStruct((B, T, D), BF16)] * n_mix,
        compiler_params=_params("parallel", "parallel"),
        name="rwkv_shift_mix",
    )(x, x, g.reshape(1, D), sc.reshape(B, 1, D), sh.reshape(B, 1, D), mu)


def _rwkv_kernel(*refs, C, N, HB, SUB):
    h_ref = refs[-1]

    @pl.when(pl.program_id(2) == 0)
    def _():
        h_ref[...] = jnp.zeros_like(h_ref)

    for s in range(SUB):
        _rwkv_chunk(slice(s * C, (s + 1) * C), *refs, C=C, N=N, HB=HB)


def _rwkv_chunk(rows, r_ref, k_ref, v_ref, lw_ref, a_ref, g_ref, kk_ref, ka_ref, rk_ref, lng_ref,
                lnb_ref, o_ref, h_ref, *, C, N, HB):
    row = lax.broadcasted_iota(jnp.int32, (C, C), 0)
    col = lax.broadcasted_iota(jnp.int32, (C, C), 1)
    incl = row >= col
    strict = row > col
    eye = (row == col).astype(F32)
    eye_n = (lax.broadcasted_iota(jnp.int32, (N, N), 0) == lax.broadcasted_iota(jnp.int32, (N, N), 1))
    heads = range(HB)

    def head(x, i):
        return x[:, i * N:(i + 1) * N]

    r_all, k_all, v_all, a_all, lw_all = (ref[0, rows, :] for ref in (r_ref, k_ref, v_ref, a_ref, lw_ref))
    cum_all = _exact_lhs_dot(incl.astype(F32), lw_all)
    tot_all = cum_all[C - 1:C, :]
    mid_all = cum_all[C // 2 - 1:C // 2, :]
    cen_all = cum_all - mid_all
    e_neg = jnp.exp(-cen_all)
    e_tail = jnp.exp(tot_all - cum_all)
    kx_all = k_all * kk_ref[...]
    kp_all = k_all * (1.0 + (a_all - 1.0) * ka_ref[...])
    rt_all = r_all * jnp.exp(cen_all)
    kn_all = kp_all * e_neg
    kt_all = kp_all * e_tail
    ep_all = jnp.exp(cen_all - lw_all)
    rkr_all = r_all * kp_all * rk_ref[...]

    def column(row_vec):
        return jnp.sum(jnp.where(eye_n, row_vec, 0.0), axis=1, keepdims=True)

    kappa = [head(kx_all, i) / jnp.maximum(
        jnp.sqrt(jnp.sum(jnp.square(head(kx_all, i)), axis=-1, keepdims=True)), 1e-12) for i in heads]
    b = [kappa[i] * head(a_all, i) for i in heads]
    kap_t = [kappa[i] * head(ep_all, i) for i in heads]
    r_t = [head(rt_all, i) for i in heads]
    v = [head(v_all, i) for i in heads]
    big = [_bdot_nt(jnp.concatenate([kap_t[i], r_t[i]], axis=0),
                    jnp.concatenate([b[i] * head(e_neg, i), head(kn_all, i)], axis=0))
           for i in heads]
    t_k = [jnp.where(strict, big[i][:C, C:], 0.0) for i in heads]
    m_b = [jnp.where(incl, big[i][C:, :C], 0.0) for i in heads]
    m_k = [jnp.where(incl, big[i][C:, C:], 0.0) for i in heads]
    t_b = [jnp.where(strict, big[i][:C, :C], 0.0) for i in heads]
    base = RW_INV_BASE
    p = [jnp.where((row // base) == (col // base), -t_b[i], 0.0) for i in heads]
    inv = [eye + p[i] for i in heads]
    tkv = [_bdot(t_k[i], v[i]) for i in heads]
    for _ in range(int(math.log2(base)) - 1):
        p = [_bdot(p[i], p[i]) for i in heads]
        inv = [inv[i] + _bdot(inv[i], p[i]) for i in heads]
    size = base
    while size < C:
        couple = ((row // (2 * size)) == (col // (2 * size))) & ((row // size) != (col // size))
        lower = [jnp.where(couple, t_b[i], 0.0) for i in heads]
        inv = [inv[i] - _bdot(inv[i], _bdot(lower[i], inv[i])) for i in heads]
        size *= 2
    h0 = [h_ref[i] for i in heads]
    h0c = [h0[i] * jnp.exp(column(head(mid_all, i))) for i in heads]
    aw = [_bdot(inv[i], jnp.concatenate([kap_t[i], tkv[i]], axis=1)) for i in heads]
    u = [_bdot(aw[i][:, :N], h0c[i]) + aw[i][:, N:] for i in heads]
    y = [_bdot(jnp.concatenate([r_t[i], m_k[i], -m_b[i]], axis=1),
               jnp.concatenate([h0c[i], v[i], u[i]], axis=0)) for i in heads]
    for i in heads:
        h_ref[i] = jnp.exp(column(head(tot_all, i))) * h0[i] + _bdot_tn(
            jnp.concatenate([head(kt_all, i), -(b[i] * head(e_tail, i))], axis=0),
            jnp.concatenate([v[i], u[i]], axis=0))
    outs = []
    for i in heads:
        mu = jnp.mean(y[i], axis=-1, keepdims=True)
        var = jnp.mean(jnp.square(y[i] - mu), axis=-1, keepdims=True)
        yn = (y[i] - mu) * lax.rsqrt(var + RW_GN_EPS)
        outs.append(yn * head(lng_ref[...], i) + head(lnb_ref[...], i)
                    + jnp.sum(head(rkr_all, i), axis=-1, keepdims=True) * v[i])
    o_ref[0, rows, :] = (jnp.concatenate(outs, axis=-1) * g_ref[0, rows, :]).astype(o_ref.dtype)


def _rwkv_core(r, k, v, lw, a, g, k_k, k_a, r_k, ln_g, ln_b):
    B, T, D = r.shape
    C, N, HB, SUB = RW_CHUNK, RW_HEAD_DIM, RW_HEADS_PER_STEP, RW_CHUNKS_PER_STEP
    W = HB * N
    seq = pl.BlockSpec((1, SUB * C, W), lambda b, h, c: (b, c, h))
    par = pl.BlockSpec((1, W), lambda b, h, c: (0, h))
    return pl.pallas_call(
        functools.partial(_rwkv_kernel, C=C, N=N, HB=HB, SUB=SUB),
        grid=(B, D // W, T // (SUB * C)),
        in_specs=[seq] * 6 + [par] * 5,
        out_specs=seq,
        out_shape=jax.ShapeDtypeStruct((B, T, D), BF16),
        scratch_shapes=[pltpu.VMEM((HB, N, N), F32)],
        compiler_params=_params("parallel", "parallel", "arbitrary"),
        name="rwkv_core",
    )(r, k, v, lw, a, g, k_k.reshape(1, D), k_a.reshape(1, D), r_k.reshape(1, D),
      ln_g.reshape(1, D), ln_b.reshape(1, D))


def _rwkv7_mix(x_res, norm_g, sc, shift, gate, mu, w_rkv, w0, w1, w2, a0, a1, a2, g1, g2, k_k, k_a,
               r_k, ln_g, ln_b, w_o):
    B, T, D = x_res.shape
    M = B * T
    xr, xk, xv, xw, xa, xg = [z.reshape(M, D) for z in _rwkv_prep(x_res, norm_g, sc, shift, mu)]
    r = _matmul(xr, w_rkv, (0, 0), name="rwkv_r")
    k = _matmul(xk, w_rkv, (0, 1), name="rwkv_k")
    v = _matmul(xv, w_rkv, (0, 2), name="rwkv_v")
    log_decay = _matmul(_matmul(xw, w1, (0,), post="tanh", out_dtype=BF16), w2, (0,),
                        bias=w0.reshape(1, D), post="rwkv_log_decay", name="rwkv_decay")
    a = _matmul(_matmul(xa, a1, (0,), out_dtype=BF16), a2, (0,), bias=a0.reshape(1, D),
                post="sigmoid", name="rwkv_a")
    g = _matmul(_matmul(xg, g1, (0,), post="sigmoid", out_dtype=BF16), g2, (0,), name="rwkv_g")
    sh = (B, T, D)
    o = _rwkv_core(r.reshape(sh), k.reshape(sh), v.reshape(sh), log_decay.reshape(sh),
                   a.reshape(sh), g.reshape(sh), k_k, k_a, r_k, ln_g, ln_b)
    return _matmul(o.reshape(M, D), w_o, (0,), res=x_res.reshape(M, D), gate=gate).reshape(sh)


def _ret_kernel(q_ref, k_ref, v_ref, g_ref, cos_ref, sin_ref, dm_ref, xz_ref, gng_ref, gnb_ref,
                o_ref, s_ref, *, dk, C, SUB):
    @pl.when(pl.program_id(2) == 0)
    def _():
        s_ref[...] = jnp.zeros_like(s_ref)

    half = dk // 2
    xi, zeta, g_chunk = xz_ref[0, :, 0:1], xz_ref[0, :, 1:2], xz_ref[0, 0:1, 2:3]
    for s in range(SUB):
        rows = slice(s * C, (s + 1) * C)
        cos, sin = cos_ref[rows, :], sin_ref[rows, :]

        def rot(z):
            z1, z2 = z[:, :half], z[:, half:]
            return jnp.concatenate([z1 * cos - z2 * sin, z1 * sin + z2 * cos], axis=-1)

        q = rot(q_ref[0, rows, :])
        k = rot(k_ref[0, rows, :] * (dk ** -0.5))
        v = v_ref[0, rows, :]
        scores = _bdot_nt(q, k) * dm_ref[0]
        state = s_ref[...]
        o = _bdot(scores, v) + _bdot(q * xi, state)
        s_ref[...] = g_chunk * state + _bdot_tn(k * zeta, v)
        mu = jnp.mean(o, axis=-1, keepdims=True)
        var = jnp.mean(jnp.square(o - mu), axis=-1, keepdims=True)
        y = (o - mu) * lax.rsqrt(var + NORM_EPS) * gng_ref[0] + gnb_ref[0]
        gate = g_ref[0, rows, :]
        o_ref[0, rows, :] = (y * (gate * jax.nn.sigmoid(gate))).astype(o_ref.dtype)


def _retention_mix(h, x_res, gate, w_in, gn_g, gn_b, w_o):
    B, T, D = h.shape
    M = B * T
    H, dk, dv, C = RET_HEADS, RET_DK, RET_DV, RET_CHUNK
    SUB = min(CHUNKS_PER_STEP, T // C)
    R = SUB * C
    proj =_matmul(h.reshape(M, D), w_in, (0,)).reshape(B, T, H * (2 * dk + 2 * dv))
    qb, kb, vb, gb = 0, (H * dk) // dk, (2 * H * dk) // dv, (2 * H * dk + H * dv) // dv
    theta = 1.0 / (RET_ROT_BASE ** jnp.linspace(0.0, 1.0, dk // 2, dtype=F32))
    ang = jnp.arange(T, dtype=F32)[:, None] * theta[None, :]
    log_gamma = jnp.log(1.0 - 2.0 ** (-5.0 - jnp.arange(H, dtype=F32)))
    pos = jnp.arange(C, dtype=F32)
    rel = pos[:, None] - pos[None, :]
    dmask = jnp.where(rel >= 0, jnp.exp(jnp.maximum(rel, 0.0) * log_gamma[:, None, None]), 0.0)
    xi = jnp.exp((pos + 1.0)[None, :] * log_gamma[:, None])
    zeta = jnp.exp((C - 1.0 - pos)[None, :] * log_gamma[:, None])
    g_chunk = jnp.broadcast_to(jnp.exp(C * log_gamma)[:, None], (H, C))
    xz = jnp.concatenate([jnp.stack([xi, zeta, g_chunk], axis=-1), jnp.zeros((H, C, 125), F32)], axis=-1)
    o = pl.pallas_call(
        functools.partial(_ret_kernel, dk=dk, C=C, SUB=SUB),
        grid=(B, H, T // R),
        in_specs=[
            pl.BlockSpec((1, R, dk), lambda b, h, c: (b, c, qb + h)),
            pl.BlockSpec((1, R, dk), lambda b, h, c: (b, c, kb + h)),
            pl.BlockSpec((1, R, dv), lambda b, h, c: (b, c, vb + h)),
            pl.BlockSpec((1, R, dv), lambda b, h, c: (b, c, gb + h)),
            pl.BlockSpec((R, dk // 2), lambda b, h, c: (c, 0)),
            pl.BlockSpec((R, dk // 2), lambda b, h, c: (c, 0)),
            pl.BlockSpec((1, C, C), lambda b, h, c: (h, 0, 0)),
            pl.BlockSpec((1, C, 128), lambda b, h, c: (h, 0, 0)),
            pl.BlockSpec((1, 1, dv), lambda b, h, c: (h, 0, 0)),
            pl.BlockSpec((1, 1, dv), lambda b, h, c: (h, 0, 0)),
        ],
        out_specs=pl.BlockSpec((1, R, dv), lambda b, h, c: (b, c, h)),
        out_shape=jax.ShapeDtypeStruct((B, T, H * dv), BF16),
        scratch_shapes=[pltpu.VMEM((dk, dv), F32)],
        compiler_params=_params("parallel", "parallel", "arbitrary"),
        name="retention_core",
    )(proj, proj, proj, proj, jnp.cos(ang), jnp.sin(ang), dmask, xz,
      gn_g.reshape(H, 1, dv), gn_b.reshape(H, 1, dv))
    return _matmul(o.reshape(M, H * dv), w_o, (0,), res=x_res.reshape(M, D), gate=gate).reshape(B, T, D)


def _gla_kernel(q_ref, k_ref, v_ref, g_ref, la_ref, gn_ref, o_ref, s_ref, *, C, dk, SUB):
    @pl.when(pl.program_id(2) == 0)
    def _():
        s_ref[...] = jnp.zeros_like(s_ref)

    row = lax.broadcasted_iota(jnp.int32, (C, C), 0)
    col = lax.broadcasted_iota(jnp.int32, (C, C), 1)
    causal = row >= col
    eye_k = (lax.broadcasted_iota(jnp.int32, (dk, dk), 0) == lax.broadcasted_iota(jnp.int32, (dk, dk), 1))
    for s in range(SUB):
        rows = slice(s * C, (s + 1) * C)
        la = la_ref[0, rows, :]
        b = _exact_lhs_dot(causal.astype(F32), la)
        b_last = b[C - 1:C, :]
        d_last_col = jnp.exp(jnp.sum(jnp.where(eye_k, b_last, 0.0), axis=1, keepdims=True))
        k, v = k_ref[0, rows, :], v_ref[0, rows, :]
        q_in = q_ref[0, rows, :] * (dk ** -0.5) * jnp.exp(b)
        k_in = k * jnp.exp(-b)
        att = jnp.where(causal, _bdot_nt(q_in, k_in), 0.0)
        state = s_ref[...]
        o = _bdot(att, v) + _bdot(q_in, state)
        s_ref[...] = d_last_col * state + _bdot_tn(k * jnp.exp(b_last - b), v)
        y = o * lax.rsqrt(jnp.mean(o * o, axis=-1, keepdims=True) + NORM_EPS) * gn_ref[...]
        gate = g_ref[0, rows, :]
        o_ref[0, rows, :] = (y * (gate * jax.nn.sigmoid(gate))).astype(o_ref.dtype)


def _gla_mix(h, x_res, gate, w_in, w_a1, w_a2, b_a, gn_g, w_o):
    B, T, D = h.shape
    M = B * T
    H, dk, dv, C = GLA_HEADS, GLA_DK, GLA_DV, GLA_CHUNK
    SUB = min(CHUNKS_PER_STEP, T // C)
    R = SUB * C
    h2 = h.reshape(M, D)
    proj = _matmul(h2, w_in, (0,)).reshape(B, T, H * (2 * dk + 2 * dv))
    log_a = _matmul(_matmul(h2, w_a1, (0,), out_dtype=BF16), w_a2, (0,), bias=b_a.reshape(1, H * dk),
                    post="gla_log_gate", name="gla_log_gate").reshape(B, T, H * dk)
    qb, kb, vb, gb = 0, H, (2 * H * dk) // dv, (2 * H * dk + H * dv) // dv
    o = pl.pallas_call(
        functools.partial(_gla_kernel, C=C, dk=dk, SUB=SUB),
        grid=(B, H, T // R),
        in_specs=[
            pl.BlockSpec((1, R, dk), lambda b, h, c: (b, c, qb + h)),
            pl.BlockSpec((1, R, dk), lambda b, h, c: (b, c, kb + h)),
            pl.BlockSpec((1, R, dv), lambda b, h, c: (b, c, vb + h)),
            pl.BlockSpec((1, R, dv), lambda b, h, c: (b, c, gb + h)),
            pl.BlockSpec((1, R, dk), lambda b, h, c: (b, c, h)),
            pl.BlockSpec((1, dv), lambda b, h, c: (0, 0)),
        ],
        out_specs=pl.BlockSpec((1, R, dv), lambda b, h, c: (b, c, h)),
        out_shape=jax.ShapeDtypeStruct((B, T, H * dv), BF16),
        scratch_shapes=[pltpu.VMEM((dk, dv), F32)],
        compiler_params=_params("parallel", "parallel", "arbitrary"),
        name="gla_core",
    )(proj, proj, proj, proj, log_a, gn_g.reshape(1, dv))
    return _matmul(o.reshape(M, H * dv), w_o, (0,), res=x_res.reshape(M, D), gate=gate).reshape(B, T, D)


def _nsa_cmp_kernel(q_ref, kc_ref, vc_ref, o_ref, sel_ref, *, tq, n_cmp, n_pad, n_slc, hpg):
    L, S, Ls, hd = NSA_CMP_BLK, NSA_CMP_STRIDE, NSA_SEL_BLK, NSA_HD
    t_pos = pl.program_id(2) * tq + lax.broadcasted_iota(jnp.int32, (tq, 1), 0)
    n_ix = lax.broadcasted_iota(jnp.int32, (1, n_pad), 1)
    valid = (n_ix * S + (L - 1) <= t_pos) & (n_ix < n_cmp)
    validf = valid.astype(F32)
    c_start = lax.broadcasted_iota(jnp.int32, (n_pad, n_slc), 0) * S
    s_start = lax.broadcasted_iota(jnp.int32, (n_pad, n_slc), 1) * Ls
    overlap = ((c_start <= s_start + (Ls - 1)) & (c_start + (L - 1) >= s_start)
               & (c_start < n_cmp * S)).astype(F32)
    kc, vc = kc_ref[0, 0], vc_ref[0, 0]
    kc_hi = kc.astype(BF16)
    kc_lo = (kc - kc_hi.astype(F32)).astype(BF16)
    heads = range(hpg)
    q = [q_ref[0, :, hh * hd:(hh + 1) * hd] for hh in heads]
    q_hi = [q[hh].astype(BF16) for hh in heads]
    q_lo = [(q[hh] - q_hi[hh].astype(F32)).astype(BF16) for hh in heads]
    s = [_bdot_nt(q_hi[hh], kc_hi) + _bdot_nt(q_hi[hh], kc_lo) + _bdot_nt(q_lo[hh], kc_hi) for hh in heads]
    s = [jnp.where(valid, s[hh], NEG_INF) for hh in heads]
    e = [jnp.exp(s[hh] - jnp.max(s[hh], axis=-1, keepdims=True)) for hh in heads]
    p = [e[hh] / jnp.sum(e[hh], axis=-1, keepdims=True) * validf for hh in heads]
    o_ref[0] = jnp.concatenate([_bdot(p[hh], vc) for hh in heads], axis=-1)
    imp = _exact_rhs_dot(sum(p), overlap)

    imp = imp.T
    j = lax.broadcasted_iota(jnp.int32, (n_slc, 1), 0)
    cur = (pl.program_id(2) * tq + lax.broadcasted_iota(jnp.int32, (1, tq), 1)) // Ls
    forced = (j == 0) | (j == cur) | (j == cur - 1)
    imp = jnp.where(j > cur, -IMP_BIG, jnp.where(forced, IMP_BIG, imp))
    rank = jnp.zeros((n_slc, tq), jnp.int32)
    for jp in range(n_slc):
        c = imp[jp:jp + 1, :]
        rank = rank + ((c > imp) | ((c == imp) & (jp < j))).astype(jnp.int32)
    sel_ref[0, 0] = (rank < min(NSA_SEL_TOPK, n_slc)).astype(F32)


def _nsa_att_kernel(qt_ref, ks_ref, vst_ref, kw_ref, vwt_ref, sel_ref, oc_ref, gt_ref, gtt_ref, o_ref,
                    m_ref, l_ref, acc_ref, *, tq, tk, T, n_slc, hpg):
    Ls, W, hd = NSA_SEL_BLK, NSA_WINDOW, NSA_HD
    qi = pl.program_id(2)
    t0 = qi * tq
    heads = range(hpg)
    cols = [slice(hh * tq, (hh + 1) * tq) for hh in heads]
    qt = jnp.concatenate([qt_ref[0, hh * hd:(hh + 1) * hd, :] for hh in heads], axis=1)
    t_q = t0 + lax.broadcasted_iota(jnp.int32, (1, tq), 1)
    sel_t = sel_ref[0, 0]

    m_ref[...] = jnp.full_like(m_ref, NEG_INF)
    l_ref[...] = jnp.zeros_like(l_ref)
    acc_ref[...] = jnp.zeros_like(acc_ref)

    pad = ks_ref.shape[2] - hd - n_slc
    sel_bias = jnp.where(sel_t > 0.5, 0.0, NEG_INF).astype(BF16)
    q_aug = jnp.concatenate([qt, jnp.concatenate([sel_bias] * hpg, axis=1),
                             jnp.zeros((pad, hpg * tq), BF16)], axis=0)

    def key_tile(kb, causal):
        k0 = pl.multiple_of(kb * tk, tk)
        vt_t = vst_ref[0, :, pl.ds(k0, tk)]
        s_all = jnp.dot(ks_ref[0, pl.ds(k0, tk), :], q_aug, preferred_element_type=F32)
        kpos = k0 + lax.broadcasted_iota(jnp.int32, (tk, 1), 0)
        ps = []
        for hh in heads:
            s = s_all[:, cols[hh]]
            if causal:
                s = jnp.where(kpos <= t_q, s, NEG_INF)
            m_old = m_ref[:, cols[hh]]
            m_new = jnp.maximum(m_old, jnp.max(s, axis=0, keepdims=True))
            alpha = jnp.exp2(m_old - m_new)
            p = jnp.exp2(s - m_new)
            l_ref[:, cols[hh]] = alpha * l_ref[:, cols[hh]] + jnp.sum(p, axis=0, keepdims=True)
            acc_ref[:, cols[hh]] = alpha * acc_ref[:, cols[hh]]
            m_ref[:, cols[hh]] = m_new
            ps.append(p.astype(BF16))
        acc_ref[...] += jnp.dot(vt_t, jnp.concatenate(ps, axis=1), preferred_element_type=F32)

    n_full = t0 // tk

    def full_tile(kb, carry):
        key_tile(kb, causal=False)
        return carry

    lax.fori_loop(0, n_full, full_tile, 0)
    key_tile(n_full, causal=True)
    o_sel_t = acc_ref[...] / l_ref[...]

    span = W + tq
    w0 = pl.multiple_of(jnp.maximum(t0 - W, 0), tq)
    kw = kw_ref[0, pl.ds(w0, span), :]
    vw_t = vwt_ref[0, :, pl.ds(w0, span)]
    wpos = w0 + lax.broadcasted_iota(jnp.int32, (span, 1), 0)
    bias_w = jnp.where((wpos <= t_q) & (wpos > t_q - W), 0.0, NEG_INF)
    s_all = jnp.dot(kw, qt, preferred_element_type=F32)
    es, ls = [], []
    for hh in heads:
        s = s_all[:, cols[hh]] + bias_w
        e = jnp.exp2(s - jnp.max(s, axis=0, keepdims=True))
        ls.append(jnp.sum(e, axis=0, keepdims=True))
        es.append(e.astype(BF16))
    o_win_t = (jnp.dot(vw_t, jnp.concatenate(es, axis=1), preferred_element_type=F32)
               / jnp.concatenate(ls, axis=1))

    gt = jax.nn.sigmoid(gt_ref[0, 0])
    gt_t = jax.nn.sigmoid(gtt_ref[0, 0])
    outs = []
    for hh in heads:
        mixed_t = (gt_t[hpg + hh:hpg + hh + 1, :] * o_sel_t[:, cols[hh]]
                   + gt_t[2 * hpg + hh:2 * hpg + hh + 1, :] * o_win_t[:, cols[hh]])
        outs.append(gt[:, hh:hh + 1] * oc_ref[0, :, hh * hd:(hh + 1) * hd] + mixed_t.T)
    o_ref[0] = jnp.concatenate(outs, axis=-1).astype(o_ref.dtype)


def _rms_norm(x, g):
    return x * lax.rsqrt(jnp.mean(x * x, axis=-1, keepdims=True) + NORM_EPS) * g


def _nsa_mix(h, x_res, gate, w_in, q_g, k_g, cmp_pos, cmp_w1, cmp_w2, w_o):
    B, T, D = h.shape
    M = B * T
    H, G, hd = NSA_HEADS, NSA_KV_HEADS, NSA_HD
    hpg = H // G
    L, S, Ls = NSA_CMP_BLK, NSA_CMP_STRIDE, NSA_SEL_BLK
    kvw = G * hd
    n_main = H * hd + 6 * kvw
    h2 = h.reshape(M, D)
    proj = _matmul(h2, w_in, (0,), n_cols=n_main)
    gate_w = jnp.pad(w_in[0, :, n_main:], ((0, 0), (0, 128 - 3 * H)))
    gates = _matmul(h2, gate_w)[:, :3 * H]
    q = _rms_norm(proj[:, :H * hd].reshape(B, T, H, hd), q_g) * (hd ** -0.5)
    q = q.reshape(B, T, H * hd)

    def kv(i):
        return proj[:, H * hd + i * kvw:H * hd + (i + 1) * kvw].reshape(B, T, G, hd)

    n_cmp = (T - L) // S + 1
    n_grp = T // S
    assert L == 2 * S and n_cmp == n_grp - 1

    def compress(z, pos, w1, w2):
        zg = z.reshape(B, n_grp, S, G, hd).transpose(0, 3, 1, 2, 4).reshape(B * G * n_grp, S * hd)
        w_halves = jnp.concatenate([w1[:S * hd], w1[S * hd:]], axis=1)
        part = _matmul(zg, w_halves, tm=min(512, B * G * n_grp)).reshape(B, G, n_grp, 2 * hd)
        pos_term = _matmul(jnp.broadcast_to(pos.reshape(1, L * hd), (8, L * hd)), w1, tm=8)[0]
        pre = part[:, :, :-1, :hd] + part[:, :, 1:, hd:] + pos_term
        pre = jnp.pad(pre, ((0, 0), (0, 0), (0, 1), (0, 0))).reshape(B * G * n_grp, hd)
        return _matmul(jax.nn.silu(pre), w2, tm=min(512, B * G * n_grp)).reshape(B, G, n_grp, hd)

    kc = _rms_norm(compress(kv(0), cmp_pos[0], cmp_w1[0], cmp_w2[0]), k_g[0])
    vc = compress(kv(1), cmp_pos[1], cmp_w1[1], cmp_w2[1])
    k_feat = 2 * hd
    blk_one_hot = (jnp.arange(T)[:, None] // Ls == jnp.arange(k_feat - hd)[None, :]).astype(BF16)
    k_sel = jnp.concatenate([_rms_norm(kv(2), k_g[1]).astype(BF16),
                             jnp.broadcast_to(blk_one_hot[None, :, None, :], (B, T, G, k_feat - hd))],
                            axis=-1).reshape(B, T, G * k_feat)
    v_sel = kv(3).reshape(B, T, kvw).astype(BF16)
    k_win = _rms_norm(kv(4), k_g[2]).reshape(B, T, kvw).astype(BF16)
    v_win = kv(5).reshape(B, T, kvw).astype(BF16)

    n_slc = T // Ls
    tq = min(NSA_CMP_TQ, T)
    o_cmp, sel = pl.pallas_call(
        functools.partial(_nsa_cmp_kernel, tq=tq, n_cmp=n_cmp, n_pad=n_grp, n_slc=n_slc, hpg=hpg),
        grid=(B, G, T // tq),
        in_specs=[
            pl.BlockSpec((1, tq, hpg * hd), lambda b, g, i: (b, i, g)),
            pl.BlockSpec((1, 1, n_grp, hd), lambda b, g, i: (b, g, 0, 0)),
            pl.BlockSpec((1, 1, n_grp, hd), lambda b, g, i: (b, g, 0, 0)),
        ],
        out_specs=[
            pl.BlockSpec((1, tq, hpg * hd), lambda b, g, i: (b, i, g)),
            pl.BlockSpec((1, 1, n_slc, tq), lambda b, g, i: (b, g, 0, i)),
        ],
        out_shape=[jax.ShapeDtypeStruct((B, T, H * hd), F32),
                   jax.ShapeDtypeStruct((B, G, n_slc, T), F32)],
        compiler_params=_params("parallel", "parallel", "parallel"),
        name="nsa_compressed",
    )(q, kc, vc)

    gt = gates.reshape(B, T, 3, G, hpg).transpose(0, 3, 1, 2, 4).reshape(B, G, T, 3 * hpg)
    tq = min(NSA_ATT_TQ, T)
    tk = min(NSA_ATT_TK, T)
    assert T >= NSA_WINDOW + tq
    k_spec = pl.BlockSpec((1, T, hd), lambda b, g, i: (b, 0, g))
    vt_spec = pl.BlockSpec((1, hd, T), lambda b, g, i: (b, g, 0))

    def feature_major(z):
        return jnp.swapaxes(z, 1, 2)

    o = pl.pallas_call(
        functools.partial(_nsa_att_kernel, tq=tq, tk=tk, T=T, n_slc=n_slc, hpg=hpg),
        grid=(B, G, T // tq),
        in_specs=[
            pl.BlockSpec((1, hpg * hd, tq), lambda b, g, i: (b, g, i)),
            pl.BlockSpec((1, T, k_feat), lambda b, g, i: (b, 0, g)), vt_spec, k_spec, vt_spec,
            pl.BlockSpec((1, 1, n_slc, tq), lambda b, g, i: (b, g, 0, i)),
            pl.BlockSpec((1, tq, hpg * hd), lambda b, g, i: (b, i, g)),
            pl.BlockSpec((1, 1, tq, 3 * hpg), lambda b, g, i: (b, g, i, 0)),
            pl.BlockSpec((1, 1, 3 * hpg, tq), lambda b, g, i: (b, g, 0, i)),
        ],
        out_specs=pl.BlockSpec((1, tq, hpg * hd), lambda b, g, i: (b, i, g)),
        out_shape=jax.ShapeDtypeStruct((B, T, H * hd), BF16),
        scratch_shapes=[pltpu.VMEM((1, hpg * tq), F32), pltpu.VMEM((1, hpg * tq), F32),
                        pltpu.VMEM((hd, hpg * tq), F32)],
        compiler_params=_params("parallel", "parallel", "arbitrary"),
        name="nsa_selected_window",
    )(feature_major((q * math.log2(math.e)).astype(BF16)), k_sel, feature_major(v_sel), k_win, feature_major(v_win), sel,
      o_cmp, gt, jnp.swapaxes(gt, 2, 3))
    return _matmul(o.reshape(M, H * hd), w_o, (0,), res=x_res.reshape(M, D), gate=gate).reshape(B, T, D)


def _clamped_swiglu(gu):
    x_glu = jnp.minimum(gu[:, :D_EXPERT], SWIGLU_LIMIT)
    x_lin = jnp.clip(gu[:, D_EXPERT:], -SWIGLU_LIMIT, SWIGLU_LIMIT)
    return x_glu * jax.nn.sigmoid(SWIGLU_ALPHA * x_glu) * (x_lin + 1.0)


def _expert_kernel(blk_e_ref, first_ref, next_ref, used_ref, x_ref, w_hbm, b_ref, o_ref, stage, w_bf16,
                   sem, *, layer, post):
    i = pl.program_id(0)

    def fetch(expert):
        return pltpu.make_async_copy(w_hbm.at[layer, expert], stage, sem.at[0])

    @pl.when(i == 0)
    def _():
        fetch(blk_e_ref[0]).start()

    @pl.when(first_ref[i] == 1)
    def _():
        fetch(blk_e_ref[i]).wait()
        w_bf16[...] = stage[...].astype(BF16)

        @pl.when(next_ref[i] >= 0)
        def _():
            fetch(next_ref[i]).start()

    @pl.when(used_ref[i] == 1)
    def _():
        acc = jnp.dot(x_ref[...], w_bf16[...], preferred_element_type=F32) + b_ref[0]
        o_ref[...] = (acc if post is None else post(acc)).astype(o_ref.dtype)

    @pl.when(used_ref[i] == 0)
    def _():
        o_ref[...] = jnp.zeros_like(o_ref)


def _expert_matmul(tables, x, w, b, layer, post, n_out, out_dtype, name):
    cap, Kd = x.shape
    E, Nd = w.shape[1], w.shape[3]
    R = MOE_ROWS
    return pl.pallas_call(
        functools.partial(_expert_kernel, layer=layer, post=post),
        grid_spec=pltpu.PrefetchScalarGridSpec(
            num_scalar_prefetch=4,
            grid=(cap // R,),
            in_specs=[
                pl.BlockSpec((R, Kd), lambda i, be, fi, nx, us: (i, 0)),
                pl.BlockSpec(memory_space=pl.ANY),
                pl.BlockSpec((None, 1, 1, Nd), lambda i, be, fi, nx, us: (layer, be[i], 0, 0)),
            ],
            out_specs=pl.BlockSpec((R, n_out), lambda i, be, fi, nx, us: (i, 0)),
            scratch_shapes=[pltpu.VMEM((Kd, Nd), F32), pltpu.VMEM((Kd, Nd), BF16),
                            pltpu.SemaphoreType.DMA((1,))],
        ),
        out_shape=jax.ShapeDtypeStruct((cap, n_out), out_dtype),
        compiler_params=_params("arbitrary"),
        name=name,
    )(*tables, x, w, b.reshape(b.shape[0], E, 1, Nd))


def _expert_up_kernel(blk_e_ref, first_ref, next_ref, used_ref, n_used_ref, tok_ref, h_hbm, w_hbm, b_ref,
                      o_ref, x_buf, x_sem, stage, w_bf16, w_sem, *, layer, R):
    i = pl.program_id(0)
    n_steps = pl.num_programs(0)

    def fetch_rows(block, par):
        for r in range(R):
            pltpu.make_async_copy(h_hbm.at[pl.ds(tok_ref[block * R + r], 1)], x_buf.at[par, pl.ds(r, 1)],
                                  x_sem.at[par]).start()

    def wait_rows(par):
        pltpu.make_async_copy(h_hbm.at[pl.ds(0, R)], x_buf.at[par], x_sem.at[par]).wait()

    def fetch_weights(expert):
        return pltpu.make_async_copy(w_hbm.at[layer, expert], stage, w_sem.at[0])

    @pl.when(i == 0)
    def _():
        fetch_weights(blk_e_ref[0]).start()
        fetch_rows(0, 0)

    @pl.when(first_ref[i] == 1)
    def _():
        fetch_weights(blk_e_ref[i]).wait()
        w_bf16[...] = stage[...].astype(BF16)

        @pl.when(next_ref[i] >= 0)
        def _():
            fetch_weights(next_ref[i]).start()

    @pl.when(used_ref[i] == 1)
    def _():
        par = i % 2
        wait_rows(par)
        fetch_rows((i + 1) % n_steps, 1 - par)
        acc = jnp.dot(x_buf[par].astype(BF16), w_bf16[...], preferred_element_type=F32) + b_ref[0]
        o_ref[...] = _clamped_swiglu(acc).astype(o_ref.dtype)

    @pl.when(used_ref[i] == 0)
    def _():
        o_ref[...] = jnp.zeros_like(o_ref)

    @pl.when(i == n_steps - 1)
    def _():
        wait_rows(n_used_ref[0] % 2)


def _expert_up(tables, n_used, tok, h, w, b, layer):
    N, D = h.shape
    E, Nd = w.shape[1], w.shape[3]
    R = MOE_ROWS
    cap = tok.shape[0]
    assert (cap // R) % 2 == 0
    n_prefetch = len(tables) + 2

    def per_expert(i, be, *_):
        return (layer, be[i], 0, 0)

    return pl.pallas_call(
        functools.partial(_expert_up_kernel, layer=layer, R=R),
        grid_spec=pltpu.PrefetchScalarGridSpec(
            num_scalar_prefetch=n_prefetch,
            grid=(cap // R,),
            in_specs=[
                pl.BlockSpec(memory_space=pl.ANY),
                pl.BlockSpec(memory_space=pl.ANY),
                pl.BlockSpec((None, 1, 1, Nd), per_expert),
            ],
            out_specs=pl.BlockSpec((R, D_EXPERT), lambda i, *_: (i, 0)),
            scratch_shapes=[pltpu.VMEM((2, R, D), F32), pltpu.SemaphoreType.DMA((2,)),
                            pltpu.VMEM((D, Nd), F32), pltpu.VMEM((D, Nd), BF16),
                            pltpu.SemaphoreType.DMA((1,))],
        ),
        out_shape=jax.ShapeDtypeStruct((cap, D_EXPERT), BF16),
        compiler_params=_params("arbitrary"),
        name="moe_up",
    )(*tables, n_used, tok, h, w, b.reshape(b.shape[0], E, 1, Nd))


def _router_kernel(x_ref, w_ref, b_ref, et_ref, p_ref, cnt_ref):
    logits = _fdot(x_ref[...], w_ref[...]) + b_ref[...]
    lane = lax.broadcasted_iota(jnp.int32, logits.shape, 1)
    vals, idxs = [], []
    for _ in range(TOP_K):
        m = jnp.max(logits, axis=-1, keepdims=True)
        idx = jnp.min(jnp.where(logits == m, lane, logits.shape[1]), axis=-1, keepdims=True)
        vals.append(m)
        idxs.append(idx)
        logits = jnp.where(lane == idx, -IMP_BIG, logits)
    es = [jnp.exp(v - vals[0]) for v in vals]
    total = sum(es)
    e_out = jnp.zeros(logits.shape, jnp.int32)
    p_out = jnp.zeros(logits.shape, F32)
    picks = jnp.zeros(logits.shape, jnp.int32)
    for k in range(TOP_K):
        e_out = jnp.where(lane == k, idxs[k], e_out)
        p_out = jnp.where(lane == k, es[k] / total, p_out)
        picks = picks + (lane == idxs[k]).astype(jnp.int32)
    et_ref[...] = e_out.T[:ROUTER_ROWS]
    p_ref[...] = p_out
    cnt_ref[0] = jnp.sum(picks, axis=0, keepdims=True)


def _route(x, w_router, b_router, layer):
    N, D = x.shape
    E, K = N_EXPERTS, TOP_K
    lanes = 128
    w_r = jnp.pad(w_router[layer], ((0, 0), (0, lanes - E)))
    b_r = jnp.concatenate([b_router[layer], jnp.full((lanes - E,), NEG_INF, F32)]).reshape(1, lanes)
    tm = min(N, 512)
    return pl.pallas_call(
        _router_kernel,
        grid=(N // tm,),
        in_specs=[pl.BlockSpec((tm, D), lambda i: (i, 0)),
                  pl.BlockSpec((D, lanes), lambda i: (0, 0)),
                  pl.BlockSpec((1, lanes), lambda i: (0, 0))],
        out_specs=[pl.BlockSpec((ROUTER_ROWS, tm), lambda i: (0, i)),
                   pl.BlockSpec((tm, lanes), lambda i: (i, 0)),
                   pl.BlockSpec((1, 1, lanes), lambda i: (i, 0, 0))],
        out_shape=[jax.ShapeDtypeStruct((ROUTER_ROWS, N), jnp.int32),
                   jax.ShapeDtypeStruct((N, lanes), F32),
                   jax.ShapeDtypeStruct((N // tm, 1, lanes), jnp.int32)],
        compiler_params=_params("parallel"),
        name="moe_router",
    )(x, w_r, b_r)


def _combine_kernel(slot_ref, out_hbm, p_ref, res_ref, gate_ref, o_ref, buf, sem, *, TT, K, N):
    i = pl.program_id(0)
    n_steps = pl.num_programs(0)

    def fetch(step, par):
        for k in range(K):
            for r in range(TT):
                row = slot_ref[k * N + step * TT + r]
                pltpu.make_async_copy(out_hbm.at[pl.ds(row, 1)], buf.at[par, pl.ds(k * TT + r, 1)],
                                      sem.at[par]).start()

    @pl.when(i == 0)
    def _():
        fetch(0, 0)

    @pl.when(i + 1 < n_steps)
    def _():
        fetch(i + 1, (i + 1) % 2)

    par = i % 2
    pltpu.make_async_copy(out_hbm.at[pl.ds(0, K * TT)], buf.at[par], sem.at[par]).wait()
    p = p_ref[...]
    y = p[:, 0:1] * buf[par, 0:TT, :]
    for k in range(1, K):
        y = y + p[:, k:k + 1] * buf[par, k * TT:(k + 1) * TT, :]
    o_ref[...] = res_ref[...] + gate_ref[0] * y


def _moe_combine(out, slot, top_w, x_res, gate, T):
    N, D = x_res.shape
    K, TT = TOP_K, MOE_COMBINE_TOKENS
    return pl.pallas_call(
        functools.partial(_combine_kernel, TT=TT, K=K, N=N),
        grid_spec=pltpu.PrefetchScalarGridSpec(
            num_scalar_prefetch=1,
            grid=(N // TT,),
            in_specs=[
                pl.BlockSpec(memory_space=pl.ANY),
                pl.BlockSpec((TT, top_w.shape[1]), lambda i, s: (i, 0)),
                pl.BlockSpec((TT, D), lambda i, s: (i, 0)),
                pl.BlockSpec((1, 1, D), lambda i, s: ((i * TT) // T, 0, 0)),
            ],
            out_specs=pl.BlockSpec((TT, D), lambda i, s: (i, 0)),
            scratch_shapes=[pltpu.VMEM((2, K * TT, D), F32), pltpu.SemaphoreType.DMA((2,))],
        ),
        out_shape=jax.ShapeDtypeStruct((N, D), F32),
        compiler_params=_params("arbitrary"),
        name="moe_combine",
    )(slot, out, top_w, x_res, gate)


def _moe_ffn(h, layer, x_res, gate, w_router, b_router, w_gu, b_gu, w_down, b_down):
    B, T, D = h.shape
    N = B * T
    E, K, R = N_EXPERTS, TOP_K, MOE_ROWS
    NK = N * K
    x = h.reshape(N, D)
    top_e_t, top_w, tile_counts = _route(x, w_router, b_router, layer)
    pair = jnp.arange(NK, dtype=jnp.int32)
    e_s, order = lax.sort((top_e_t[:K].reshape(-1), pair), num_keys=1, is_stable=True)
    counts = jnp.sum(tile_counts, axis=(0, 1))[:E]
    padded = (counts + R - 1) // R * R
    g_start = jnp.cumsum(counts) - counts
    p_end = jnp.cumsum(padded)
    shift = (p_end - padded) - g_start
    dest = pair + jnp.take(shift, e_s, axis=0)
    n_blocks = (NK + R - 1) // R + E
    cap = n_blocks * R
    experts = jnp.arange(E, dtype=jnp.int32)[None, :]
    blocks = jnp.arange(n_blocks, dtype=jnp.int32)
    blk_e = jnp.minimum(jnp.sum(p_end[None, :] <= (blocks * R)[:, None], axis=1), E - 1).astype(jnp.int32)
    blk_is = blk_e[:, None] == experts
    blk_shift = jnp.sum(jnp.where(blk_is, shift[None, :], 0), axis=1)
    used = (blocks * R < p_end[E - 1]).astype(jnp.int32)
    first = used * (blk_e != jnp.concatenate([jnp.full((1,), -1, jnp.int32), blk_e[:-1]])).astype(jnp.int32)
    after = jnp.sum(jnp.where(blk_is, p_end[None, :], 0), axis=1) // R
    after_e = jnp.sum(jnp.where(after[:, None] == blocks[None, :], blk_e[None, :], 0), axis=1)
    next_e = jnp.where(after * R < p_end[E - 1], after_e, -1).astype(jnp.int32)
    tables = (blk_e, first, next_e, used)
    src = jnp.clip(jnp.arange(cap, dtype=jnp.int32) - jnp.repeat(blk_shift, R), 0, NK - 1)
    tok = jnp.take(order, src, axis=0) % N
    n_used = jnp.sum(used, keepdims=True)
    act = _expert_up(tables, n_used, tok, x, w_gu, b_gu, layer)
    out = _expert_matmul(tables, act, w_down, b_down, layer, None, D, F32, "moe_down")

    _, slot = lax.sort((order, dest), num_keys=1)
    return _moe_combine(out, slot, top_w, x_res.reshape(N, D), gate, T).reshape(B, T, D)


def kernel(x, c, ada_w, ada_b, norm_g, rw_mu, rw_w_rkv, rw_w0, rw_w1, rw_w2, rw_a0, rw_a1, rw_a2, rw_g1, rw_g2, rw_k_k, rw_k_a, rw_r_k, rw_ln_g, rw_ln_b, rw_w_o, ret_w_in, ret_gn_g, ret_gn_b, ret_w_o, gla_w_in, gla_w_a1, gla_w_a2, gla_b_a, gla_gn_g, gla_w_o, nsa_w_in, nsa_q_g, nsa_k_g, nsa_cmp_pos, nsa_cmp_w1, nsa_cmp_w2, nsa_w_o, moe_router_w, moe_router_b, moe_w_gu, moe_b_gu, moe_w_down, moe_b_down):
    B, T, D = x.shape
    depth = ada_w.shape[0]
    c_act = jnp.pad(jax.nn.silu(c), ((0, 8 - B), (0, 0)))
    for i in range(depth):
        mod = _matmul(c_act, ada_w, (i,), tm=8, tn=1024)[:B] + ada_b[i]
        sh1, sc1, gt1, sh2, sc2, gt2 = jnp.split(mod, 6, axis=-1)
        gt1, gt2 = gt1.reshape(B, 1, D), gt2.reshape(B, 1, D)
        m, j = i % 4, i // 4
        if m == 0:
            x = _rwkv7_mix(x, norm_g[i, 0], sc1, sh1, gt1, rw_mu[j], rw_w_rkv[j:j + 1], rw_w0[j],
                           rw_w1[j:j + 1], rw_w2[j:j + 1], rw_a0[j], rw_a1[j:j + 1], rw_a2[j:j + 1],
                           rw_g1[j:j + 1], rw_g2[j:j + 1], rw_k_k[j], rw_k_a[j], rw_r_k[j], rw_ln_g[j],
                           rw_ln_b[j], rw_w_o[j:j + 1])
        elif m == 1:
            h, = _norm_modulate(x, norm_g[i, 0], sc1, sh1, (BF16,))
            x = _retention_mix(h, x, gt1, ret_w_in[j:j + 1], ret_gn_g[j], ret_gn_b[j], ret_w_o[j:j + 1])
        elif m == 2:
            h, = _norm_modulate(x, norm_g[i, 0], sc1, sh1, (BF16,))
            x = _gla_mix(h, x, gt1, gla_w_in[j:j + 1], gla_w_a1[j:j + 1], gla_w_a2[j:j + 1], gla_b_a[j],
                         gla_gn_g[j], gla_w_o[j:j + 1])
        else:
            h, = _norm_modulate(x, norm_g[i, 0], sc1, sh1, (BF16,))
            x = _nsa_mix(h, x, gt1, nsa_w_in[j:j + 1], nsa_q_g[j], nsa_k_g[j], nsa_cmp_pos[j],
                         nsa_cmp_w1[j], nsa_cmp_w2[j], nsa_w_o[j:j + 1])
        h, = _norm_modulate(x, norm_g[i, 1], sc2, sh2, (F32,))
        x = _moe_ffn(h, i, x, gt2, moe_router_w, moe_router_b, moe_w_gu, moe_b_gu, moe_w_down,
                     moe_b_down)
    return x
```

```python
import functools
import math

import jax
import jax.numpy as jnp
from jax import lax
from jax.experimental import pallas as pl
from jax.experimental.pallas import tpu as pltpu

F32 = jnp.float32
BF16 = jnp.bfloat16
HIGHEST = lax.Precision.HIGHEST

NORM_EPS = 1e-6
NEG_INF = -1e30

RW_HEAD_DIM = 64
RW_GN_EPS = 64e-5
RW_CHUNK = 128
RW_HEADS_PER_STEP = 8
RW_CHUNKS_PER_STEP = 1
RW_INV_BASE = 8

RET_HEADS = 8
RET_DK = 256
RET_DV = 512
RET_CHUNK = 128
RET_ROT_BASE = 10000.0
CHUNKS_PER_STEP = 4

GLA_HEADS = 4
GLA_DK = 256
GLA_DV = 512
GLA_GATE_NORM = 16.0
GLA_CHUNK = 64

NSA_HEADS = 16
NSA_KV_HEADS = 4
NSA_HD = 128
NSA_CMP_BLK = 32
NSA_CMP_STRIDE = 16
NSA_SEL_BLK = 64
NSA_SEL_TOPK = 16
NSA_WINDOW = 512
NSA_CMP_TQ = 256
NSA_ATT_TQ = 128
NSA_ATT_TK = 512
IMP_BIG = 3e38

N_EXPERTS = 32
TOP_K = 4
D_EXPERT = 768
SWIGLU_ALPHA = 1.702
SWIGLU_LIMIT = 7.0
MOE_ROWS = 256
MOE_COMBINE_TOKENS = 64
ROUTER_ROWS = 8

VMEM_LIMIT_BYTES = 52 * 1024 * 1024
MATMUL_VMEM_BUDGET = 48 * 1024 * 1024


def _params(*sem):
    return pltpu.CompilerParams(dimension_semantics=sem, vmem_limit_bytes=VMEM_LIMIT_BYTES)


def _bdot(a, b):
    return jnp.dot(a.astype(BF16), b.astype(BF16), preferred_element_type=F32)


def _bdot_nt(a, b):
    return lax.dot_general(a.astype(BF16), b.astype(BF16), (((1,), (1,)), ((), ())),
                           preferred_element_type=F32)


def _bdot_tn(a, b):
    return lax.dot_general(a.astype(BF16), b.astype(BF16), (((0,), (0,)), ((), ())),
                           preferred_element_type=F32)


def _fdot(a, b):
    return jnp.dot(a, b, precision=HIGHEST, preferred_element_type=F32)


def _fdot_nt(a, b):
    return lax.dot_general(a, b, (((1,), (1,)), ((), ())), precision=HIGHEST,
                           preferred_element_type=F32)


def _split3(x):
    hi = x.astype(BF16)
    rem = x - hi.astype(F32)
    mid = rem.astype(BF16)
    return hi, mid, (rem - mid.astype(F32)).astype(BF16)


def _exact_lhs_dot(m, x):
    mb = m.astype(BF16)
    return sum(jnp.dot(mb, part, preferred_element_type=F32) for part in _split3(x))


def _exact_rhs_dot(x, m):
    mb = m.astype(BF16)
    return sum(jnp.dot(part, mb, preferred_element_type=F32) for part in _split3(x))


def _softplus(z):
    return jnp.maximum(z, 0.0) + jnp.log(1.0 + jnp.exp(-jnp.abs(z)))


_POST = {
    None: lambda z: z,
    "tanh": jnp.tanh,
    "sigmoid": jax.nn.sigmoid,
    "silu": jax.nn.silu,
    "rwkv_log_decay": lambda z: -jnp.exp(-_softplus(-z) - 0.5),
    "gla_log_gate": lambda z: -_softplus(-z) / GLA_GATE_NORM,
}


def _mm_kernel(*refs, has_bias, has_res, post):
    x_ref, w_ref = refs[0], refs[1]
    pos = 2
    acc = _bdot(x_ref[...], w_ref[...])
    if has_bias:
        acc = acc + refs[pos][...]
        pos += 1
    acc = _POST[post](acc)
    if has_res:
        acc = refs[pos][...] + refs[pos + 1][0] * acc
        pos += 2
    o_ref = refs[pos]
    o_ref[...] = acc.astype(o_ref.dtype)


def _matmul(x, w, lead=(), *, bias=None, res=None, gate=None, n_cols=None, tm=None, tn=None,
            post=None, out_dtype=F32, name="matmul"):
    M, K = x.shape
    N = n_cols if n_cols is not None else w.shape[-1]
    if tm is None:
        tm = min(M, 1024 if x.dtype == BF16 else 512)
    if tn is None:
        def vmem_bytes(cols):
            window = tm * K * x.dtype.itemsize + K * cols * 4 + tm * cols * jnp.dtype(out_dtype).itemsize
            window += tm * cols * 4 if res is not None else 0
            return 2 * window + K * cols * 2 + tm * cols * 4
        fits = [c for c in (1024, 512, 256, 128) if N % c == 0 and vmem_bytes(c) <= MATMUL_VMEM_BUDGET]
        tn = fits[0] if fits else N
    assert M % tm == 0
    nlead = len(lead)
    in_specs = [
        pl.BlockSpec((tm, K), lambda i, j: (i, 0)),
        pl.BlockSpec((None,) * nlead + (K, tn), lambda i, j: tuple(lead) + (0, j)),
    ]
    args = [x, w]
    if bias is not None:
        in_specs.append(pl.BlockSpec((1, tn), lambda i, j: (0, j)))
        args.append(bias)
    if res is not None:
        rows_per_gate = M // gate.shape[0]
        assert rows_per_gate % tm == 0
        in_specs.append(pl.BlockSpec((tm, tn), lambda i, j: (i, j)))
        in_specs.append(pl.BlockSpec((1, 1, tn), lambda i, j: ((i * tm) // rows_per_gate, 0, j)))
        args += [res, gate]
    return pl.pallas_call(
        functools.partial(_mm_kernel, has_bias=bias is not None, has_res=res is not None, post=post),
        grid=(M // tm, pl.cdiv(N, tn)),
        in_specs=in_specs,
        out_specs=pl.BlockSpec((tm, tn), lambda i, j: (i, j)),
        out_shape=jax.ShapeDtypeStruct((M, N), out_dtype),
        compiler_params=_params("parallel", "parallel"),
        name=name,
    )(*args)


def _modulated_norm(x, g, sc, sh):
    y = x * lax.rsqrt(jnp.mean(x * x, axis=-1, keepdims=True) + NORM_EPS) * g
    return y * (1.0 + sc) + sh


def _normmod_kernel(x_ref, g_ref, sc_ref, sh_ref, *o_refs):
    h = _modulated_norm(x_ref[0], g_ref[...], sc_ref[0], sh_ref[0])
    for o_ref in o_refs:
        o_ref[0] = h.astype(o_ref.dtype)


def _norm_modulate(x, g, sc, sh, out_dtypes):
    B, T, D = x.shape
    tr = min(T, 512)
    row = pl.BlockSpec((1, tr, D), lambda b, i: (b, i, 0))
    per_batch = pl.BlockSpec((1, 1, D), lambda b, i: (b, 0, 0))
    return pl.pallas_call(
        _normmod_kernel,
        grid=(B, T // tr),
        in_specs=[row, pl.BlockSpec((1, D), lambda b, i: (0, 0)), per_batch, per_batch],
        out_specs=[row] * len(out_dtypes),
        out_shape=[jax.ShapeDtypeStruct((B, T, D), dt) for dt in out_dtypes],
        compiler_params=_params("parallel", "parallel"),
        name="norm_modulate",
    )(x, g.reshape(1, D), sc.reshape(B, 1, D), sh.reshape(B, 1, D))


def _rwkv_prep_kernel(x_ref, halo_ref, g_ref, sc_ref, sh_ref, mu_ref, *o_refs, halo_rows):
    h = _modulated_norm(x_ref[0], g_ref[...], sc_ref[0], sh_ref[0])
    h_halo = _modulated_norm(halo_ref[0], g_ref[...], sc_ref[0], sh_ref[0])
    first = jnp.where(pl.program_id(1) > 0, h_halo[halo_rows - 1:halo_rows, :], 0.0)
    row = lax.broadcasted_iota(jnp.int32, h.shape, 0)
    d = jnp.where(row == 0, first, pltpu.roll(h, 1, axis=0)) - h
    for j, o_ref in enumerate(o_refs):
        o_ref[0] = (h + d * mu_ref[j:j + 1, :]).astype(o_ref.dtype)


def _rwkv_prep(x, g, sc, sh, mu):
    B, T, D = x.shape
    tr = min(T, 512)
    hr = 8
    n_mix = mu.shape[0]
    row = pl.BlockSpec((1, tr, D), lambda b, i: (b, i, 0))
    per_batch = pl.BlockSpec((1, 1, D), lambda b, i: (b, 0, 0))
    return pl.pallas_call(
        functools.partial(_rwkv_prep_kernel, halo_rows=hr),
        grid=(B, T // tr),
        in_specs=[row,
                  pl.BlockSpec((1, hr, D), lambda b, i: (b, jnp.maximum(i * (tr // hr) - 1, 0), 0)),
                  pl.BlockSpec((1, D), lambda b, i: (0, 0)), per_batch, per_batch,
                  pl.BlockSpec((n_mix, D), lambda b, i: (0, 0))],
        out_specs=[row] * n_mix,
        out_shape=[jax.ShapeDtypeStruct((B, T, D), BF16)] * n_mix,
        compiler_params=_params("parallel", "parallel"),
        name="rwkv_shift_mix",
    )(x, x, g.reshape(1, D), sc.reshape(B, 1, D), sh.reshape(B, 1, D), mu)


def _rwkv_kernel(*refs, C, N, HB, SUB):
    h_ref = refs[-1]

    @pl.when(pl.program_id(2) == 0)
    def _():
        h_ref[...] = jnp.zeros_like(h_ref)

    for s in range(SUB):
        _rwkv_chunk(slice(s * C, (s + 1) * C), *refs, C=C, N=N, HB=HB)


def _rwkv_chunk(rows, r_ref, k_ref, v_ref, lw_ref, a_ref, g_ref, kk_ref, ka_ref, rk_ref, lng_ref,
                lnb_ref, o_ref, h_ref, *, C, N, HB):
    row = lax.broadcasted_iota(jnp.int32, (C, C), 0)
    col = lax.broadcasted_iota(jnp.int32, (C, C), 1)
    incl = row >= col
    strict = row > col
    eye = (row == col).astype(F32)
    eye_n = (lax.broadcasted_iota(jnp.int32, (N, N), 0) == lax.broadcasted_iota(jnp.int32, (N, N), 1))
    heads = range(HB)

    def head(x, i):
        return x[:, i * N:(i + 1) * N]

    r_all, k_all, v_all, a_all, lw_all = (ref[0, rows, :] for ref in (r_ref, k_ref, v_ref, a_ref, lw_ref))
    cum_all = _exact_lhs_dot(incl.astype(F32), lw_all)
    tot_all = cum_all[C - 1:C, :]
    mid_all = cum_all[C // 2 - 1:C // 2, :]
    cen_all = cum_all - mid_all
    e_neg = jnp.exp(-cen_all)
    e_tail = jnp.exp(tot_all - cum_all)
    kx_all = k_all * kk_ref[...]
    kp_all = k_all * (1.0 + (a_all - 1.0) * ka_ref[...])
    rt_all = r_all * jnp.exp(cen_all)
    kn_all = kp_all * e_neg
    kt_all = kp_all * e_tail
    ep_all = jnp.exp(cen_all - lw_all)
    rkr_all = r_all * kp_all * rk_ref[...]

    def column(row_vec):
        return jnp.sum(jnp.where(eye_n, row_vec, 0.0), axis=1, keepdims=True)

    kappa = [head(kx_all, i) / jnp.maximum(
        jnp.sqrt(jnp.sum(jnp.square(head(kx_all, i)), axis=-1, keepdims=True)), 1e-12) for i in heads]
    b = [kappa[i] * head(a_all, i) for i in heads]
    kap_t = [kappa[i] * head(ep_all, i) for i in heads]
    r_t = [head(rt_all, i) for i in heads]
    v = [head(v_all, i) for i in heads]
    big = [_bdot_nt(jnp.concatenate([kap_t[i], r_t[i]], axis=0),
                    jnp.concatenate([b[i] * head(e_neg, i), head(kn_all, i)], axis=0))
           for i in heads]
    t_k = [jnp.where(strict, big[i][:C, C:], 0.0) for i in heads]
    m_b = [jnp.where(incl, big[i][C:, :C], 0.0) for i in heads]
    m_k = [jnp.where(incl, big[i][C:, C:], 0.0) for i in heads]
    t_b = [jnp.where(strict, big[i][:C, :C], 0.0) for i in heads]
    base = RW_INV_BASE
    p = [jnp.where((row // base) == (col // base), -t_b[i], 0.0) for i in heads]
    inv = [eye + p[i] for i in heads]
    tkv = [_bdot(t_k[i], v[i]) for i in heads]
    for _ in range(int(math.log2(base)) - 1):
        p = [_bdot(p[i], p[i]) for i in heads]
        inv = [inv[i] + _bdot(inv[i], p[i]) for i in heads]
    size = base
    while size < C:
        couple = ((row // (2 * size)) == (col // (2 * size))) & ((row // size) != (col // size))
        lower = [jnp.where(couple, t_b[i], 0.0) for i in heads]
        inv = [inv[i] - _bdot(inv[i], _bdot(lower[i], inv[i])) for i in heads]
        size *= 2
    h0 = [h_ref[i] for i in heads]
    h0c = [h0[i] * jnp.exp(column(head(mid_all, i))) for i in heads]
    aw = [_bdot(inv[i], jnp.concatenate([kap_t[i], tkv[i]], axis=1)) for i in heads]
    u = [_bdot(aw[i][:, :N], h0c[i]) + aw[i][:, N:] for i in heads]
    y = [_bdot(jnp.concatenate([r_t[i], m_k[i], -m_b[i]], axis=1),
               jnp.concatenate([h0c[i], v[i], u[i]], axis=0)) for i in heads]
    for i in heads:
        h_ref[i] = jnp.exp(column(head(tot_all, i))) * h0[i] + _bdot_tn(
            jnp.concatenate([head(kt_all, i), -(b[i] * head(e_tail, i))], axis=0),
            jnp.concatenate([v[i], u[i]], axis=0))
    outs = []
    for i in heads:
        mu = jnp.mean(y[i], axis=-1, keepdims=True)
        var = jnp.mean(jnp.square(y[i] - mu), axis=-1, keepdims=True)
        yn = (y[i] - mu) * lax.rsqrt(var + RW_GN_EPS)
        outs.append(yn * head(lng_ref[...], i) + head(lnb_ref[...], i)
                    + jnp.sum(head(rkr_all, i), axis=-1, keepdims=True) * v[i])
    o_ref[0, rows, :] = (jnp.concatenate(outs, axis=-1) * g_ref[0, rows, :]).astype(o_ref.dtype)


def _rwkv_core(r, k, v, lw, a, g, k_k, k_a, r_k, ln_g, ln_b):
    B, T, D = r.shape
    C, N, HB, SUB = RW_CHUNK, RW_HEAD_DIM, RW_HEADS_PER_STEP, RW_CHUNKS_PER_STEP
    W = HB * N
    seq = pl.BlockSpec((1, SUB * C, W), lambda b, h, c: (b, c, h))
    par = pl.BlockSpec((1, W), lambda b, h, c: (0, h))
    return pl.pallas_call(
        functools.partial(_rwkv_kernel, C=C, N=N, HB=HB, SUB=SUB),
        grid=(B, D // W, T // (SUB * C)),
        in_specs=[seq] * 6 + [par] * 5,
        out_specs=seq,
        out_shape=jax.ShapeDtypeStruct((B, T, D), BF16),
        scratch_shapes=[pltpu.VMEM((HB, N, N), F32)],
        compiler_params=_params("parallel", "parallel", "arbitrary"),
        name="rwkv_core",
    )(r, k, v, lw, a, g, k_k.reshape(1, D), k_a.reshape(1, D), r_k.reshape(1, D),
      ln_g.reshape(1, D), ln_b.reshape(1, D))


def _rwkv7_mix(x_res, norm_g, sc, shift, gate, mu, w_rkv, w0, w1, w2, a0, a1, a2, g1, g2, k_k, k_a,
               r_k, ln_g, ln_b, w_o):
    B, T, D = x_res.shape
    M = B * T
    xr, xk, xv, xw, xa, xg = [z.reshape(M, D) for z in _rwkv_prep(x_res, norm_g, sc, shift, mu)]
    r = _matmul(xr, w_rkv, (0, 0), name="rwkv_r")
    k = _matmul(xk, w_rkv, (0, 1), name="rwkv_k")
    v = _matmul(xv, w_rkv, (0, 2), name="rwkv_v")
    log_decay = _matmul(_matmul(xw, w1, (0,), post="tanh", out_dtype=BF16), w2, (0,),
                        bias=w0.reshape(1, D), post="rwkv_log_decay", name="rwkv_decay")
    a = _matmul(_matmul(xa, a1, (0,), out_dtype=BF16), a2, (0,), bias=a0.reshape(1, D),
                post="sigmoid", name="rwkv_a")
    g = _matmul(_matmul(xg, g1, (0,), post="sigmoid", out_dtype=BF16), g2, (0,), name="rwkv_g")
    sh = (B, T, D)
    o = _rwkv_core(r.reshape(sh), k.reshape(sh), v.reshape(sh), log_decay.reshape(sh),
                   a.reshape(sh), g.reshape(sh), k_k, k_a, r_k, ln_g, ln_b)
    return _matmul(o.reshape(M, D), w_o, (0,), res=x_res.reshape(M, D), gate=gate).reshape(sh)


def _ret_kernel(q_ref, k_ref, v_ref, g_ref, cos_ref, sin_ref, dm_ref, xz_ref, gng_ref, gnb_ref,
                o_ref, s_ref, *, dk, C, SUB):
    @pl.when(pl.program_id(2) == 0)
    def _():
        s_ref[...] = jnp.zeros_like(s_ref)

    half = dk // 2
    xi, zeta, g_chunk = xz_ref[0, :, 0:1], xz_ref[0, :, 1:2], xz_ref[0, 0:1, 2:3]
    for s in range(SUB):
        rows = slice(s * C, (s + 1) * C)
        cos, sin = cos_ref[rows, :], sin_ref[rows, :]

        def rot(z):
            z1, z2 = z[:, :half], z[:, half:]
            return jnp.concatenate([z1 * cos - z2 * sin, z1 * sin + z2 * cos], axis=-1)

        q = rot(q_ref[0, rows, :])
        k = rot(k_ref[0, rows, :] * (dk ** -0.5))
        v = v_ref[0, rows, :]
        scores = _bdot_nt(q, k) * dm_ref[0]
        state = s_ref[...]
        o = _bdot(scores, v) + _bdot(q * xi, state)
        s_ref[...] = g_chunk * state + _bdot_tn(k * zeta, v)
        mu = jnp.mean(o, axis=-1, keepdims=True)
        var = jnp.mean(jnp.square(o - mu), axis=-1, keepdims=True)
        y = (o - mu) * lax.rsqrt(var + NORM_EPS) * gng_ref[0] + gnb_ref[0]
        gate = g_ref[0, rows, :]
        o_ref[0, rows, :] = (y * (gate * jax.nn.sigmoid(gate))).astype(o_ref.dtype)


def _retention_mix(h, x_res, gate, w_in, gn_g, gn_b, w_o):
    B, T, D = h.shape
    M = B * T
    H, dk, dv, C = RET_HEADS, RET_DK, RET_DV, RET_CHUNK
    SUB = min(CHUNKS_PER_STEP, T // C)
    R = SUB * C
    proj =_matmul(h.reshape(M, D), w_in, (0,)).reshape(B, T, H * (2 * dk + 2 * dv))
    qb, kb, vb, gb = 0, (H * dk) // dk, (2 * H * dk) // dv, (2 * H * dk + H * dv) // dv
    theta = 1.0 / (RET_ROT_BASE ** jnp.linspace(0.0, 1.0, dk // 2, dtype=F32))
    ang = jnp.arange(T, dtype=F32)[:, None] * theta[None, :]
    log_gamma = jnp.log(1.0 - 2.0 ** (-5.0 - jnp.arange(H, dtype=F32)))
    pos = jnp.arange(C, dtype=F32)
    rel = pos[:, None] - pos[None, :]
    dmask = jnp.where(rel >= 0, jnp.exp(jnp.maximum(rel, 0.0) * log_gamma[:, None, None]), 0.0)
    xi = jnp.exp((pos + 1.0)[None, :] * log_gamma[:, None])
    zeta = jnp.exp((C - 1.0 - pos)[None, :] * log_gamma[:, None])
    g_chunk = jnp.broadcast_to(jnp.exp(C * log_gamma)[:, None], (H, C))
    xz = jnp.concatenate([jnp.stack([xi, zeta, g_chunk], axis=-1), jnp.zeros((H, C, 125), F32)], axis=-1)
    o = pl.pallas_call(
        functools.partial(_ret_kernel, dk=dk, C=C, SUB=SUB),
        grid=(B, H, T // R),
        in_specs=[
            pl.BlockSpec((1, R, dk), lambda b, h, c: (b, c, qb + h)),
            pl.BlockSpec((1, R, dk), lambda b, h, c: (b, c, kb + h)),
            pl.BlockSpec((1, R, dv), lambda b, h, c: (b, c, vb + h)),
            pl.BlockSpec((1, R, dv), lambda b, h, c: (b, c, gb + h)),
            pl.BlockSpec((R, dk // 2), lambda b, h, c: (c, 0)),
            pl.BlockSpec((R, dk // 2), lambda b, h, c: (c, 0)),
            pl.BlockSpec((1, C, C), lambda b, h, c: (h, 0, 0)),
            pl.BlockSpec((1, C, 128), lambda b, h, c: (h, 0, 0)),
            pl.BlockSpec((1, 1, dv), lambda b, h, c: (h, 0, 0)),
            pl.BlockSpec((1, 1, dv), lambda b, h, c: (h, 0, 0)),
        ],
        out_specs=pl.BlockSpec((1, R, dv), lambda b, h, c: (b, c, h)),
        out_shape=jax.ShapeDtypeStruct((B, T, H * dv), BF16),
        scratch_shapes=[pltpu.VMEM((dk, dv), F32)],
        compiler_params=_params("parallel", "parallel", "arbitrary"),
        name="retention_core",
    )(proj, proj, proj, proj, jnp.cos(ang), jnp.sin(ang), dmask, xz,
      gn_g.reshape(H, 1, dv), gn_b.reshape(H, 1, dv))
    return _matmul(o.reshape(M, H * dv), w_o, (0,), res=x_res.reshape(M, D), gate=gate).reshape(B, T, D)


def _gla_kernel(q_ref, k_ref, v_ref, g_ref, la_ref, gn_ref, o_ref, s_ref, *, C, dk, SUB):
    @pl.when(pl.program_id(2) == 0)
    def _():
        s_ref[...] = jnp.zeros_like(s_ref)

    row = lax.broadcasted_iota(jnp.int32, (C, C), 0)
    col = lax.broadcasted_iota(jnp.int32, (C, C), 1)
    causal = row >= col
    eye_k = (lax.broadcasted_iota(jnp.int32, (dk, dk), 0) == lax.broadcasted_iota(jnp.int32, (dk, dk), 1))
    for s in range(SUB):
        rows = slice(s * C, (s + 1) * C)
        la = la_ref[0, rows, :]
        b = _exact_lhs_dot(causal.astype(F32), la)
        b_last = b[C - 1:C, :]
        d_last_col = jnp.exp(jnp.sum(jnp.where(eye_k, b_last, 0.0), axis=1, keepdims=True))
        k, v = k_ref[0, rows, :], v_ref[0, rows, :]
        q_in = q_ref[0, rows, :] * (dk ** -0.5) * jnp.exp(b)
        k_in = k * jnp.exp(-b)
        att = jnp.where(causal, _bdot_nt(q_in, k_in), 0.0)
        state = s_ref[...]
        o = _bdot(att, v) + _bdot(q_in, state)
        s_ref[...] = d_last_col * state + _bdot_tn(k * jnp.exp(b_last - b), v)
        y = o * lax.rsqrt(jnp.mean(o * o, axis=-1, keepdims=True) + NORM_EPS) * gn_ref[...]
        gate = g_ref[0, rows, :]
        o_ref[0, rows, :] = (y * (gate * jax.nn.sigmoid(gate))).astype(o_ref.dtype)


def _gla_mix(h, x_res, gate, w_in, w_a1, w_a2, b_a, gn_g, w_o):
    B, T, D = h.shape
    M = B * T
    H, dk, dv, C = GLA_HEADS, GLA_DK, GLA_DV, GLA_CHUNK
    SUB = min(CHUNKS_PER_STEP, T // C)
    R = SUB * C
    h2 = h.reshape(M, D)
    proj = _matmul(h2, w_in, (0,)).reshape(B, T, H * (2 * dk + 2 * dv))
    log_a = _matmul(_matmul(h2, w_a1, (0,), out_dtype=BF16), w_a2, (0,), bias=b_a.reshape(1, H * dk),
                    post="gla_log_gate", name="gla_log_gate").reshape(B, T, H * dk)
    qb, kb, vb, gb = 0, H, (2 * H * dk) // dv, (2 * H * dk + H * dv) // dv
    o = pl.pallas_call(
        functools.partial(_gla_kernel, C=C, dk=dk, SUB=SUB),
        grid=(B, H, T // R),
        in_specs=[
            pl.BlockSpec((1, R, dk), lambda b, h, c: (b, c, qb + h)),
            pl.BlockSpec((1, R, dk), lambda b, h, c: (b, c, kb + h)),
            pl.BlockSpec((1, R, dv), lambda b, h, c: (b, c, vb + h)),
            pl.BlockSpec((1, R, dv), lambda b, h, c: (b, c, gb + h)),
            pl.BlockSpec((1, R, dk), lambda b, h, c: (b, c, h)),
            pl.BlockSpec((1, dv), lambda b, h, c: (0, 0)),
        ],
        out_specs=pl.BlockSpec((1, R, dv), lambda b, h, c: (b, c, h)),
        out_shape=jax.ShapeDtypeStruct((B, T, H * dv), BF16),
        scratch_shapes=[pltpu.VMEM((dk, dv), F32)],
        compiler_params=_params("parallel", "parallel", "arbitrary"),
        name="gla_core",
    )(proj, proj, proj, proj, log_a, gn_g.reshape(1, dv))
    return _matmul(o.reshape(M, H * dv), w_o, (0,), res=x_res.reshape(M, D), gate=gate).reshape(B, T, D)


def _nsa_cmp_kernel(q_ref, kc_ref, vc_ref, o_ref, sel_ref, *, tq, n_cmp, n_pad, n_slc, hpg):
    L, S, Ls, hd = NSA_CMP_BLK, NSA_CMP_STRIDE, NSA_SEL_BLK, NSA_HD
    t_pos = pl.program_id(2) * tq + lax.broadcasted_iota(jnp.int32, (tq, 1), 0)
    n_ix = lax.broadcasted_iota(jnp.int32, (1, n_pad), 1)
    valid = (n_ix * S + (L - 1) <= t_pos) & (n_ix < n_cmp)
    validf = valid.astype(F32)
    c_start = lax.broadcasted_iota(jnp.int32, (n_pad, n_slc), 0) * S
    s_start = lax.broadcasted_iota(jnp.int32, (n_pad, n_slc), 1) * Ls
    overlap = ((c_start <= s_start + (Ls - 1)) & (c_start + (L - 1) >= s_start)
               & (c_start < n_cmp * S)).astype(F32)
    kc, vc = kc_ref[0, 0], vc_ref[0, 0]
    kc_hi = kc.astype(BF16)
    kc_lo = (kc - kc_hi.astype(F32)).astype(BF16)
    heads = range(hpg)
    q = [q_ref[0, :, hh * hd:(hh + 1) * hd] for hh in heads]
    q_hi = [q[hh].astype(BF16) for hh in heads]
    q_lo = [(q[hh] - q_hi[hh].astype(F32)).astype(BF16) for hh in heads]
    s = [_bdot_nt(q_hi[hh], kc_hi) + _bdot_nt(q_hi[hh], kc_lo) + _bdot_nt(q_lo[hh], kc_hi) for hh in heads]
    s = [jnp.where(valid, s[hh], NEG_INF) for hh in heads]
    e = [jnp.exp(s[hh] - jnp.max(s[hh], axis=-1, keepdims=True)) for hh in heads]
    p = [e[hh] / jnp.sum(e[hh], axis=-1, keepdims=True) * validf for hh in heads]
    o_ref[0] = jnp.concatenate([_bdot(p[hh], vc) for hh in heads], axis=-1)
    imp = _exact_rhs_dot(sum(p), overlap)

    imp = imp.T
    j = lax.broadcasted_iota(jnp.int32, (n_slc, 1), 0)
    cur = (pl.program_id(2) * tq + lax.broadcasted_iota(jnp.int32, (1, tq), 1)) // Ls
    forced = (j == 0) | (j == cur) | (j == cur - 1)
    imp = jnp.where(j > cur, -IMP_BIG, jnp.where(forced, IMP_BIG, imp))
    rank = jnp.zeros((n_slc, tq), jnp.int32)
    for jp in range(n_slc):
        c = imp[jp:jp + 1, :]
        rank = rank + ((c > imp) | ((c == imp) & (jp < j))).astype(jnp.int32)
    sel_ref[0, 0] = (rank < min(NSA_SEL_TOPK, n_slc)).astype(F32)


def _nsa_att_kernel(qt_ref, ks_ref, vst_ref, kw_ref, vwt_ref, sel_ref, oc_ref, gt_ref, gtt_ref, o_ref,
                    m_ref, l_ref, acc_ref, *, tq, tk, T, n_slc, hpg):
    Ls, W, hd = NSA_SEL_BLK, NSA_WINDOW, NSA_HD
    qi = pl.program_id(2)
    t0 = qi * tq
    heads = range(hpg)
    cols = [slice(hh * tq, (hh + 1) * tq) for hh in heads]
    qt = jnp.concatenate([qt_ref[0, hh * hd:(hh + 1) * hd, :] for hh in heads], axis=1)
    t_q = t0 + lax.broadcasted_iota(jnp.int32, (1, tq), 1)
    sel_t = sel_ref[0, 0]

    m_ref[...] = jnp.full_like(m_ref, NEG_INF)
    l_ref[...] = jnp.zeros_like(l_ref)
    acc_ref[...] = jnp.zeros_like(acc_ref)

    pad = ks_ref.shape[2] - hd - n_slc
    sel_bias = jnp.where(sel_t > 0.5, 0.0, NEG_INF).astype(BF16)
    q_aug = jnp.concatenate([qt, jnp.concatenate([sel_bias] * hpg, axis=1),
                             jnp.zeros((pad, hpg * tq), BF16)], axis=0)

    def key_tile(kb, causal):
        k0 = pl.multiple_of(kb * tk, tk)
        vt_t = vst_ref[0, :, pl.ds(k0, tk)]
        s_all = jnp.dot(ks_ref[0, pl.ds(k0, tk), :], q_aug, preferred_element_type=F32)
        kpos = k0 + lax.broadcasted_iota(jnp.int32, (tk, 1), 0)
        ps = []
        for hh in heads:
            s = s_all[:, cols[hh]]
            if causal:
                s = jnp.where(kpos <= t_q, s, NEG_INF)
            m_old = m_ref[:, cols[hh]]
            m_new = jnp.maximum(m_old, jnp.max(s, axis=0, keepdims=True))
            alpha = jnp.exp2(m_old - m_new)
            p = jnp.exp2(s - m_new)
            l_ref[:, cols[hh]] = alpha * l_ref[:, cols[hh]] + jnp.sum(p, axis=0, keepdims=True)
            acc_ref[:, cols[hh]] = alpha * acc_ref[:, cols[hh]]
            m_ref[:, cols[hh]] = m_new
            ps.append(p.astype(BF16))
        acc_ref[...] += jnp.dot(vt_t, jnp.concatenate(ps, axis=1), preferred_element_type=F32)

    n_full = t0 // tk

    def full_tile(kb, carry):
        key_tile(kb, causal=False)
        return carry

    lax.fori_loop(0, n_full, full_tile, 0)
    key_tile(n_full, causal=True)
    o_sel_t = acc_ref[...] / l_ref[...]

    span = W + tq
    w0 = pl.multiple_of(jnp.maximum(t0 - W, 0), tq)
    kw = kw_ref[0, pl.ds(w0, span), :]
    vw_t = vwt_ref[0, :, pl.ds(w0, span)]
    wpos = w0 + lax.broadcasted_iota(jnp.int32, (span, 1), 0)
    bias_w = jnp.where((wpos <= t_q) & (wpos > t_q - W), 0.0, NEG_INF)
    s_all = jnp.dot(kw, qt, preferred_element_type=F32)
    es, ls = [], []
    for hh in heads:
        s = s_all[:, cols[hh]] + bias_w
        e = jnp.exp2(s - jnp.max(s, axis=0, keepdims=True))
        ls.append(jnp.sum(e, axis=0, keepdims=True))
        es.append(e.astype(BF16))
    o_win_t = (jnp.dot(vw_t, jnp.concatenate(es, axis=1), preferred_element_type=F32)
               / jnp.concatenate(ls, axis=1))

    gt = jax.nn.sigmoid(gt_ref[0, 0])
    gt_t = jax.nn.sigmoid(gtt_ref[0, 0])
    outs = []
    for hh in heads:
        mixed_t = (gt_t[hpg + hh:hpg + hh + 1, :] * o_sel_t[:, cols[hh]]
                   + gt_t[2 * hpg + hh:2 * hpg + hh + 1, :] * o_win_t[:, cols[hh]])
        outs.append(gt[:, hh:hh + 1] * oc_ref[0, :, hh * hd:(hh + 1) * hd] + mixed_t.T)
    o_ref[0] = jnp.concatenate(outs, axis=-1).astype(o_ref.dtype)


def _rms_norm(x, g):
    return x * lax.rsqrt(jnp.mean(x * x, axis=-1, keepdims=True) + NORM_EPS) * g


def _nsa_mix(h, x_res, gate, w_in, q_g, k_g, cmp_pos, cmp_w1, cmp_w2, w_o):
    B, T, D = h.shape
    M = B * T
    H, G, hd = NSA_HEADS, NSA_KV_HEADS, NSA_HD
    hpg = H // G
    L, S, Ls = NSA_CMP_BLK, NSA_CMP_STRIDE, NSA_SEL_BLK
    kvw = G * hd
    n_main = H * hd + 6 * kvw
    h2 = h.reshape(M, D)
    proj = _matmul(h2, w_in, (0,), n_cols=n_main)
    gate_w = jnp.pad(w_in[0, :, n_main:], ((0, 0), (0, 128 - 3 * H)))
    gates = _matmul(h2, gate_w)[:, :3 * H]
    q = _rms_norm(proj[:, :H * hd].reshape(B, T, H, hd), q_g) * (hd ** -0.5)
    q = q.reshape(B, T, H * hd)

    def kv(i):
        return proj[:, H * hd + i * kvw:H * hd + (i + 1) * kvw].reshape(B, T, G, hd)

    n_cmp = (T - L) // S + 1
    n_grp = T // S
    assert L == 2 * S and n_cmp == n_grp - 1

    def compress(z, pos, w1, w2):
        zg = z.reshape(B, n_grp, S, G, hd).transpose(0, 3, 1, 2, 4).reshape(B * G * n_grp, S * hd)
        w_halves = jnp.concatenate([w1[:S * hd], w1[S * hd:]], axis=1)
        part = _matmul(zg, w_halves, tm=min(512, B * G * n_grp)).reshape(B, G, n_grp, 2 * hd)
        pos_term = _matmul(jnp.broadcast_to(pos.reshape(1, L * hd), (8, L * hd)), w1, tm=8)[0]
        pre = part[:, :, :-1, :hd] + part[:, :, 1:, hd:] + pos_term
        pre = jnp.pad(pre, ((0, 0), (0, 0), (0, 1), (0, 0))).reshape(B * G * n_grp, hd)
        return _matmul(jax.nn.silu(pre), w2, tm=min(512, B * G * n_grp)).reshape(B, G, n_grp, hd)

    kc = _rms_norm(compress(kv(0), cmp_pos[0], cmp_w1[0], cmp_w2[0]), k_g[0])
    vc = compress(kv(1), cmp_pos[1], cmp_w1[1], cmp_w2[1])
    k_feat = 2 * hd
    blk_one_hot = (jnp.arange(T)[:, None] // Ls == jnp.arange(k_feat - hd)[None, :]).astype(BF16)
    k_sel = jnp.concatenate([_rms_norm(kv(2), k_g[1]).astype(BF16),
                             jnp.broadcast_to(blk_one_hot[None, :, None, :], (B, T, G, k_feat - hd))],
                            axis=-1).reshape(B, T, G * k_feat)
    v_sel = kv(3).reshape(B, T, kvw).astype(BF16)
    k_win = _rms_norm(kv(4), k_g[2]).reshape(B, T, kvw).astype(BF16)
    v_win = kv(5).reshape(B, T, kvw).astype(BF16)

    n_slc = T // Ls
    tq = min(NSA_CMP_TQ, T)
    o_cmp, sel = pl.pallas_call(
        functools.partial(_nsa_cmp_kernel, tq=tq, n_cmp=n_cmp, n_pad=n_grp, n_slc=n_slc, hpg=hpg),
        grid=(B, G, T // tq),
        in_specs=[
            pl.BlockSpec((1, tq, hpg * hd), lambda b, g, i: (b, i, g)),
            pl.BlockSpec((1, 1, n_grp, hd), lambda b, g, i: (b, g, 0, 0)),
            pl.BlockSpec((1, 1, n_grp, hd), lambda b, g, i: (b, g, 0, 0)),
        ],
        out_specs=[
            pl.BlockSpec((1, tq, hpg * hd), lambda b, g, i: (b, i, g)),
            pl.BlockSpec((1, 1, n_slc, tq), lambda b, g, i: (b, g, 0, i)),
        ],
        out_shape=[jax.ShapeDtypeStruct((B, T, H * hd), F32),
                   jax.ShapeDtypeStruct((B, G, n_slc, T), F32)],
        compiler_params=_params("parallel", "parallel", "parallel"),
        name="nsa_compressed",
    )(q, kc, vc)

    gt = gates.reshape(B, T, 3, G, hpg).transpose(0, 3, 1, 2, 4).reshape(B, G, T, 3 * hpg)
    tq = min(NSA_ATT_TQ, T)
    tk = min(NSA_ATT_TK, T)
    assert T >= NSA_WINDOW + tq
    k_spec = pl.BlockSpec((1, T, hd), lambda b, g, i: (b, 0, g))
    vt_spec = pl.BlockSpec((1, hd, T), lambda b, g, i: (b, g, 0))

    def feature_major(z):
        return jnp.swapaxes(z, 1, 2)

    o = pl.pallas_call(
        functools.partial(_nsa_att_kernel, tq=tq, tk=tk, T=T, n_slc=n_slc, hpg=hpg),
        grid=(B, G, T // tq),
        in_specs=[
            pl.BlockSpec((1, hpg * hd, tq), lambda b, g, i: (b, g, i)),
            pl.BlockSpec((1, T, k_feat), lambda b, g, i: (b, 0, g)), vt_spec, k_spec, vt_spec,
            pl.BlockSpec((1, 1, n_slc, tq), lambda b, g, i: (b, g, 0, i)),
            pl.BlockSpec((1, tq, hpg * hd), lambda b, g, i: (b, i, g)),
            pl.BlockSpec((1, 1, tq, 3 * hpg), lambda b, g, i: (b, g, i, 0)),
            pl.BlockSpec((1, 1, 3 * hpg, tq), lambda b, g, i: (b, g, 0, i)),
        ],
        out_specs=pl.BlockSpec((1, tq, hpg * hd), lambda b, g, i: (b, i, g)),
        out_shape=jax.ShapeDtypeStruct((B, T, H * hd), BF16),
        scratch_shapes=[pltpu.VMEM((1, hpg * tq), F32), pltpu.VMEM((1, hpg * tq), F32),
                        pltpu.VMEM((hd, hpg * tq), F32)],
        compiler_params=_params("parallel", "parallel", "arbitrary"),
        name="nsa_selected_window",
    )(feature_major((q * math.log2(math.e)).astype(BF16)), k_sel, feature_major(v_sel), k_win, feature_major(v_win), sel,
      o_cmp, gt, jnp.swapaxes(gt, 2, 3))
    return _matmul(o.reshape(M, H * hd), w_o, (0,), res=x_res.reshape(M, D), gate=gate).reshape(B, T, D)


def _clamped_swiglu(gu):
    x_glu = jnp.minimum(gu[:, :D_EXPERT], SWIGLU_LIMIT)
    x_lin = jnp.clip(gu[:, D_EXPERT:], -SWIGLU_LIMIT, SWIGLU_LIMIT)
    return x_glu * jax.nn.sigmoid(SWIGLU_ALPHA * x_glu) * (x_lin + 1.0)


def _expert_kernel(blk_e_ref, first_ref, next_ref, used_ref, x_ref, w_hbm, b_ref, o_ref, stage, w_bf16,
                   sem, *, layer, post):
    i = pl.program_id(0)

    def fetch(expert):
        return pltpu.make_async_copy(w_hbm.at[layer, expert], stage, sem.at[0])

    @pl.when(i == 0)
    def _():
        fetch(blk_e_ref[0]).start()

    @pl.when(first_ref[i] == 1)
    def _():
        fetch(blk_e_ref[i]).wait()
        w_bf16[...] = stage[...].astype(BF16)

    @pl.when(used_ref[i] == 1)
    def _():
        acc = jnp.dot(x_ref[...], w_bf16[...], preferred_element_type=F32) + b_ref[0]
        o_ref[...] = (acc if post is None else post(acc)).astype(o_ref.dtype)

    @pl.when(used_ref[i] == 0)
    def _():
        o_ref[...] = jnp.zeros_like(o_ref)

    @pl.when((first_ref[i] == 1) & (next_ref[i] >= 0))
    def _():
        fetch(next_ref[i]).start()


def _expert_matmul(tables, x, w, b, layer, post, n_out, out_dtype, name):
    cap, Kd = x.shape
    E, Nd = w.shape[1], w.shape[3]
    R = MOE_ROWS
    return pl.pallas_call(
        functools.partial(_expert_kernel, layer=layer, post=post),
        grid_spec=pltpu.PrefetchScalarGridSpec(
            num_scalar_prefetch=4,
            grid=(cap // R,),
            in_specs=[
                pl.BlockSpec((R, Kd), lambda i, be, fi, nx, us: (i, 0)),
                pl.BlockSpec(memory_space=pl.ANY),
                pl.BlockSpec((None, 1, 1, Nd), lambda i, be, fi, nx, us: (layer, be[i], 0, 0)),
            ],
            out_specs=pl.BlockSpec((R, n_out), lambda i, be, fi, nx, us: (i, 0)),
            scratch_shapes=[pltpu.VMEM((Kd, Nd), F32), pltpu.VMEM((Kd, Nd), BF16),
                            pltpu.SemaphoreType.DMA((1,))],
        ),
        out_shape=jax.ShapeDtypeStruct((cap, n_out), out_dtype),
        compiler_params=_params("arbitrary"),
        name=name,
    )(*tables, x, w, b.reshape(b.shape[0], E, 1, Nd))


def _expert_up_kernel(blk_e_ref, first_ref, next_ref, used_ref, n_used_ref, tok_ref, h_hbm, w_hbm, b_ref,
                      o_ref, x_buf, x_sem, stage, w_bf16, w_sem, *, layer, R):
    i = pl.program_id(0)
    n_steps = pl.num_programs(0)

    def fetch_rows(block, par):
        for r in range(R):
            pltpu.make_async_copy(h_hbm.at[pl.ds(tok_ref[block * R + r], 1)], x_buf.at[par, pl.ds(r, 1)],
                                  x_sem.at[par]).start()

    def wait_rows(par):
        pltpu.make_async_copy(h_hbm.at[pl.ds(0, R)], x_buf.at[par], x_sem.at[par]).wait()

    def fetch_weights(expert):
        return pltpu.make_async_copy(w_hbm.at[layer, expert], stage, w_sem.at[0])

    @pl.when(i == 0)
    def _():
        fetch_rows(0, 0)
        fetch_weights(blk_e_ref[0]).start()

    @pl.when(first_ref[i] == 1)
    def _():
        fetch_weights(blk_e_ref[i]).wait()
        w_bf16[...] = stage[...].astype(BF16)

    @pl.when(used_ref[i] == 1)
    def _():
        par = i % 2
        wait_rows(par)
        fetch_rows((i + 1) % n_steps, 1 - par)
        acc = jnp.dot(x_buf[par].astype(BF16), w_bf16[...], preferred_element_type=F32) + b_ref[0]
        o_ref[...] = _clamped_swiglu(acc).astype(o_ref.dtype)

    @pl.when(used_ref[i] == 0)
    def _():
        o_ref[...] = jnp.zeros_like(o_ref)

    @pl.when((first_ref[i] == 1) & (next_ref[i] >= 0))
    def _():
        fetch_weights(next_ref[i]).start()

    @pl.when(i == n_steps - 1)
    def _():
        wait_rows(n_used_ref[0] % 2)


def _expert_up(tables, n_used, tok, h, w, b, layer):
    N, D = h.shape
    E, Nd = w.shape[1], w.shape[3]
    R = MOE_ROWS
    cap = tok.shape[0]
    assert (cap // R) % 2 == 0
    n_prefetch = len(tables) + 2

    def per_expert(i, be, *_):
        return (layer, be[i], 0, 0)

    return pl.pallas_call(
        functools.partial(_expert_up_kernel, layer=layer, R=R),
        grid_spec=pltpu.PrefetchScalarGridSpec(
            num_scalar_prefetch=n_prefetch,
            grid=(cap // R,),
            in_specs=[
                pl.BlockSpec(memory_space=pl.ANY),
                pl.BlockSpec(memory_space=pl.ANY),
                pl.BlockSpec((None, 1, 1, Nd), per_expert),
            ],
            out_specs=pl.BlockSpec((R, D_EXPERT), lambda i, *_: (i, 0)),
            scratch_shapes=[pltpu.VMEM((2, R, D), F32), pltpu.SemaphoreType.DMA((2,)),
                            pltpu.VMEM((D, Nd), F32), pltpu.VMEM((D, Nd), BF16),
                            pltpu.SemaphoreType.DMA((1,))],
        ),
        out_shape=jax.ShapeDtypeStruct((cap, D_EXPERT), BF16),
        compiler_params=_params("arbitrary"),
        name="moe_up",
    )(*tables, n_used, tok, h, w, b.reshape(b.shape[0], E, 1, Nd))


def _router_kernel(x_ref, w_ref, b_ref, et_ref, p_ref, cnt_ref):
    logits = _fdot(x_ref[...], w_ref[...]) + b_ref[...]
    lane = lax.broadcasted_iota(jnp.int32, logits.shape, 1)
    vals, idxs = [], []
    for _ in range(TOP_K):
        m = jnp.max(logits, axis=-1, keepdims=True)
        idx = jnp.min(jnp.where(logits == m, lane, logits.shape[1]), axis=-1, keepdims=True)
        vals.append(m)
        idxs.append(idx)
        logits = jnp.where(lane == idx, -IMP_BIG, logits)
    es = [jnp.exp(v - vals[0]) for v in vals]
    total = sum(es)
    e_out = jnp.zeros(logits.shape, jnp.int32)
    p_out = jnp.zeros(logits.shape, F32)
    picks = jnp.zeros(logits.shape, jnp.int32)
    for k in range(TOP_K):
        e_out = jnp.where(lane == k, idxs[k], e_out)
        p_out = jnp.where(lane == k, es[k] / total, p_out)
        picks = picks + (lane == idxs[k]).astype(jnp.int32)
    et_ref[...] = e_out.T[:ROUTER_ROWS]
    p_ref[...] = p_out
    cnt_ref[0] = jnp.sum(picks, axis=0, keepdims=True)


def _route(x, w_router, b_router, layer):
    N, D = x.shape
    E, K = N_EXPERTS, TOP_K
    lanes = 128
    w_r = jnp.pad(w_router[layer], ((0, 0), (0, lanes - E)))
    b_r = jnp.concatenate([b_router[layer], jnp.full((lanes - E,), NEG_INF, F32)]).reshape(1, lanes)
    tm = min(N, 512)
    return pl.pallas_call(
        _router_kernel,
        grid=(N // tm,),
        in_specs=[pl.BlockSpec((tm, D), lambda i: (i, 0)),
                  pl.BlockSpec((D, lanes), lambda i: (0, 0)),
                  pl.BlockSpec((1, lanes), lambda i: (0, 0))],
        out_specs=[pl.BlockSpec((ROUTER_ROWS, tm), lambda i: (0, i)),
                   pl.BlockSpec((tm, lanes), lambda i: (i, 0)),
                   pl.BlockSpec((1, 1, lanes), lambda i: (i, 0, 0))],
        out_shape=[jax.ShapeDtypeStruct((ROUTER_ROWS, N), jnp.int32),
                   jax.ShapeDtypeStruct((N, lanes), F32),
                   jax.ShapeDtypeStruct((N // tm, 1, lanes), jnp.int32)],
        compiler_params=_params("parallel"),
        name="moe_router",
    )(x, w_r, b_r)


def _combine_kernel(slot_ref, out_hbm, p_ref, res_ref, gate_ref, o_ref, buf, sem, *, TT, K, N):
    i = pl.program_id(0)
    n_steps = pl.num_programs(0)

    def fetch(step, par):
        for k in range(K):
            for r in range(TT):
                row = slot_ref[k * N + step * TT + r]
                pltpu.make_async_copy(out_hbm.at[pl.ds(row, 1)], buf.at[par, pl.ds(k * TT + r, 1)],
                                      sem.at[par]).start()

    @pl.when(i == 0)
    def _():
        fetch(0, 0)

    @pl.when(i + 1 < n_steps)
    def _():
        fetch(i + 1, (i + 1) % 2)

    par = i % 2
    pltpu.make_async_copy(out_hbm.at[pl.ds(0, K * TT)], buf.at[par], sem.at[par]).wait()
    p = p_ref[...]
    y = p[:, 0:1] * buf[par, 0:TT, :]
    for k in range(1, K):
        y = y + p[:, k:k + 1] * buf[par, k * TT:(k + 1) * TT, :]
    o_ref[...] = res_ref[...] + gate_ref[0] * y


def _moe_combine(out, slot, top_w, x_res, gate, T):
    N, D = x_res.shape
    K, TT = TOP_K, MOE_COMBINE_TOKENS
    return pl.pallas_call(
        functools.partial(_combine_kernel, TT=TT, K=K, N=N),
        grid_spec=pltpu.PrefetchScalarGridSpec(
            num_scalar_prefetch=1,
            grid=(N // TT,),
            in_specs=[
                pl.BlockSpec(memory_space=pl.ANY),
                pl.BlockSpec((TT, top_w.shape[1]), lambda i, s: (i, 0)),
                pl.BlockSpec((TT, D), lambda i, s: (i, 0)),
                pl.BlockSpec((1, 1, D), lambda i, s: ((i * TT) // T, 0, 0)),
            ],
            out_specs=pl.BlockSpec((TT, D), lambda i, s: (i, 0)),
            scratch_shapes=[pltpu.VMEM((2, K * TT, D), F32), pltpu.SemaphoreType.DMA((2,))],
        ),
        out_shape=jax.ShapeDtypeStruct((N, D), F32),
        compiler_params=_params("arbitrary"),
        name="moe_combine",
    )(slot, out, top_w, x_res, gate)


def _moe_ffn(h, layer, x_res, gate, w_router, b_router, w_gu, b_gu, w_down, b_down):
    B, T, D = h.shape
    N = B * T
    E, K, R = N_EXPERTS, TOP_K, MOE_ROWS
    NK = N * K
    x = h.reshape(N, D)
    top_e_t, top_w, tile_counts = _route(x, w_router, b_router, layer)
    pair = jnp.arange(NK, dtype=jnp.int32)
    e_s, order = lax.sort((top_e_t[:K].reshape(-1), pair), num_keys=1, is_stable=True)
    counts = jnp.sum(tile_counts, axis=(0, 1))[:E]
    padded = (counts + R - 1) // R * R
    g_start = jnp.cumsum(counts) - counts
    p_end = jnp.cumsum(padded)
    shift = (p_end - padded) - g_start
    dest = pair + jnp.take(shift, e_s, axis=0)
    n_blocks = (NK + R - 1) // R + E
    cap = n_blocks * R
    experts = jnp.arange(E, dtype=jnp.int32)[None, :]
    blocks = jnp.arange(n_blocks, dtype=jnp.int32)
    blk_e = jnp.minimum(jnp.sum(p_end[None, :] <= (blocks * R)[:, None], axis=1), E - 1).astype(jnp.int32)
    blk_is = blk_e[:, None] == experts
    blk_shift = jnp.sum(jnp.where(blk_is, shift[None, :], 0), axis=1)
    used = (blocks * R < p_end[E - 1]).astype(jnp.int32)
    first = used * (blk_e != jnp.concatenate([jnp.full((1,), -1, jnp.int32), blk_e[:-1]])).astype(jnp.int32)
    after = jnp.sum(jnp.where(blk_is, p_end[None, :], 0), axis=1) // R
    after_e = jnp.sum(jnp.where(after[:, None] == blocks[None, :], blk_e[None, :], 0), axis=1)
    next_e = jnp.where(after * R < p_end[E - 1], after_e, -1).astype(jnp.int32)
    tables = (blk_e, first, next_e, used)
    src = jnp.clip(jnp.arange(cap, dtype=jnp.int32) - jnp.repeat(blk_shift, R), 0, NK - 1)
    tok = jnp.take(order, src, axis=0) % N
    n_used = jnp.sum(used, keepdims=True)
    act = _expert_up(tables, n_used, tok, x, w_gu, b_gu, layer)
    out = _expert_matmul(tables, act, w_down, b_down, layer, None, D, F32, "moe_down")

    _, slot = lax.sort((order, dest), num_keys=1)
    return _moe_combine(out, slot, top_w, x_res.reshape(N, D), gate, T).reshape(B, T, D)


def kernel(x, c, ada_w, ada_b, norm_g, rw_mu, rw_w_rkv, rw_w0, rw_w1, rw_w2, rw_a0, rw_a1, rw_a2, rw_g1, rw_g2, rw_k_k, rw_k_a, rw_r_k, rw_ln_g, rw_ln_b, rw_w_o, ret_w_in, ret_gn_g, ret_gn_b, ret_w_o, gla_w_in, gla_w_a1, gla_w_a2, gla_b_a, gla_gn_g, gla_w_o, nsa_w_in, nsa_q_g, nsa_k_g, nsa_cmp_pos, nsa_cmp_w1, nsa_cmp_w2, nsa_w_o, moe_router_w, moe_router_b, moe_w_gu, moe_b_gu, moe_w_down, moe_b_down):
    B, T, D = x.shape
    depth = ada_w.shape[0]
    c_act = jnp.pad(jax.nn.silu(c), ((0, 8 - B), (0, 0)))
    for i in range(depth):
        mod = _matmul(c_act, ada_w, (i,), tm=8, tn=1024)[:B] + ada_b[i]
        sh1, sc1, gt1, sh2, sc2, gt2 = jnp.split(mod, 6, axis=-1)
        gt1, gt2 = gt1.reshape(B, 1, D), gt2.reshape(B, 1, D)
        m, j = i % 4, i // 4
        if m == 0:
            x = _rwkv7_mix(x, norm_g[i, 0], sc1, sh1, gt1, rw_mu[j], rw_w_rkv[j:j + 1], rw_w0[j],
                           rw_w1[j:j + 1], rw_w2[j:j + 1], rw_a0[j], rw_a1[j:j + 1], rw_a2[j:j + 1],
                           rw_g1[j:j + 1], rw_g2[j:j + 1], rw_k_k[j], rw_k_a[j], rw_r_k[j], rw_ln_g[j],
                           rw_ln_b[j], rw_w_o[j:j + 1])
        elif m == 1:
            h, = _norm_modulate(x, norm_g[i, 0], sc1, sh1, (BF16,))
            x = _retention_mix(h, x, gt1, ret_w_in[j:j + 1], ret_gn_g[j], ret_gn_b[j], ret_w_o[j:j + 1])
        elif m == 2:
            h, = _norm_modulate(x, norm_g[i, 0], sc1, sh1, (BF16,))
            x = _gla_mix(h, x, gt1, gla_w_in[j:j + 1], gla_w_a1[j:j + 1], gla_w_a2[j:j + 1], gla_b_a[j],
                         gla_gn_g[j], gla_w_o[j:j + 1])
        else:
            h, = _norm_modulate(x, norm_g[i, 0], sc1, sh1, (BF16,))
            x = _nsa_mix(h, x, gt1, nsa_w_in[j:j + 1], nsa_q_g[j], nsa_k_g[j], nsa_cmp_pos[j],
                         nsa_cmp_w1[j], nsa_cmp_w2[j], nsa_w_o[j:j + 1])
        h, = _norm_modulate(x, norm_g[i, 1], sc2, sh2, (F32,))
        x = _moe_ffn(h, i, x, gt2, moe_router_w, moe_router_b, moe_w_gu, moe_b_gu, moe_w_down,
                     moe_b_down)
    return x
```

```python
import functools
import math

import jax
import jax.numpy as jnp
from jax import lax
from jax.experimental import pallas as pl
from jax.experimental.pallas import tpu as pltpu

F32 = jnp.float32
BF16 = jnp.bfloat16
HIGHEST = lax.Precision.HIGHEST

NORM_EPS = 1e-6
NEG_INF = -1e30

RW_HEAD_DIM = 64
RW_GN_EPS = 64e-5
RW_CHUNK = 128
RW_HEADS_PER_STEP = 8
RW_CHUNKS_PER_STEP = 1
RW_INV_BASE = 8

RET_HEADS = 8
RET_DK = 256
RET_DV = 512
RET_CHUNK = 128
RET_ROT_BASE = 10000.0
CHUNKS_PER_STEP = 8

GLA_HEADS = 4
GLA_DK = 256
GLA_DV = 512
GLA_GATE_NORM = 16.0
GLA_CHUNK = 64

NSA_HEADS = 16
NSA_KV_HEADS = 4
NSA_HD = 128
NSA_CMP_BLK = 32
NSA_CMP_STRIDE = 16
NSA_SEL_BLK = 64
NSA_SEL_TOPK = 16
NSA_WINDOW = 512
NSA_CMP_TQ = 256
NSA_ATT_TQ = 128
NSA_ATT_TK = 512
IMP_BIG = 3e38

N_EXPERTS = 32
TOP_K = 4
D_EXPERT = 768
SWIGLU_ALPHA = 1.702
SWIGLU_LIMIT = 7.0
MOE_ROWS = 512
MOE_COMBINE_TOKENS = 128
ROUTER_ROWS = 8

VMEM_LIMIT_BYTES = 52 * 1024 * 1024
MATMUL_VMEM_BUDGET = 48 * 1024 * 1024


def _params(*sem):
    return pltpu.CompilerParams(dimension_semantics=sem, vmem_limit_bytes=VMEM_LIMIT_BYTES)


def _bdot(a, b):
    return jnp.dot(a.astype(BF16), b.astype(BF16), preferred_element_type=F32)


def _bdot_nt(a, b):
    return lax.dot_general(a.astype(BF16), b.astype(BF16), (((1,), (1,)), ((), ())),
                           preferred_element_type=F32)


def _bdot_tn(a, b):
    return lax.dot_general(a.astype(BF16), b.astype(BF16), (((0,), (0,)), ((), ())),
                           preferred_element_type=F32)


def _fdot(a, b):
    return jnp.dot(a, b, precision=HIGHEST, preferred_element_type=F32)


def _fdot_nt(a, b):
    return lax.dot_general(a, b, (((1,), (1,)), ((), ())), precision=HIGHEST,
                           preferred_element_type=F32)


def _split3(x):
    hi = x.astype(BF16)
    rem = x - hi.astype(F32)
    mid = rem.astype(BF16)
    return hi, mid, (rem - mid.astype(F32)).astype(BF16)


def _exact_lhs_dot(m, x):
    mb = m.astype(BF16)
    return sum(jnp.dot(mb, part, preferred_element_type=F32) for part in _split3(x))


def _exact_rhs_dot(x, m):
    mb = m.astype(BF16)
    return sum(jnp.dot(part, mb, preferred_element_type=F32) for part in _split3(x))


def _softplus(z):
    return jnp.maximum(z, 0.0) + jnp.log(1.0 + jnp.exp(-jnp.abs(z)))


_POST = {
    None: lambda z: z,
    "tanh": jnp.tanh,
    "sigmoid": jax.nn.sigmoid,
    "silu": jax.nn.silu,
    "rwkv_log_decay": lambda z: -jnp.exp(-_softplus(-z) - 0.5),
    "gla_log_gate": lambda z: -_softplus(-z) / GLA_GATE_NORM,
}


def _mm_kernel(*refs, has_bias, has_res, post):
    x_ref, w_ref = refs[0], refs[1]
    pos = 2
    acc = _bdot(x_ref[...], w_ref[...])
    if has_bias:
        acc = acc + refs[pos][...]
        pos += 1
    acc = _POST[post](acc)
    if has_res:
        acc = refs[pos][...] + refs[pos + 1][0] * acc
        pos += 2
    o_ref = refs[pos]
    o_ref[...] = acc.astype(o_ref.dtype)


def _matmul(x, w, lead=(), *, bias=None, res=None, gate=None, n_cols=None, tm=None, tn=None,
            post=None, out_dtype=F32, name="matmul"):
    M, K = x.shape
    N = n_cols if n_cols is not None else w.shape[-1]
    if tm is None:
        tm = min(M, 1024 if x.dtype == BF16 else 512)
    if tn is None:
        def vmem_bytes(cols):
            window = tm * K * x.dtype.itemsize + K * cols * 4 + tm * cols * jnp.dtype(out_dtype).itemsize
            window += tm * cols * 4 if res is not None else 0
            return 2 * window + K * cols * 2 + tm * cols * 4
        fits = [c for c in (1024, 512, 256, 128) if N % c == 0 and vmem_bytes(c) <= MATMUL_VMEM_BUDGET]
        tn = fits[0] if fits else N
    assert M % tm == 0
    nlead = len(lead)
    in_specs = [
        pl.BlockSpec((tm, K), lambda i, j: (i, 0)),
        pl.BlockSpec((None,) * nlead + (K, tn), lambda i, j: tuple(lead) + (0, j)),
    ]
    args = [x, w]
    if bias is not None:
        in_specs.append(pl.BlockSpec((1, tn), lambda i, j: (0, j)))
        args.append(bias)
    if res is not None:
        rows_per_gate = M // gate.shape[0]
        assert rows_per_gate % tm == 0
        in_specs.append(pl.BlockSpec((tm, tn), lambda i, j: (i, j)))
        in_specs.append(pl.BlockSpec((1, 1, tn), lambda i, j: ((i * tm) // rows_per_gate, 0, j)))
        args += [res, gate]
    return pl.pallas_call(
        functools.partial(_mm_kernel, has_bias=bias is not None, has_res=res is not None, post=post),
        grid=(M // tm, pl.cdiv(N, tn)),
        in_specs=in_specs,
        out_specs=pl.BlockSpec((tm, tn), lambda i, j: (i, j)),
        out_shape=jax.ShapeDtypeStruct((M, N), out_dtype),
        compiler_params=_params("parallel", "parallel"),
        name=name,
    )(*args)


def _modulated_norm(x, g, sc, sh):
    y = x * lax.rsqrt(jnp.mean(x * x, axis=-1, keepdims=True) + NORM_EPS) * g
    return y * (1.0 + sc) + sh


def _normmod_kernel(x_ref, g_ref, sc_ref, sh_ref, *o_refs):
    h = _modulated_norm(x_ref[0], g_ref[...], sc_ref[0], sh_ref[0])
    for o_ref in o_refs:
        o_ref[0] = h.astype(o_ref.dtype)


def _norm_modulate(x, g, sc, sh, out_dtypes):
    B, T, D = x.shape
    tr = min(T, 512)
    row = pl.BlockSpec((1, tr, D), lambda b, i: (b, i, 0))
    per_batch = pl.BlockSpec((1, 1, D), lambda b, i: (b, 0, 0))
    return pl.pallas_call(
        _normmod_kernel,
        grid=(B, T // tr),
        in_specs=[row, pl.BlockSpec((1, D), lambda b, i: (0, 0)), per_batch, per_batch],
        out_specs=[row] * len(out_dtypes),
        out_shape=[jax.ShapeDtypeStruct((B, T, D), dt) for dt in out_dtypes],
        compiler_params=_params("parallel", "parallel"),
        name="norm_modulate",
    )(x, g.reshape(1, D), sc.reshape(B, 1, D), sh.reshape(B, 1, D))


def _rwkv_prep_kernel(x_ref, halo_ref, g_ref, sc_ref, sh_ref, mu_ref, *o_refs, halo_rows):
    h = _modulated_norm(x_ref[0], g_ref[...], sc_ref[0], sh_ref[0])
    h_halo = _modulated_norm(halo_ref[0], g_ref[...], sc_ref[0], sh_ref[0])
    first = jnp.where(pl.program_id(1) > 0, h_halo[halo_rows - 1:halo_rows, :], 0.0)
    row = lax.broadcasted_iota(jnp.int32, h.shape, 0)
    d = jnp.where(row == 0, first, pltpu.roll(h, 1, axis=0)) - h
    for j, o_ref in enumerate(o_refs):
        o_ref[0] = (h + d * mu_ref[j:j + 1, :]).astype(o_ref.dtype)


def _rwkv_prep(x, g, sc, sh, mu):
    B, T, D = x.shape
    tr = min(T, 512)
    hr = 8
    n_mix = mu.shape[0]
    row = pl.BlockSpec((1, tr, D), lambda b, i: (b, i, 0))
    per_batch = pl.BlockSpec((1, 1, D), lambda b, i: (b, 0, 0))
    return pl.pallas_call(
        functools.partial(_rwkv_prep_kernel, halo_rows=hr),
        grid=(B, T // tr),
        in_specs=[row,
                  pl.BlockSpec((1, hr, D), lambda b, i: (b, jnp.maximum(i * (tr // hr) - 1, 0), 0)),
                  pl.BlockSpec((1, D), lambda b, i: (0, 0)), per_batch, per_batch,
                  pl.BlockSpec((n_mix, D), lambda b, i: (0, 0))],
        out_specs=[row] * n_mix,
        out_shape=[jax.ShapeDtypeStruct((B, T, D), BF16)] * n_mix,
        compiler_params=_params("parallel", "parallel"),
        name="rwkv_shift_mix",
    )(x, x, g.reshape(1, D), sc.reshape(B, 1, D), sh.reshape(B, 1, D), mu)


def _rwkv_kernel(*refs, C, N, HB, SUB):
    h_ref = refs[-1]

    @pl.when(pl.program_id(2) == 0)
    def _():
        h_ref[...] = jnp.zeros_like(h_ref)

    for s in range(SUB):
        _rwkv_chunk(slice(s * C, (s + 1) * C), *refs, C=C, N=N, HB=HB)


def _rwkv_chunk(rows, r_ref, k_ref, v_ref, lw_ref, a_ref, g_ref, kk_ref, ka_ref, rk_ref, lng_ref,
                lnb_ref, o_ref, h_ref, *, C, N, HB):
    row = lax.broadcasted_iota(jnp.int32, (C, C), 0)
    col = lax.broadcasted_iota(jnp.int32, (C, C), 1)
    incl = row >= col
    strict = row > col
    eye = (row == col).astype(F32)
    eye_n = (lax.broadcasted_iota(jnp.int32, (N, N), 0) == lax.broadcasted_iota(jnp.int32, (N, N), 1))
    heads = range(HB)

    def head(x, i):
        return x[:, i * N:(i + 1) * N]

    r_all, k_all, v_all, a_all, lw_all = (ref[0, rows, :] for ref in (r_ref, k_ref, v_ref, a_ref, lw_ref))
    cum_all = _exact_lhs_dot(incl.astype(F32), lw_all)
    tot_all = cum_all[C - 1:C, :]
    mid_all = cum_all[C // 2 - 1:C // 2, :]
    cen_all = cum_all - mid_all
    e_neg = jnp.exp(-cen_all)
    e_tail = jnp.exp(tot_all - cum_all)
    kx_all = k_all * kk_ref[...]
    kp_all = k_all * (1.0 + (a_all - 1.0) * ka_ref[...])
    rt_all = r_all * jnp.exp(cen_all)
    kn_all = kp_all * e_neg
    kt_all = kp_all * e_tail
    ep_all = jnp.exp(cen_all - lw_all)
    rkr_all = r_all * kp_all * rk_ref[...]

    def column(row_vec):
        return jnp.sum(jnp.where(eye_n, row_vec, 0.0), axis=1, keepdims=True)

    kappa = [head(kx_all, i) / jnp.maximum(
        jnp.sqrt(jnp.sum(jnp.square(head(kx_all, i)), axis=-1, keepdims=True)), 1e-12) for i in heads]
    b = [kappa[i] * head(a_all, i) for i in heads]
    kap_t = [kappa[i] * head(ep_all, i) for i in heads]
    r_t = [head(rt_all, i) for i in heads]
    v = [head(v_all, i) for i in heads]
    big = [_bdot_nt(jnp.concatenate([kap_t[i], r_t[i]], axis=0),
                    jnp.concatenate([b[i] * head(e_neg, i), head(kn_all, i)], axis=0))
           for i in heads]
    t_k = [jnp.where(strict, big[i][:C, C:], 0.0) for i in heads]
    m_b = [jnp.where(incl, big[i][C:, :C], 0.0) for i in heads]
    m_k = [jnp.where(incl, big[i][C:, C:], 0.0) for i in heads]
    t_b = [jnp.where(strict, big[i][:C, :C], 0.0) for i in heads]
    base = RW_INV_BASE
    p = [jnp.where((row // base) == (col // base), -t_b[i], 0.0) for i in heads]
    inv = [eye + p[i] for i in heads]
    tkv = [_bdot(t_k[i], v[i]) for i in heads]
    for _ in range(int(math.log2(base)) - 1):
        p = [_bdot(p[i], p[i]) for i in heads]
        inv = [inv[i] + _bdot(inv[i], p[i]) for i in heads]
    size = base
    while size < C:
        couple = ((row // (2 * size)) == (col // (2 * size))) & ((row // size) != (col // size))
        lower = [jnp.where(couple, t_b[i], 0.0) for i in heads]
        inv = [inv[i] - _bdot(inv[i], _bdot(lower[i], inv[i])) for i in heads]
        size *= 2
    h0 = [h_ref[i] for i in heads]
    h0c = [h0[i] * jnp.exp(column(head(mid_all, i))) for i in heads]
    aw = [_bdot(inv[i], jnp.concatenate([kap_t[i], tkv[i]], axis=1)) for i in heads]
    u = [_bdot(aw[i][:, :N], h0c[i]) + aw[i][:, N:] for i in heads]
    y = [_bdot(jnp.concatenate([r_t[i], m_k[i], -m_b[i]], axis=1),
               jnp.concatenate([h0c[i], v[i], u[i]], axis=0)) for i in heads]
    for i in heads:
        h_ref[i] = jnp.exp(column(head(tot_all, i))) * h0[i] + _bdot_tn(
            jnp.concatenate([head(kt_all, i), -(b[i] * head(e_tail, i))], axis=0),
            jnp.concatenate([v[i], u[i]], axis=0))
    outs = []
    for i in heads:
        mu = jnp.mean(y[i], axis=-1, keepdims=True)
        var = jnp.mean(jnp.square(y[i] - mu), axis=-1, keepdims=True)
        yn = (y[i] - mu) * lax.rsqrt(var + RW_GN_EPS)
        outs.append(yn * head(lng_ref[...], i) + head(lnb_ref[...], i)
                    + jnp.sum(head(rkr_all, i), axis=-1, keepdims=True) * v[i])
    o_ref[0, rows, :] = (jnp.concatenate(outs, axis=-1) * g_ref[0, rows, :]).astype(o_ref.dtype)


def _rwkv_core(r, k, v, lw, a, g, k_k, k_a, r_k, ln_g, ln_b):
    B, T, D = r.shape
    C, N, HB, SUB = RW_CHUNK, RW_HEAD_DIM, RW_HEADS_PER_STEP, RW_CHUNKS_PER_STEP
    W = HB * N
    seq = pl.BlockSpec((1, SUB * C, W), lambda b, h, c: (b, c, h))
    par = pl.BlockSpec((1, W), lambda b, h, c: (0, h))
    return pl.pallas_call(
        functools.partial(_rwkv_kernel, C=C, N=N, HB=HB, SUB=SUB),
        grid=(B, D // W, T // (SUB * C)),
        in_specs=[seq] * 6 + [par] * 5,
        out_specs=seq,
        out_shape=jax.ShapeDtypeStruct((B, T, D), BF16),
        scratch_shapes=[pltpu.VMEM((HB, N, N), F32)],
        compiler_params=_params("parallel", "parallel", "arbitrary"),
        name="rwkv_core",
    )(r, k, v, lw, a, g, k_k.reshape(1, D), k_a.reshape(1, D), r_k.reshape(1, D),
      ln_g.reshape(1, D), ln_b.reshape(1, D))


def _rwkv7_mix(x_res, norm_g, sc, shift, gate, mu, w_rkv, w0, w1, w2, a0, a1, a2, g1, g2, k_k, k_a,
               r_k, ln_g, ln_b, w_o):
    B, T, D = x_res.shape
    M = B * T
    xr, xk, xv, xw, xa, xg = [z.reshape(M, D) for z in _rwkv_prep(x_res, norm_g, sc, shift, mu)]
    r = _matmul(xr, w_rkv, (0, 0), name="rwkv_r")
    k = _matmul(xk, w_rkv, (0, 1), name="rwkv_k")
    v = _matmul(xv, w_rkv, (0, 2), name="rwkv_v")
    log_decay = _matmul(_matmul(xw, w1, (0,), post="tanh", out_dtype=BF16), w2, (0,),
                        bias=w0.reshape(1, D), post="rwkv_log_decay", name="rwkv_decay")
    a = _matmul(_matmul(xa, a1, (0,), out_dtype=BF16), a2, (0,), bias=a0.reshape(1, D),
                post="sigmoid", name="rwkv_a")
    g = _matmul(_matmul(xg, g1, (0,), post="sigmoid", out_dtype=BF16), g2, (0,), name="rwkv_g")
    sh = (B, T, D)
    o = _rwkv_core(r.reshape(sh), k.reshape(sh), v.reshape(sh), log_decay.reshape(sh),
                   a.reshape(sh), g.reshape(sh), k_k, k_a, r_k, ln_g, ln_b)
    return _matmul(o.reshape(M, D), w_o, (0,), res=x_res.reshape(M, D), gate=gate).reshape(sh)


def _ret_kernel(q_ref, k_ref, v_ref, g_ref, cos_ref, sin_ref, dm_ref, xz_ref, gng_ref, gnb_ref,
                o_ref, s_ref, *, dk, C, SUB):
    @pl.when(pl.program_id(2) == 0)
    def _():
        s_ref[...] = jnp.zeros_like(s_ref)

    half = dk // 2
    xi, zeta, g_chunk = xz_ref[0, :, 0:1], xz_ref[0, :, 1:2], xz_ref[0, 0:1, 2:3]
    for s in range(SUB):
        rows = slice(s * C, (s + 1) * C)
        cos, sin = cos_ref[rows, :], sin_ref[rows, :]

        def rot(z):
            z1, z2 = z[:, :half], z[:, half:]
            return jnp.concatenate([z1 * cos - z2 * sin, z1 * sin + z2 * cos], axis=-1)

        q = rot(q_ref[0, rows, :])
        k = rot(k_ref[0, rows, :] * (dk ** -0.5))
        v = v_ref[0, rows, :]
        scores = _bdot_nt(q, k) * dm_ref[0]
        state = s_ref[...]
        o = _bdot(scores, v) + _bdot(q * xi, state)
        s_ref[...] = g_chunk * state + _bdot_tn(k * zeta, v)
        mu = jnp.mean(o, axis=-1, keepdims=True)
        var = jnp.mean(jnp.square(o - mu), axis=-1, keepdims=True)
        y = (o - mu) * lax.rsqrt(var + NORM_EPS) * gng_ref[0] + gnb_ref[0]
        gate = g_ref[0, rows, :]
        o_ref[0, rows, :] = (y * (gate * jax.nn.sigmoid(gate))).astype(o_ref.dtype)


def _retention_mix(h, x_res, gate, w_in, gn_g, gn_b, w_o):
    B, T, D = h.shape
    M = B * T
    H, dk, dv, C = RET_HEADS, RET_DK, RET_DV, RET_CHUNK
    SUB = min(CHUNKS_PER_STEP, T // C)
    R = SUB * C
    proj =_matmul(h.reshape(M, D), w_in, (0,)).reshape(B, T, H * (2 * dk + 2 * dv))
    qb, kb, vb, gb = 0, (H * dk) // dk, (2 * H * dk) // dv, (2 * H * dk + H * dv) // dv
    theta = 1.0 / (RET_ROT_BASE ** jnp.linspace(0.0, 1.0, dk // 2, dtype=F32))
    ang = jnp.arange(T, dtype=F32)[:, None] * theta[None, :]
    log_gamma = jnp.log(1.0 - 2.0 ** (-5.0 - jnp.arange(H, dtype=F32)))
    pos = jnp.arange(C, dtype=F32)
    rel = pos[:, None] - pos[None, :]
    dmask = jnp.where(rel >= 0, jnp.exp(jnp.maximum(rel, 0.0) * log_gamma[:, None, None]), 0.0)
    xi = jnp.exp((pos + 1.0)[None, :] * log_gamma[:, None])
    zeta = jnp.exp((C - 1.0 - pos)[None, :] * log_gamma[:, None])
    g_chunk = jnp.broadcast_to(jnp.exp(C * log_gamma)[:, None], (H, C))
    xz = jnp.concatenate([jnp.stack([xi, zeta, g_chunk], axis=-1), jnp.zeros((H, C, 125), F32)], axis=-1)
    o = pl.pallas_call(
        functools.partial(_ret_kernel, dk=dk, C=C, SUB=SUB),
        grid=(B, H, T // R),
        in_specs=[
            pl.BlockSpec((1, R, dk), lambda b, h, c: (b, c, qb + h)),
            pl.BlockSpec((1, R, dk), lambda b, h, c: (b, c, kb + h)),
            pl.BlockSpec((1, R, dv), lambda b, h, c: (b, c, vb + h)),
            pl.BlockSpec((1, R, dv), lambda b, h, c: (b, c, gb + h)),
            pl.BlockSpec((R, dk // 2), lambda b, h, c: (c, 0)),
            pl.BlockSpec((R, dk // 2), lambda b, h, c: (c, 0)),
            pl.BlockSpec((1, C, C), lambda b, h, c: (h, 0, 0)),
            pl.BlockSpec((1, C, 128), lambda b, h, c: (h, 0, 0)),
            pl.BlockSpec((1, 1, dv), lambda b, h, c: (h, 0, 0)),
            pl.BlockSpec((1, 1, dv), lambda b, h, c: (h, 0, 0)),
        ],
        out_specs=pl.BlockSpec((1, R, dv), lambda b, h, c: (b, c, h)),
        out_shape=jax.ShapeDtypeStruct((B, T, H * dv), BF16),
        scratch_shapes=[pltpu.VMEM((dk, dv), F32)],
        compiler_params=_params("parallel", "parallel", "arbitrary"),
        name="retention_core",
    )(proj, proj, proj, proj, jnp.cos(ang), jnp.sin(ang), dmask, xz,
      gn_g.reshape(H, 1, dv), gn_b.reshape(H, 1, dv))
    return _matmul(o.reshape(M, H * dv), w_o, (0,), res=x_res.reshape(M, D), gate=gate).reshape(B, T, D)


def _gla_kernel(q_ref, k_ref, v_ref, g_ref, la_ref, gn_ref, o_ref, s_ref, *, C, dk, SUB):
    @pl.when(pl.program_id(2) == 0)
    def _():
        s_ref[...] = jnp.zeros_like(s_ref)

    row = lax.broadcasted_iota(jnp.int32, (C, C), 0)
    col = lax.broadcasted_iota(jnp.int32, (C, C), 1)
    causal = row >= col
    eye_k = (lax.broadcasted_iota(jnp.int32, (dk, dk), 0) == lax.broadcasted_iota(jnp.int32, (dk, dk), 1))
    for s in range(SUB):
        rows = slice(s * C, (s + 1) * C)
        la = la_ref[0, rows, :]
        b = _exact_lhs_dot(causal.astype(F32), la)
        b_last = b[C - 1:C, :]
        d_last_col = jnp.exp(jnp.sum(jnp.where(eye_k, b_last, 0.0), axis=1, keepdims=True))
        k, v = k_ref[0, rows, :], v_ref[0, rows, :]
        q_in = q_ref[0, rows, :] * (dk ** -0.5) * jnp.exp(b)
        k_in = k * jnp.exp(-b)
        att = jnp.where(causal, _bdot_nt(q_in, k_in), 0.0)
        state = s_ref[...]
        o = _bdot(att, v) + _bdot(q_in, state)
        s_ref[...] = d_last_col * state + _bdot_tn(k * jnp.exp(b_last - b), v)
        y = o * lax.rsqrt(jnp.mean(o * o, axis=-1, keepdims=True) + NORM_EPS) * gn_ref[...]
        gate = g_ref[0, rows, :]
        o_ref[0, rows, :] = (y * (gate * jax.nn.sigmoid(gate))).astype(o_ref.dtype)


def _gla_mix(h, x_res, gate, w_in, w_a1, w_a2, b_a, gn_g, w_o):
    B, T, D = h.shape
    M = B * T
    H, dk, dv, C = GLA_HEADS, GLA_DK, GLA_DV, GLA_CHUNK
    SUB = min(CHUNKS_PER_STEP, T // C)
    R = SUB * C
    h2 = h.reshape(M, D)
    proj = _matmul(h2, w_in, (0,)).reshape(B, T, H * (2 * dk + 2 * dv))
    log_a = _matmul(_matmul(h2, w_a1, (0,), out_dtype=BF16), w_a2, (0,), bias=b_a.reshape(1, H * dk),
                    post="gla_log_gate", name="gla_log_gate").reshape(B, T, H * dk)
    qb, kb, vb, gb = 0, H, (2 * H * dk) // dv, (2 * H * dk + H * dv) // dv
    o = pl.pallas_call(
        functools.partial(_gla_kernel, C=C, dk=dk, SUB=SUB),
        grid=(B, H, T // R),
        in_specs=[
            pl.BlockSpec((1, R, dk), lambda b, h, c: (b, c, qb + h)),
            pl.BlockSpec((1, R, dk), lambda b, h, c: (b, c, kb + h)),
            pl.BlockSpec((1, R, dv), lambda b, h, c: (b, c, vb + h)),
            pl.BlockSpec((1, R, dv), lambda b, h, c: (b, c, gb + h)),
            pl.BlockSpec((1, R, dk), lambda b, h, c: (b, c, h)),
            pl.BlockSpec((1, dv), lambda b, h, c: (0, 0)),
        ],
        out_specs=pl.BlockSpec((1, R, dv), lambda b, h, c: (b, c, h)),
        out_shape=jax.ShapeDtypeStruct((B, T, H * dv), BF16),
        scratch_shapes=[pltpu.VMEM((dk, dv), F32)],
        compiler_params=_params("parallel", "parallel", "arbitrary"),
        name="gla_core",
    )(proj, proj, proj, proj, log_a, gn_g.reshape(1, dv))
    return _matmul(o.reshape(M, H * dv), w_o, (0,), res=x_res.reshape(M, D), gate=gate).reshape(B, T, D)


def _nsa_cmp_kernel(q_ref, kc_ref, vc_ref, o_ref, sel_ref, *, tq, n_cmp, n_pad, n_slc, hpg):
    L, S, Ls, hd = NSA_CMP_BLK, NSA_CMP_STRIDE, NSA_SEL_BLK, NSA_HD
    t_pos = pl.program_id(2) * tq + lax.broadcasted_iota(jnp.int32, (tq, 1), 0)
    n_ix = lax.broadcasted_iota(jnp.int32, (1, n_pad), 1)
    valid = (n_ix * S + (L - 1) <= t_pos) & (n_ix < n_cmp)
    validf = valid.astype(F32)
    c_start = lax.broadcasted_iota(jnp.int32, (n_pad, n_slc), 0) * S
    s_start = lax.broadcasted_iota(jnp.int32, (n_pad, n_slc), 1) * Ls
    overlap = ((c_start <= s_start + (Ls - 1)) & (c_start + (L - 1) >= s_start)
               & (c_start < n_cmp * S)).astype(F32)
    kc, vc = kc_ref[0, 0], vc_ref[0, 0]
    kc_hi = kc.astype(BF16)
    kc_lo = (kc - kc_hi.astype(F32)).astype(BF16)
    heads = range(hpg)
    q = [q_ref[0, :, hh * hd:(hh + 1) * hd] for hh in heads]
    q_hi = [q[hh].astype(BF16) for hh in heads]
    q_lo = [(q[hh] - q_hi[hh].astype(F32)).astype(BF16) for hh in heads]
    s = [_bdot_nt(q_hi[hh], kc_hi) + _bdot_nt(q_hi[hh], kc_lo) + _bdot_nt(q_lo[hh], kc_hi) for hh in heads]
    s = [jnp.where(valid, s[hh], NEG_INF) for hh in heads]
    e = [jnp.exp(s[hh] - jnp.max(s[hh], axis=-1, keepdims=True)) for hh in heads]
    p = [e[hh] / jnp.sum(e[hh], axis=-1, keepdims=True) * validf for hh in heads]
    o_ref[0] = jnp.concatenate([_bdot(p[hh], vc) for hh in heads], axis=-1)
    imp = _exact_rhs_dot(sum(p), overlap)

    imp = imp.T
    j = lax.broadcasted_iota(jnp.int32, (n_slc, 1), 0)
    cur = (pl.program_id(2) * tq + lax.broadcasted_iota(jnp.int32, (1, tq), 1)) // Ls
    forced = (j == 0) | (j == cur) | (j == cur - 1)
    imp = jnp.where(j > cur, -IMP_BIG, jnp.where(forced, IMP_BIG, imp))
    rank = jnp.zeros((n_slc, tq), jnp.int32)
    for jp in range(n_slc):
        c = imp[jp:jp + 1, :]
        rank = rank + ((c > imp) | ((c == imp) & (jp < j))).astype(jnp.int32)
    sel_ref[0, 0] = (rank < min(NSA_SEL_TOPK, n_slc)).astype(F32)


def _nsa_att_kernel(qt_ref, ks_ref, vst_ref, kw_ref, vwt_ref, sel_ref, oc_ref, gt_ref, gtt_ref, o_ref,
                    m_ref, l_ref, acc_ref, *, tq, tk, T, n_slc, hpg):
    Ls, W, hd = NSA_SEL_BLK, NSA_WINDOW, NSA_HD
    qi = pl.program_id(2)
    t0 = qi * tq
    heads = range(hpg)
    cols = [slice(hh * tq, (hh + 1) * tq) for hh in heads]
    qt = jnp.concatenate([qt_ref[0, hh * hd:(hh + 1) * hd, :] for hh in heads], axis=1)
    t_q = t0 + lax.broadcasted_iota(jnp.int32, (1, tq), 1)
    sel_t = sel_ref[0, 0]

    m_ref[...] = jnp.full_like(m_ref, NEG_INF)
    l_ref[...] = jnp.zeros_like(l_ref)
    acc_ref[...] = jnp.zeros_like(acc_ref)

    pad = ks_ref.shape[2] - hd - n_slc
    sel_bias = jnp.where(sel_t > 0.5, 0.0, NEG_INF).astype(BF16)
    q_aug = jnp.concatenate([qt, jnp.concatenate([sel_bias] * hpg, axis=1),
                             jnp.zeros((pad, hpg * tq), BF16)], axis=0)

    def key_tile(kb, causal):
        k0 = pl.multiple_of(kb * tk, tk)
        vt_t = vst_ref[0, :, pl.ds(k0, tk)]
        s_all = jnp.dot(ks_ref[0, pl.ds(k0, tk), :], q_aug, preferred_element_type=F32)
        kpos = k0 + lax.broadcasted_iota(jnp.int32, (tk, 1), 0)
        ps = []
        for hh in heads:
            s = s_all[:, cols[hh]]
            if causal:
                s = jnp.where(kpos <= t_q, s, NEG_INF)
            m_old = m_ref[:, cols[hh]]
            m_new = jnp.maximum(m_old, jnp.max(s, axis=0, keepdims=True))
            alpha = jnp.exp2(m_old - m_new)
            p = jnp.exp2(s - m_new)
            l_ref[:, cols[hh]] = alpha * l_ref[:, cols[hh]] + jnp.sum(p, axis=0, keepdims=True)
            acc_ref[:, cols[hh]] = alpha * acc_ref[:, cols[hh]]
            m_ref[:, cols[hh]] = m_new
            ps.append(p.astype(BF16))
        acc_ref[...] += jnp.dot(vt_t, jnp.concatenate(ps, axis=1), preferred_element_type=F32)

    n_full = t0 // tk

    def full_tile(kb, carry):
        key_tile(kb, causal=False)
        return carry

    lax.fori_loop(0, n_full, full_tile, 0)
    key_tile(n_full, causal=True)
    o_sel_t = acc_ref[...] / l_ref[...]

    span = W + tq
    w0 = pl.multiple_of(jnp.maximum(t0 - W, 0), tq)
    kw = kw_ref[0, pl.ds(w0, span), :]
    vw_t = vwt_ref[0, :, pl.ds(w0, span)]
    wpos = w0 + lax.broadcasted_iota(jnp.int32, (span, 1), 0)
    bias_w = jnp.where((wpos <= t_q) & (wpos > t_q - W), 0.0, NEG_INF)
    s_all = jnp.dot(kw, qt, preferred_element_type=F32)
    es, ls = [], []
    for hh in heads:
        s = s_all[:, cols[hh]] + bias_w
        e = jnp.exp2(s - jnp.max(s, axis=0, keepdims=True))
        ls.append(jnp.sum(e, axis=0, keepdims=True))
        es.append(e.astype(BF16))
    o_win_t = (jnp.dot(vw_t, jnp.concatenate(es, axis=1), preferred_element_type=F32)
               / jnp.concatenate(ls, axis=1))

    gt = jax.nn.sigmoid(gt_ref[0, 0])
    gt_t = jax.nn.sigmoid(gtt_ref[0, 0])
    outs = []
    for hh in heads:
        mixed_t = (gt_t[hpg + hh:hpg + hh + 1, :] * o_sel_t[:, cols[hh]]
                   + gt_t[2 * hpg + hh:2 * hpg + hh + 1, :] * o_win_t[:, cols[hh]])
        outs.append(gt[:, hh:hh + 1] * oc_ref[0, :, hh * hd:(hh + 1) * hd] + mixed_t.T)
    o_ref[0] = jnp.concatenate(outs, axis=-1).astype(o_ref.dtype)


def _rms_norm(x, g):
    return x * lax.rsqrt(jnp.mean(x * x, axis=-1, keepdims=True) + NORM_EPS) * g


def _nsa_mix(h, x_res, gate, w_in, q_g, k_g, cmp_pos, cmp_w1, cmp_w2, w_o):
    B, T, D = h.shape
    M = B * T
    H, G, hd = NSA_HEADS, NSA_KV_HEADS, NSA_HD
    hpg = H // G
    L, S, Ls = NSA_CMP_BLK, NSA_CMP_STRIDE, NSA_SEL_BLK
    kvw = G * hd
    n_main = H * hd + 6 * kvw
    h2 = h.reshape(M, D)
    proj = _matmul(h2, w_in, (0,), n_cols=n_main)
    gate_w = jnp.pad(w_in[0, :, n_main:], ((0, 0), (0, 128 - 3 * H)))
    gates = _matmul(h2, gate_w)[:, :3 * H]
    q = _rms_norm(proj[:, :H * hd].reshape(B, T, H, hd), q_g) * (hd ** -0.5)
    q = q.reshape(B, T, H * hd)

    def kv(i):
        return proj[:, H * hd + i * kvw:H * hd + (i + 1) * kvw].reshape(B, T, G, hd)

    n_cmp = (T - L) // S + 1
    n_grp = T // S
    assert L == 2 * S and n_cmp == n_grp - 1

    def compress(z, pos, w1, w2):
        zg = z.reshape(B, n_grp, S, G, hd).transpose(0, 3, 1, 2, 4).reshape(B * G * n_grp, S * hd)
        w_halves = jnp.concatenate([w1[:S * hd], w1[S * hd:]], axis=1)
        part = _matmul(zg, w_halves, tm=min(512, B * G * n_grp)).reshape(B, G, n_grp, 2 * hd)
        pos_term = _matmul(jnp.broadcast_to(pos.reshape(1, L * hd), (8, L * hd)), w1, tm=8)[0]
        pre = part[:, :, :-1, :hd] + part[:, :, 1:, hd:] + pos_term
        pre = jnp.pad(pre, ((0, 0), (0, 0), (0, 1), (0, 0))).reshape(B * G * n_grp, hd)
        return _matmul(jax.nn.silu(pre), w2, tm=min(512, B * G * n_grp)).reshape(B, G, n_grp, hd)

    kc = _rms_norm(compress(kv(0), cmp_pos[0], cmp_w1[0], cmp_w2[0]), k_g[0])
    vc = compress(kv(1), cmp_pos[1], cmp_w1[1], cmp_w2[1])
    k_feat = 2 * hd
    blk_one_hot = (jnp.arange(T)[:, None] // Ls == jnp.arange(k_feat - hd)[None, :]).astype(BF16)
    k_sel = jnp.concatenate([_rms_norm(kv(2), k_g[1]).astype(BF16),
                             jnp.broadcast_to(blk_one_hot[None, :, None, :], (B, T, G, k_feat - hd))],
                            axis=-1).reshape(B, T, G * k_feat)
    v_sel = kv(3).reshape(B, T, kvw).astype(BF16)
    k_win = _rms_norm(kv(4), k_g[2]).reshape(B, T, kvw).astype(BF16)
    v_win = kv(5).reshape(B, T, kvw).astype(BF16)

    n_slc = T // Ls
    tq = min(NSA_CMP_TQ, T)
    o_cmp, sel = pl.pallas_call(
        functools.partial(_nsa_cmp_kernel, tq=tq, n_cmp=n_cmp, n_pad=n_grp, n_slc=n_slc, hpg=hpg),
        grid=(B, G, T // tq),
        in_specs=[
            pl.BlockSpec((1, tq, hpg * hd), lambda b, g, i: (b, i, g)),
            pl.BlockSpec((1, 1, n_grp, hd), lambda b, g, i: (b, g, 0, 0)),
            pl.BlockSpec((1, 1, n_grp, hd), lambda b, g, i: (b, g, 0, 0)),
        ],
        out_specs=[
            pl.BlockSpec((1, tq, hpg * hd), lambda b, g, i: (b, i, g)),
            pl.BlockSpec((1, 1, n_slc, tq), lambda b, g, i: (b, g, 0, i)),
        ],
        out_shape=[jax.ShapeDtypeStruct((B, T, H * hd), F32),
                   jax.ShapeDtypeStruct((B, G, n_slc, T), F32)],
        compiler_params=_params("parallel", "parallel", "parallel"),
        name="nsa_compressed",
    )(q, kc, vc)

    gt = gates.reshape(B, T, 3, G, hpg).transpose(0, 3, 1, 2, 4).reshape(B, G, T, 3 * hpg)
    tq = min(NSA_ATT_TQ, T)
    tk = min(NSA_ATT_TK, T)
    assert T >= NSA_WINDOW + tq
    k_spec = pl.BlockSpec((1, T, hd), lambda b, g, i: (b, 0, g))
    vt_spec = pl.BlockSpec((1, hd, T), lambda b, g, i: (b, g, 0))

    def feature_major(z):
        return jnp.swapaxes(z, 1, 2)

    o = pl.pallas_call(
        functools.partial(_nsa_att_kernel, tq=tq, tk=tk, T=T, n_slc=n_slc, hpg=hpg),
        grid=(B, G, T // tq),
        in_specs=[
            pl.BlockSpec((1, hpg * hd, tq), lambda b, g, i: (b, g, i)),
            pl.BlockSpec((1, T, k_feat), lambda b, g, i: (b, 0, g)), vt_spec, k_spec, vt_spec,
            pl.BlockSpec((1, 1, n_slc, tq), lambda b, g, i: (b, g, 0, i)),
            pl.BlockSpec((1, tq, hpg * hd), lambda b, g, i: (b, i, g)),
            pl.BlockSpec((1, 1, tq, 3 * hpg), lambda b, g, i: (b, g, i, 0)),
            pl.BlockSpec((1, 1, 3 * hpg, tq), lambda b, g, i: (b, g, 0, i)),
        ],
        out_specs=pl.BlockSpec((1, tq, hpg * hd), lambda b, g, i: (b, i, g)),
        out_shape=jax.ShapeDtypeStruct((B, T, H * hd), BF16),
        scratch_shapes=[pltpu.VMEM((1, hpg * tq), F32), pltpu.VMEM((1, hpg * tq), F32),
                        pltpu.VMEM((hd, hpg * tq), F32)],
        compiler_params=_params("parallel", "parallel", "arbitrary"),
        name="nsa_selected_window",
    )(feature_major((q * math.log2(math.e)).astype(BF16)), k_sel, feature_major(v_sel), k_win, feature_major(v_win), sel,
      o_cmp, gt, jnp.swapaxes(gt, 2, 3))
    return _matmul(o.reshape(M, H * hd), w_o, (0,), res=x_res.reshape(M, D), gate=gate).reshape(B, T, D)


def _clamped_swiglu(gu):
    x_glu = jnp.minimum(gu[:, :D_EXPERT], SWIGLU_LIMIT)
    x_lin = jnp.clip(gu[:, D_EXPERT:], -SWIGLU_LIMIT, SWIGLU_LIMIT)
    return x_glu * jax.nn.sigmoid(SWIGLU_ALPHA * x_glu) * (x_lin + 1.0)


def _expert_kernel(blk_e_ref, first_ref, next_ref, used_ref, x_ref, w_hbm, b_ref, o_ref, stage, w_bf16,
                   sem, *, layer, post):
    i = pl.program_id(0)

    def fetch(expert):
        return pltpu.make_async_copy(w_hbm.at[layer, expert], stage, sem.at[0])

    @pl.when(i == 0)
    def _():
        fetch(blk_e_ref[0]).start()

    @pl.when(first_ref[i] == 1)
    def _():
        fetch(blk_e_ref[i]).wait()
        w_bf16[...] = stage[...].astype(BF16)

        @pl.when(next_ref[i] >= 0)
        def _():
            fetch(next_ref[i]).start()

    @pl.when(used_ref[i] == 1)
    def _():
        acc = jnp.dot(x_ref[...], w_bf16[...], preferred_element_type=F32) + b_ref[0]
        o_ref[...] = (acc if post is None else post(acc)).astype(o_ref.dtype)

    @pl.when(used_ref[i] == 0)
    def _():
        o_ref[...] = jnp.zeros_like(o_ref)


def _expert_matmul(tables, x, w, b, layer, post, n_out, out_dtype, name):
    cap, Kd = x.shape
    E, Nd = w.shape[1], w.shape[3]
    R = MOE_ROWS
    return pl.pallas_call(
        functools.partial(_expert_kernel, layer=layer, post=post),
        grid_spec=pltpu.PrefetchScalarGridSpec(
            num_scalar_prefetch=4,
            grid=(cap // R,),
            in_specs=[
                pl.BlockSpec((R, Kd), lambda i, be, fi, nx, us: (i, 0)),
                pl.BlockSpec(memory_space=pl.ANY),
                pl.BlockSpec((None, 1, 1, Nd), lambda i, be, fi, nx, us: (layer, be[i], 0, 0)),
            ],
            out_specs=pl.BlockSpec((R, n_out), lambda i, be, fi, nx, us: (i, 0)),
            scratch_shapes=[pltpu.VMEM((Kd, Nd), F32), pltpu.VMEM((Kd, Nd), BF16),
                            pltpu.SemaphoreType.DMA((1,))],
        ),
        out_shape=jax.ShapeDtypeStruct((cap, n_out), out_dtype),
        compiler_params=_params("arbitrary"),
        name=name,
    )(*tables, x, w, b.reshape(b.shape[0], E, 1, Nd))


def _expert_up_kernel(blk_e_ref, first_ref, next_ref, used_ref, n_used_ref, tok_ref, h_hbm, w_hbm, b_ref,
                      o_ref, x_buf, x_sem, stage, w_bf16, w_sem, *, layer, R):
    i = pl.program_id(0)
    n_steps = pl.num_programs(0)

    def fetch_rows(block, par):
        for r in range(R):
            pltpu.make_async_copy(h_hbm.at[pl.ds(tok_ref[block * R + r], 1)], x_buf.at[par, pl.ds(r, 1)],
                                  x_sem.at[par]).start()

    def wait_rows(par):
        pltpu.make_async_copy(h_hbm.at[pl.ds(0, R)], x_buf.at[par], x_sem.at[par]).wait()

    def fetch_weights(expert):
        return pltpu.make_async_copy(w_hbm.at[layer, expert], stage, w_sem.at[0])

    @pl.when(i == 0)
    def _():
        fetch_rows(0, 0)
        fetch_weights(blk_e_ref[0]).start()

    @pl.when(first_ref[i] == 1)
    def _():
        fetch_weights(blk_e_ref[i]).wait()
        w_bf16[...] = stage[...].astype(BF16)

        @pl.when(next_ref[i] >= 0)
        def _():
            fetch_weights(next_ref[i]).start()

    @pl.when(used_ref[i] == 1)
    def _():
        par = i % 2
        wait_rows(par)
        fetch_rows((i + 1) % n_steps, 1 - par)
        acc = jnp.dot(x_buf[par].astype(BF16), w_bf16[...], preferred_element_type=F32) + b_ref[0]
        o_ref[...] = _clamped_swiglu(acc).astype(o_ref.dtype)

    @pl.when(used_ref[i] == 0)
    def _():
        o_ref[...] = jnp.zeros_like(o_ref)

    @pl.when(i == n_steps - 1)
    def _():
        wait_rows(n_used_ref[0] % 2)


def _expert_up(tables, n_used, tok, h, w, b, layer):
    N, D = h.shape
    E, Nd = w.shape[1], w.shape[3]
    R = MOE_ROWS
    cap = tok.shape[0]
    assert (cap // R) % 2 == 0
    n_prefetch = len(tables) + 2

    def per_expert(i, be, *_):
        return (layer, be[i], 0, 0)

    return pl.pallas_call(
        functools.partial(_expert_up_kernel, layer=layer, R=R),
        grid_spec=pltpu.PrefetchScalarGridSpec(
            num_scalar_prefetch=n_prefetch,
            grid=(cap // R,),
            in_specs=[
                pl.BlockSpec(memory_space=pl.ANY),
                pl.BlockSpec(memory_space=pl.ANY),
                pl.BlockSpec((None, 1, 1, Nd), per_expert),
            ],
            out_specs=pl.BlockSpec((R, D_EXPERT), lambda i, *_: (i, 0)),
            scratch_shapes=[pltpu.VMEM((2, R, D), F32), pltpu.SemaphoreType.DMA((2,)),
                            pltpu.VMEM((D, Nd), F32), pltpu.VMEM((D, Nd), BF16),
                            pltpu.SemaphoreType.DMA((1,))],
        ),
        out_shape=jax.ShapeDtypeStruct((cap, D_EXPERT), BF16),
        compiler_params=_params("arbitrary"),
        name="moe_up",
    )(*tables, n_used, tok, h, w, b.reshape(b.shape[0], E, 1, Nd))


def _router_kernel(x_ref, w_ref, b_ref, et_ref, p_ref, cnt_ref):
    logits = _fdot(x_ref[...], w_ref[...]) + b_ref[...]
    lane = lax.broadcasted_iota(jnp.int32, logits.shape, 1)
    vals, idxs = [], []
    for _ in range(TOP_K):
        m = jnp.max(logits, axis=-1, keepdims=True)
        idx = jnp.min(jnp.where(logits == m, lane, logits.shape[1]), axis=-1, keepdims=True)
        vals.append(m)
        idxs.append(idx)
        logits = jnp.where(lane == idx, -IMP_BIG, logits)
    es = [jnp.exp(v - vals[0]) for v in vals]
    total = sum(es)
    e_out = jnp.zeros(logits.shape, jnp.int32)
    p_out = jnp.zeros(logits.shape, F32)
    picks = jnp.zeros(logits.shape, jnp.int32)
    for k in range(TOP_K):
        e_out = jnp.where(lane == k, idxs[k], e_out)
        p_out = jnp.where(lane == k, es[k] / total, p_out)
        picks = picks + (lane == idxs[k]).astype(jnp.int32)
    et_ref[...] = e_out.T[:ROUTER_ROWS]
    p_ref[...] = p_out
    cnt_ref[0] = jnp.sum(picks, axis=0, keepdims=True)


def _route(x, w_router, b_router, layer):
    N, D = x.shape
    E, K = N_EXPERTS, TOP_K
    lanes = 128
    w_r = jnp.pad(w_router[layer], ((0, 0), (0, lanes - E)))
    b_r = jnp.concatenate([b_router[layer], jnp.full((lanes - E,), NEG_INF, F32)]).reshape(1, lanes)
    tm = min(N, 512)
    return pl.pallas_call(
        _router_kernel,
        grid=(N // tm,),
        in_specs=[pl.BlockSpec((tm, D), lambda i: (i, 0)),
                  pl.BlockSpec((D, lanes), lambda i: (0, 0)),
                  pl.BlockSpec((1, lanes), lambda i: (0, 0))],
        out_specs=[pl.BlockSpec((ROUTER_ROWS, tm), lambda i: (0, i)),
                   pl.BlockSpec((tm, lanes), lambda i: (i, 0)),
                   pl.BlockSpec((1, 1, lanes), lambda i: (i, 0, 0))],
        out_shape=[jax.ShapeDtypeStruct((ROUTER_ROWS, N), jnp.int32),
                   jax.ShapeDtypeStruct((N, lanes), F32),
                   jax.ShapeDtypeStruct((N // tm, 1, lanes), jnp.int32)],
        compiler_params=_params("parallel"),
        name="moe_router",
    )(x, w_r, b_r)


def _combine_kernel(slot_ref, out_hbm, p_ref, res_ref, gate_ref, o_ref, buf, sem, *, TT, K, N):
    i = pl.program_id(0)
    n_steps = pl.num_programs(0)

    def fetch(step, par):
        for k in range(K):
            for r in range(TT):
                row = slot_ref[k * N + step * TT + r]
                pltpu.make_async_copy(out_hbm.at[pl.ds(row, 1)], buf.at[par, pl.ds(k * TT + r, 1)],
                                      sem.at[par]).start()

    @pl.when(i == 0)
    def _():
        fetch(0, 0)

    @pl.when(i + 1 < n_steps)
    def _():
        fetch(i + 1, (i + 1) % 2)

    par = i % 2
    pltpu.make_async_copy(out_hbm.at[pl.ds(0, K * TT)], buf.at[par], sem.at[par]).wait()
    p = p_ref[...]
    y = p[:, 0:1] * buf[par, 0:TT, :]
    for k in range(1, K):
        y = y + p[:, k:k + 1] * buf[par, k * TT:(k + 1) * TT, :]
    o_ref[...] = res_ref[...] + gate_ref[0] * y


def _moe_combine(out, slot, top_w, x_res, gate, T):
    N, D = x_res.shape
    K, TT = TOP_K, MOE_COMBINE_TOKENS
    return pl.pallas_call(
        functools.partial(_combine_kernel, TT=TT, K=K, N=N),
        grid_spec=pltpu.PrefetchScalarGridSpec(
            num_scalar_prefetch=1,
            grid=(N // TT,),
            in_specs=[
                pl.BlockSpec(memory_space=pl.ANY),
                pl.BlockSpec((TT, top_w.shape[1]), lambda i, s: (i, 0)),
                pl.BlockSpec((TT, D), lambda i, s: (i, 0)),
                pl.BlockSpec((1, 1, D), lambda i, s: ((i * TT) // T, 0, 0)),
            ],
            out_specs=pl.BlockSpec((TT, D), lambda i, s: (i, 0)),
            scratch_shapes=[pltpu.VMEM((2, K * TT, D), F32), pltpu.SemaphoreType.DMA((2,))],
        ),
        out_shape=jax.ShapeDtypeStruct((N, D), F32),
        compiler_params=_params("arbitrary"),
        name="moe_combine",
    )(slot, out, top_w, x_res, gate)


def _moe_ffn(h, layer, x_res, gate, w_router, b_router, w_gu, b_gu, w_down, b_down):
    B, T, D = h.shape
    N = B * T
    E, K, R = N_EXPERTS, TOP_K, MOE_ROWS
    NK = N * K
    x = h.reshape(N, D)
    top_e_t, top_w, tile_counts = _route(x, w_router, b_router, layer)
    pair = jnp.arange(NK, dtype=jnp.int32)
    e_s, order = lax.sort((top_e_t[:K].reshape(-1), pair), num_keys=1, is_stable=True)
    counts = jnp.sum(tile_counts, axis=(0, 1))[:E]
    padded = (counts + R - 1) // R * R
    g_start = jnp.cumsum(counts) - counts
    p_end = jnp.cumsum(padded)
    shift = (p_end - padded) - g_start
    dest = pair + jnp.take(shift, e_s, axis=0)
    n_blocks = (NK + R - 1) // R + E
    cap = n_blocks * R
    experts = jnp.arange(E, dtype=jnp.int32)[None, :]
    blocks = jnp.arange(n_blocks, dtype=jnp.int32)
    blk_e = jnp.minimum(jnp.sum(p_end[None, :] <= (blocks * R)[:, None], axis=1), E - 1).astype(jnp.int32)
    blk_is = blk_e[:, None] == experts
    blk_shift = jnp.sum(jnp.where(blk_is, shift[None, :], 0), axis=1)
    used = (blocks * R < p_end[E - 1]).astype(jnp.int32)
    first = used * (blk_e != jnp.concatenate([jnp.full((1,), -1, jnp.int32), blk_e[:-1]])).astype(jnp.int32)
    after = jnp.sum(jnp.where(blk_is, p_end[None, :], 0), axis=1) // R
    after_e = jnp.sum(jnp.where(after[:, None] == blocks[None, :], blk_e[None, :], 0), axis=1)
    next_e = jnp.where(after * R < p_end[E - 1], after_e, -1).astype(jnp.int32)
    tables = (blk_e, first, next_e, used)
    src = jnp.clip(jnp.arange(cap, dtype=jnp.int32) - jnp.repeat(blk_shift, R), 0, NK - 1)
    tok = jnp.take(order, src, axis=0) % N
    n_used = jnp.sum(used, keepdims=True)
    act = _expert_up(tables, n_used, tok, x, w_gu, b_gu, layer)
    out = _expert_matmul(tables, act, w_down, b_down, layer, None, D, F32, "moe_down")

    _, slot = lax.sort((order, dest), num_keys=1)
    return _moe_combine(out, slot, top_w, x_res.reshape(N, D), gate, T).reshape(B, T, D)


def kernel(x, c, ada_w, ada_b, norm_g, rw_mu, rw_w_rkv, rw_w0, rw_w1, rw_w2, rw_a0, rw_a1, rw_a2, rw_g1, rw_g2, rw_k_k, rw_k_a, rw_r_k, rw_ln_g, rw_ln_b, rw_w_o, ret_w_in, ret_gn_g, ret_gn_b, ret_w_o, gla_w_in, gla_w_a1, gla_w_a2, gla_b_a, gla_gn_g, gla_w_o, nsa_w_in, nsa_q_g, nsa_k_g, nsa_cmp_pos, nsa_cmp_w1, nsa_cmp_w2, nsa_w_o, moe_router_w, moe_router_b, moe_w_gu, moe_b_gu, moe_w_down, moe_b_down):
    B, T, D = x.shape
    depth = ada_w.shape[0]
    c_act = jnp.pad(jax.nn.silu(c), ((0, 8 - B), (0, 0)))
    for i in range(depth):
        mod = _matmul(c_act, ada_w, (i,), tm=8, tn=1024)[:B] + ada_b[i]
        sh1, sc1, gt1, sh2, sc2, gt2 = jnp.split(mod, 6, axis=-1)
        gt1, gt2 = gt1.reshape(B, 1, D), gt2.reshape(B, 1, D)
        m, j = i % 4, i // 4
        if m == 0:
            x = _rwkv7_mix(x, norm_g[i, 0], sc1, sh1, gt1, rw_mu[j], rw_w_rkv[j:j + 1], rw_w0[j],
                           rw_w1[j:j + 1], rw_w2[j:j + 1], rw_a0[j], rw_a1[j:j + 1], rw_a2[j:j + 1],
                           rw_g1[j:j + 1], rw_g2[j:j + 1], rw_k_k[j], rw_k_a[j], rw_r_k[j], rw_ln_g[j],
                           rw_ln_b[j], rw_w_o[j:j + 1])
        elif m == 1:
            h, = _norm_modulate(x, norm_g[i, 0], sc1, sh1, (BF16,))
            x = _retention_mix(h, x, gt1, ret_w_in[j:j + 1], ret_gn_g[j], ret_gn_b[j], ret_w_o[j:j + 1])
        elif m == 2:
            h, = _norm_modulate(x, norm_g[i, 0], sc1, sh1, (BF16,))
            x = _gla_mix(h, x, gt1, gla_w_in[j:j + 1], gla_w_a1[j:j + 1], gla_w_a2[j:j + 1], gla_b_a[j],
                         gla_gn_g[j], gla_w_o[j:j + 1])
        else:
            h, = _norm_modulate(x, norm_g[i, 0], sc1, sh1, (BF16,))
            x = _nsa_mix(h, x, gt1, nsa_w_in[j:j + 1], nsa_q_g[j], nsa_k_g[j], nsa_cmp_pos[j],
                         nsa_cmp_w1[j], nsa_cmp_w2[j], nsa_w_o[j:j + 1])
        h, = _norm_modulate(x, norm_g[i, 1], sc2, sh2, (F32,))
        x = _moe_ffn(h, i, x, gt2, moe_router_w, moe_router_b, moe_w_gu, moe_b_gu, moe_w_down,
                     moe_b_down)
    return x
```
